```python
import math
import jax, jax.numpy as jnp
from jax import lax
import numpy as np

D_MODEL = 1024
BATCH = 1
SEQ = 16384
DEPTH = 2

HG_HEADS = 4
HG_DIM = 128
HG_WIDTH = HG_HEADS * HG_DIM
HG_CHUNK = 64
AT_HEADS = 4
AT_DIM = 64
AT_WIDTH = AT_HEADS * AT_DIM
MOBA_BLOCK = 256
MOBA_TOPK = 3
MOBA_QBLOCK = 64
REL_BUCKETS = 32
REL_MAX_DIST = 1024
CV_GROUPS = 4
CV_WIDTH = CV_GROUPS * 64
CV_KERNEL = 3
D_MIX = HG_WIDTH + AT_WIDTH + CV_WIDTH
D_IN = 4 * HG_WIDTH + 3 * AT_WIDTH + 3 * CV_WIDTH
D_FF = 2816
EPS = 1e-6

kernel_name = 'hybrid_hgrn2_moba_shortconv'


def rms_norm(x, w):
    xf = x.astype(jnp.float32)
    y = xf * lax.rsqrt(jnp.mean(xf * xf, axis=-1, keepdims=True) + EPS)
    return (y * w.astype(jnp.float32)).astype(x.dtype)


def swiglu(x, wg, wu, wd):
    return (jax.nn.silu(x @ wg) * (x @ wu)) @ wd


def t5_bucket(dist):
    dist = jnp.maximum(dist, 0)
    max_exact = REL_BUCKETS // 2
    d = jnp.maximum(dist, 1).astype(jnp.float32)
    large = max_exact + (jnp.log(d / max_exact) / math.log(REL_MAX_DIST / max_exact)
                         * (REL_BUCKETS - max_exact)).astype(jnp.int32)
    large = jnp.minimum(large, REL_BUCKETS - 1)
    return jnp.where(dist < max_exact, dist, large)


def hgrn2_mixer(q, f_pre, i, g, lb, norm_w):
    B, S, _ = q.shape
    f32 = jnp.float32
    nc = S // HG_CHUNK
    fp = f_pre.astype(f32)
    lbf = lb.astype(f32)
    log_f = jnp.log(lbf + (1.0 - lbf) * jax.nn.sigmoid(fp))
    k = (1.0 - lbf) * jax.nn.sigmoid(-fp)

    def to_chunks(t):
        return t.astype(f32).reshape(B, nc, HG_CHUNK, HG_HEADS, HG_DIM).transpose(1, 0, 3, 2, 4)

    qc, kc, vc, lc = to_chunks(q), to_chunks(k), to_chunks(i), to_chunks(log_f)
    causal = jnp.tril(jnp.ones((HG_CHUNK, HG_CHUNK), dtype=bool))

    def step(state, xs):
        qt, kt, vt, lt = xs
        b = jnp.cumsum(lt, axis=2)
        diff = b[:, :, :, None, :] - b[:, :, None, :, :]
        decay = jnp.exp(jnp.where(causal[:, :, None], diff, -jnp.inf))
        scores = jnp.einsum('bhtk,bhsk,bhtsk->bhts', qt, kt, decay)
        o = (jnp.einsum('bhts,bhsv->bhtv', scores, vt)
             + jnp.einsum('bhtk,bhkv->bhtv', qt * jnp.exp(b), state))
        b_last = b[:, :, -1:, :]
        new_state = (jnp.exp(b_last[:, :, 0, :, None]) * state
                     + jnp.einsum('bhsk,bhsv->bhkv', kt * jnp.exp(b_last - b), vt))
        return new_state, o

    s0 = jnp.zeros((B, HG_HEADS, HG_DIM, HG_DIM), f32)
    _, o = lax.scan(step, s0, (qc, kc, vc, lc))
    o = o.transpose(1, 0, 3, 2, 4).reshape(B, S, HG_HEADS, HG_DIM)
    gate = jax.nn.silu(g.astype(f32).reshape(B, S, HG_HEADS, HG_DIM))
    o = rms_norm(o, norm_w) * gate
    return o.reshape(B, S, HG_WIDTH)


def moba_mixer(q, k, v, rel_bias):
    B, S, _ = q.shape
    f32 = jnp.float32
    nblk = -(-S // MOBA_BLOCK)
    s_pad = nblk * MOBA_BLOCK
    pad = ((0, 0), (0, s_pad - S), (0, 0))

    def heads(t):
        return jnp.pad(t.astype(f32), pad).reshape(B, s_pad, AT_HEADS, AT_DIM)

    q, k, v = heads(q), heads(k), heads(v)
    scale = AT_DIM ** -0.5
    kb = k.reshape(B, nblk, MOBA_BLOCK, AT_HEADS, AT_DIM).transpose(0, 3, 1, 2, 4)
    vb = v.reshape(B, nblk, MOBA_BLOCK, AT_HEADS, AT_DIM).transpose(0, 3, 1, 2, 4)
    k_mean = jnp.mean(kb, axis=3)
    gate = jnp.einsum('bshd,bhnd->bshn', q, k_mean)
    own = jnp.arange(s_pad) // MOBA_BLOCK
    past = jnp.arange(nblk)[None, :] < own[:, None]
    gate = jnp.where(past[None, :, None, :], gate, -jnp.inf)
    topk = min(MOBA_TOPK, nblk)
    _, sel = lax.top_k(gate, topk)
    sel_valid = sel < own[None, :, None, None]

    bias_hn = rel_bias.T.astype(f32)
    b_idx = jnp.arange(B)[:, None, None, None]
    h_idx = jnp.arange(AT_HEADS)[None, None, :, None]
    offs = jnp.arange(MOBA_BLOCK)
    n_qb = s_pad // MOBA_QBLOCK

    def qblock(j):
        start = j * MOBA_QBLOCK
        q_c = lax.dynamic_slice_in_dim(q, start, MOBA_QBLOCK, axis=1)
        sel_c = lax.dynamic_slice_in_dim(sel, start, MOBA_QBLOCK, axis=1)
        val_c = lax.dynamic_slice_in_dim(sel_valid, start, MOBA_QBLOCK, axis=1)
        q_pos = start + jnp.arange(MOBA_QBLOCK)
        own_blk = start // MOBA_BLOCK
        k_sel = kb[b_idx, h_idx, sel_c]
        v_sel = vb[b_idx, h_idx, sel_c]
        k_pos = sel_c[..., None] * MOBA_BLOCK + offs
        dist_sel = q_pos[None, :, None, None, None] - k_pos
        s_sel = (jnp.einsum('bqhd,bqhnkd->bqhnk', q_c, k_sel) * scale
                 + bias_hn[h_idx[..., None], t5_bucket(dist_sel)])
        s_sel = jnp.where(val_c[..., None], s_sel, -jnp.inf)
        k_own = lax.dynamic_index_in_dim(kb, own_blk, axis=2, keepdims=False)
        v_own = lax.dynamic_index_in_dim(vb, own_blk, axis=2, keepdims=False)
        dist_own = q_pos[:, None] - (own_blk * MOBA_BLOCK + offs)[None, :]
        bias_own = bias_hn[:, t5_bucket(dist_own)].transpose(1, 0, 2)[None]
        s_own = jnp.einsum('bqhd,bhkd->bqhk', q_c, k_own) * scale + bias_own
        s_own = jnp.where((dist_own >= 0)[None, :, None, :], s_own, -jnp.inf)
        logits = jnp.concatenate(
            [s_sel.reshape(B, MOBA_QBLOCK, AT_HEADS, topk * MOBA_BLOCK), s_own], axis=-1)
        p = jax.nn.softmax(logits, axis=-1)
        p_sel = p[..., :topk * MOBA_BLOCK].reshape(B, MOBA_QBLOCK, AT_HEADS, topk, MOBA_BLOCK)
        p_own = p[..., topk * MOBA_BLOCK:]
        return (jnp.einsum('bqhnk,bqhnkd->bqhd', p_sel, v_sel)
                + jnp.einsum('bqhk,bhkd->bqhd', p_own, v_own))

    out = lax.map(qblock, jnp.arange(n_qb))
    out = out.transpose(1, 0, 2, 3, 4).reshape(B, s_pad, AT_WIDTH)
    return out[:, :S]


def short_conv_mixer(b_gate, c_gate, h, conv_w):
    u = c_gate * h
    y = lax.conv_general_dilated(u, conv_w[:, None, :], window_strides=(1,),
                                 padding=[(CV_KERNEL - 1, 0)],
                                 dimension_numbers=('NWC', 'WIO', 'NWC'),
                                 feature_group_count=CV_WIDTH)
    return b_gate * y


def hybrid_mixer(h, w_in, w_out, lb, hg_norm_w, conv_w, rel_bias):
    z = h @ w_in
    splits = np.cumsum([HG_WIDTH] * 4 + [AT_WIDTH] * 3 + [CV_WIDTH] * 2).tolist()
    hq, hf, hi, hg, aq, ak, av, cb, cc, ch = jnp.split(z, splits, axis=-1)
    o_hg = hgrn2_mixer(hq, hf, hi, hg, lb, hg_norm_w).astype(h.dtype)
    o_at = moba_mixer(aq, ak, av, rel_bias).astype(h.dtype)
    o_cv = short_conv_mixer(cb, cc, ch, conv_w)
    o = jnp.concatenate([o_hg, o_at, o_cv], axis=-1)
    return o @ w_out


def setup_inputs(seed: int = 0) -> dict:
    key = jax.random.key(seed)
    ks = jax.random.split(key, 16)
    f32 = jnp.float32

    def nrm(k, shape, fan_in):
        return jax.random.normal(k, shape, f32) * (fan_in ** -0.5)

    return {
        'x': jax.random.normal(ks[0], (BATCH, SEQ, D_MODEL), f32),
        'norm_w': 1.0 + 0.05 * jax.random.normal(ks[1], (DEPTH, 6, D_MODEL), f32),
        'ffn1_wg': nrm(ks[2], (DEPTH, D_MODEL, D_FF), D_MODEL),
        'ffn1_wu': nrm(ks[3], (DEPTH, D_MODEL, D_FF), D_MODEL),
        'ffn1_wd': nrm(ks[4], (DEPTH, D_FF, D_MODEL), D_FF),
        'mix_w_in': nrm(ks[5], (DEPTH, D_MODEL, D_IN), D_MODEL),
        'mix_w_out': nrm(ks[6], (DEPTH, D_MIX, D_MODEL), D_MIX),
        'hg_lb': jax.random.normal(ks[7], (DEPTH, HG_WIDTH), f32),
        'hg_norm_w': 1.0 + 0.05 * jax.random.normal(ks[8], (DEPTH, HG_DIM), f32),
        'conv_w': nrm(ks[9], (DEPTH, CV_KERNEL, CV_WIDTH), CV_KERNEL),
        'ffn2_wg': nrm(ks[10], (DEPTH, D_MODEL, D_FF), D_MODEL),
        'ffn2_wu': nrm(ks[11], (DEPTH, D_MODEL, D_FF), D_MODEL),
        'ffn2_wd': nrm(ks[12], (DEPTH, D_FF, D_MODEL), D_FF),
        'rel_bias': 0.5 * jax.random.normal(ks[13], (REL_BUCKETS, AT_HEADS), f32),
    }


def reference(x, norm_w, ffn1_wg, ffn1_wu, ffn1_wd, mix_w_in, mix_w_out, hg_lb, hg_norm_w,
              conv_w, ffn2_wg, ffn2_wu, ffn2_wd, rel_bias):
    lb_soft = jax.nn.softmax(hg_lb.astype(jnp.float32), axis=0)
    lower_bounds = jnp.cumsum(lb_soft, axis=0) - lb_soft[0]
    for l in range(DEPTH):
        h = swiglu(rms_norm(x, norm_w[l, 0]), ffn1_wg[l], ffn1_wu[l], ffn1_wd[l])
        x = x + 0.5 * rms_norm(h, norm_w[l, 1])
        h = hybrid_mixer(rms_norm(x, norm_w[l, 2]), mix_w_in[l], mix_w_out[l], lower_bounds[l],
                         hg_norm_w[l], conv_w[l], rel_bias)
        x = x + rms_norm(h, norm_w[l, 3])
        h = swiglu(rms_norm(x, norm_w[l, 4]), ffn2_wg[l], ffn2_wu[l], ffn2_wd[l])
        x = x + 0.5 * rms_norm(h, norm_w[l, 5])
    return x
```

```python
import functools
import math

import numpy as np
import jax
import jax.numpy as jnp
from jax import lax
from jax.experimental import pallas as pl
from jax.experimental.pallas import tpu as pltpu

F32 = jnp.float32
BF16 = jnp.bfloat16

D_MODEL = 1024
D_FF = 2816
HG_HEADS = 4
HG_DIM = 128
HG_WIDTH = HG_HEADS * HG_DIM
AT_HEADS = 4
AT_DIM = 64
AT_WIDTH = AT_HEADS * AT_DIM
MOBA_BLOCK = 256
MOBA_TOPK = 3
REL_BUCKETS = 32
REL_MAX_DIST = 1024
CV_WIDTH = 256
CV_KERNEL = 3
D_MIX = HG_WIDTH + AT_WIDTH + CV_WIDTH
D_IN = 4 * HG_WIDTH + 3 * AT_WIDTH + 3 * CV_WIDTH
EPS = 1e-6

ROW_TILE = 512
HG_TILE = 256
KSUM_ROWS = 8
NEG = -1e30
VMEM_LIMIT = 56 * 1024 * 1024
FF_CHUNKS = ((0, 768), (768, 1536), (1536, 2304), (2304, 2816))

_NT = (((1,), (1,)), ((), ()))
_TN = (((0,), (0,)), ((), ()))


def _rms(x, w):
    ms = jnp.mean(x * x, axis=-1, keepdims=True)
    return x * lax.rsqrt(ms + EPS) * w


def _const_spec(shape):
    nd = len(shape)
    return pl.BlockSpec(shape, lambda *_: (0,) * nd, pipeline_mode=pl.Buffered(1))


def _params(*sem):
    return pltpu.CompilerParams(dimension_semantics=sem, vmem_limit_bytes=VMEM_LIMIT)


def _ffn_kernel(x_ref, nw_ref, wg_ref, wu_ref, wd_ref, o_ref):
    x = x_ref[...]
    xn = _rms(x, nw_ref[0:1, :]).astype(BF16)
    h = None
    for c0, c1 in FF_CHUNKS:
        g = jnp.dot(xn, wg_ref[:, c0:c1], preferred_element_type=F32)
        u = jnp.dot(xn, wu_ref[:, c0:c1], preferred_element_type=F32)
        a = (g * jax.nn.sigmoid(g) * u).astype(BF16)
        part = jnp.dot(a, wd_ref[c0:c1, :], preferred_element_type=F32)
        h = part if h is None else h + part
    o_ref[...] = x + 0.5 * _rms(h, nw_ref[1:2, :])


def _ffn(x, nw2, wg, wu, wd):
    s = x.shape[0]
    return pl.pallas_call(
        _ffn_kernel,
        grid=(s // ROW_TILE,),
        in_specs=[
            pl.BlockSpec((ROW_TILE, D_MODEL), lambda i: (i, 0)),
            _const_spec((2, D_MODEL)),
            _const_spec((D_MODEL, D_FF)),
            _const_spec((D_MODEL, D_FF)),
            _const_spec((D_FF, D_MODEL)),
        ],
        out_specs=pl.BlockSpec((ROW_TILE, D_MODEL), lambda i: (i, 0)),
        out_shape=jax.ShapeDtypeStruct((s, D_MODEL), F32),
        compiler_params=_params("parallel"),
        name="ffn",
    )(x, nw2, wg, wu, wd)


def _inproj_kernel(x_ref, nw_ref, w_ref, zhg_ref, q_ref, k_ref, vt_ref, ksum_ref, zcv_ref):
    xn = _rms(x_ref[...], nw_ref[...]).astype(BF16)
    c = 4 * HG_WIDTH
    zhg_ref[...] = jnp.dot(xn, w_ref[:, 0:c], preferred_element_type=F32)
    q_ref[...] = jnp.dot(xn, w_ref[:, c:c + AT_WIDTH], preferred_element_type=F32)
    k = jnp.dot(xn, w_ref[:, c + AT_WIDTH:c + 2 * AT_WIDTH], preferred_element_type=F32)
    v = jnp.dot(xn, w_ref[:, c + 2 * AT_WIDTH:c + 3 * AT_WIDTH], preferred_element_type=F32)
    c += 3 * AT_WIDTH
    zcv_ref[...] = jnp.dot(xn, w_ref[:, c:c + 3 * CV_WIDTH], preferred_element_type=F32)
    for b in range(ROW_TILE // MOBA_BLOCK):
        kb = k[b * MOBA_BLOCK:(b + 1) * MOBA_BLOCK, :]
        vb = v[b * MOBA_BLOCK:(b + 1) * MOBA_BLOCK, :]
        k_ref[b] = kb.astype(BF16)
        vt_ref[b] = vb.T.astype(BF16)
        ksum_ref[b * KSUM_ROWS:(b + 1) * KSUM_ROWS, :] = jnp.sum(
            kb.reshape(MOBA_BLOCK // KSUM_ROWS, KSUM_ROWS, AT_WIDTH), axis=0)


def _inproj(x, nw, w_in):
    s = x.shape[0]
    nblk = s // MOBA_BLOCK
    bpt = ROW_TILE // MOBA_BLOCK
    return pl.pallas_call(
        _inproj_kernel,
        grid=(s // ROW_TILE,),
        in_specs=[
            pl.BlockSpec((ROW_TILE, D_MODEL), lambda i: (i, 0)),
            _const_spec((1, D_MODEL)),
            _const_spec((D_MODEL, D_IN)),
        ],
        out_specs=[
            pl.BlockSpec((ROW_TILE, 4 * HG_WIDTH), lambda i: (i, 0)),
            pl.BlockSpec((ROW_TILE, AT_WIDTH), lambda i: (i, 0)),
            pl.BlockSpec((bpt, MOBA_BLOCK, AT_WIDTH), lambda i: (i, 0, 0)),
            pl.BlockSpec((bpt, AT_WIDTH, MOBA_BLOCK), lambda i: (i, 0, 0)),
            pl.BlockSpec((bpt * KSUM_ROWS, AT_WIDTH), lambda i: (i, 0)),
            pl.BlockSpec((ROW_TILE, 3 * CV_WIDTH), lambda i: (i, 0)),
        ],
        out_shape=[
            jax.ShapeDtypeStruct((s, 4 * HG_WIDTH), F32),
            jax.ShapeDtypeStruct((s, AT_WIDTH), F32),
            jax.ShapeDtypeStruct((nblk, MOBA_BLOCK, AT_WIDTH), BF16),
            jax.ShapeDtypeStruct((nblk, AT_WIDTH, MOBA_BLOCK), BF16),
            jax.ShapeDtypeStruct((nblk * KSUM_ROWS, AT_WIDTH), F32),
            jax.ShapeDtypeStruct((s, 3 * CV_WIDTH), F32),
        ],
        compiler_params=_params("parallel"),
        name="inproj",
    )(x, nw, w_in)


def _hg_levels():
    levels = []
    n = HG_TILE
    while n >= 2:
        levels.append(n)
        n //= 2
    return levels


def _hgrn_kernel(layer, q_ref, f_ref, i_ref, g_ref, lb_ref, nw_ref, o_ref, st_ref):
    t_idx = pl.program_id(1)

    @pl.when(t_idx == 0)
    def _():
        st_ref[...] = jnp.zeros_like(st_ref)

    lbraw = lb_ref[...]
    e = jnp.exp(lbraw - jnp.max(lbraw, axis=0, keepdims=True))
    soft = e / jnp.sum(e, axis=0, keepdims=True)
    lb = jnp.sum(soft[0:layer + 1, :], axis=0, keepdims=True) - soft[0:1, :]

    fp = f_ref[...]
    logf = jnp.log(lb + (1.0 - lb) * jax.nn.sigmoid(fp))
    kk = (1.0 - lb) * jax.nn.sigmoid(-fp)
    q = q_ref[...]
    v = i_ref[...]
    vb = v.astype(BF16)

    row = lax.broadcasted_iota(jnp.int32, (HG_TILE, HG_DIM), 0)
    b = logf
    sh = 1
    while sh < HG_TILE:
        b = b + jnp.where(row >= sh, pltpu.roll(b, sh, axis=0), 0.0)
        sh *= 2

    ti = lax.broadcasted_iota(jnp.int32, (HG_TILE, HG_TILE), 0)
    si = lax.broadcasted_iota(jnp.int32, (HG_TILE, HG_TILE), 1)
    scores = jnp.where(
        ti == si,
        lax.dot_general(q.astype(BF16), kk.astype(BF16), _NT, preferred_element_type=F32),
        0.0)
    for n in _hg_levels():
        half = n // 2
        if half >= 8:
            pieces = []
            for blk in range(HG_TILE // n):
                m = blk * n + half - 1
                pieces.append(jnp.broadcast_to(b[m:m + 1, :], (n, HG_DIM)))
            bm = pieces[0] if len(pieces) == 1 else jnp.concatenate(pieces, axis=0)
        else:
            b3 = b.reshape(HG_TILE // 8, 8, HG_DIM)
            sub = lax.broadcasted_iota(jnp.int32, (HG_TILE // 8, 8, HG_DIM), 1)
            bm3 = None
            for blk in range(8 // n):
                m = blk * n + half - 1
                piece = jnp.broadcast_to(b3[:, m:m + 1, :], b3.shape)
                bm3 = piece if bm3 is None else jnp.where(sub >= blk * n, piece, bm3)
            bm = bm3.reshape(HG_TILE, HG_DIM)
        upper = (row & (n - 1)) >= half
        ex = jnp.exp(jnp.where(upper, b - bm, bm - b))
        qs = jnp.where(upper, q * ex, 0.0).astype(BF16)
        ks = jnp.where(upper, 0.0, kk * ex).astype(BF16)
        lvl = lax.dot_general(qs, ks, _NT, preferred_element_type=F32)
        if n == HG_TILE:
            scores = scores + lvl
        else:
            scores = jnp.where((ti & -n) == (si & -n), scores + lvl, scores)

    st = st_ref[...]
    b_last = b[HG_TILE - 1:HG_TILE, :]
    o = jnp.dot(scores.astype(BF16), vb, preferred_element_type=F32)
    o = o + lax.dot_general((q * jnp.exp(b)).astype(BF16), st.astype(BF16), _NT,
                            preferred_element_type=F32)
    kdec = (kk * jnp.exp(b_last - b)).astype(BF16)
    st_ref[...] = st * jnp.exp(b_last) + lax.dot_general(vb, kdec, _TN, preferred_element_type=F32)

    gate = g_ref[...]
    o_ref[...] = (_rms(o, nw_ref[...]) * (gate * jax.nn.sigmoid(gate))).astype(o_ref.dtype)


def _hgrn(zhg, hg_lb, hg_nw, layer):
    s = zhg.shape[0]
    depth = hg_lb.shape[0]

    def col(k):
        return pl.BlockSpec((HG_TILE, HG_DIM), lambda h, t: (t, k * HG_HEADS + h))

    return pl.pallas_call(
        functools.partial(_hgrn_kernel, layer),
        grid=(HG_HEADS, s // HG_TILE),
        in_specs=[
            col(0), col(1), col(2), col(3),
            pl.BlockSpec((depth, HG_DIM), lambda h, t: (0, h)),
            pl.BlockSpec((1, HG_DIM), lambda h, t: (0, 0)),
        ],
        out_specs=pl.BlockSpec((HG_TILE, HG_DIM), lambda h, t: (t, h)),
        out_shape=jax.ShapeDtypeStruct((s, HG_WIDTH), BF16),
        scratch_shapes=[pltpu.VMEM((HG_DIM, HG_DIM), F32)],
        compiler_params=_params("parallel", "arbitrary"),
        name="hgrn2",
    )(zhg, zhg, zhg, zhg, hg_lb, hg_nw)


N_BIAS_TABLES = 6


def _bucket_thresholds():
    max_exact = REL_BUCKETS // 2
    d = np.arange(1, 2 * REL_MAX_DIST, dtype=np.float64)
    large = max_exact + (np.log(d / max_exact) / math.log(REL_MAX_DIST / max_exact)
                         * (REL_BUCKETS - max_exact)).astype(np.int64)
    large = np.minimum(large, REL_BUCKETS - 1)
    bucket = np.where(d < max_exact, d.astype(np.int64), large)
    thr = [0] * REL_BUCKETS
    for bkt in range(1, REL_BUCKETS):
        thr[bkt] = int(d[np.argmax(bucket >= bkt)])
    assert (N_BIAS_TABLES - 2) * MOBA_BLOCK + 1 >= thr[REL_BUCKETS - 1]
    return thr


def _bias_kernel(rb_ref, o_ref):
    thr = _bucket_thresholds()
    key = lax.broadcasted_iota(jnp.int32, (MOBA_BLOCK, MOBA_BLOCK), 0)
    qry = lax.broadcasted_iota(jnp.int32, (MOBA_BLOCK, MOBA_BLOCK), 1)
    for t in range(N_BIAS_TABLES):
        dist = qry - key + t * MOBA_BLOCK
        for h in range(AT_HEADS):
            val = jnp.full((MOBA_BLOCK, MOBA_BLOCK), rb_ref[REL_BUCKETS - 1, h], F32)
            for bkt in range(REL_BUCKETS - 2, -1, -1):
                val = jnp.where(dist < thr[bkt + 1], rb_ref[bkt, h], val)
            if t == 0:
                val = jnp.where(dist < 0, NEG, val)
            o_ref[t, h] = val


def _bias_tables(rel_bias):
    return pl.pallas_call(
        _bias_kernel,
        in_specs=[pl.BlockSpec(memory_space=pltpu.SMEM)],
        out_shape=jax.ShapeDtypeStruct((N_BIAS_TABLES, AT_HEADS, MOBA_BLOCK, MOBA_BLOCK), F32),
        name="moba_bias",
    )(rel_bias)


def _moba_kernel(q_ref, k_ref, vt_ref, ksum_ref, bias_ref, o_ref, kmean_ref, mask_ref, qs_ref, ot_ref):
    i = pl.program_id(0)
    nblk = k_ref.shape[0]

    @pl.when(i == 0)
    def _():
        ks = ksum_ref[...].reshape(nblk, KSUM_ROWS, AT_WIDTH)
        kmean_ref[...] = jnp.sum(ks, axis=1) * (1.0 / MOBA_BLOCK)

    q = q_ref[...]
    lane_head = lax.broadcasted_iota(jnp.int32, (MOBA_BLOCK, AT_WIDTH), 1) // AT_DIM
    jio = lax.broadcasted_iota(jnp.int32, (nblk, MOBA_BLOCK), 0).astype(F32)
    fi = i.astype(F32)
    kmean = kmean_ref[...]
    for h in range(AT_HEADS):
        qm = jnp.where(lane_head == h, q, 0.0)
        gate = lax.dot_general(kmean, qm, _NT, precision=lax.Precision.HIGHEST,
                               preferred_element_type=F32)
        gate = jnp.where(jio < fi, gate, -jnp.inf)
        sel = jio == fi
        for _ in range(MOBA_TOPK):
            mx = jnp.max(gate, axis=0, keepdims=True)
            cand = jnp.where(gate == mx, jio, float(nblk))
            idx = jnp.min(cand, axis=0, keepdims=True)
            pick = (jio == idx) & (mx > -jnp.inf)
            sel = sel | pick
            gate = jnp.where(pick, -jnp.inf, gate)
        mask_ref[h] = jnp.where(sel, 0.0, NEG)
        qs_ref[h] = (qm * (AT_DIM ** -0.5)).T.astype(BF16)

    for h in range(AT_HEADS):
        def body(jj, carry, h=h):
            m, l, acc = carry
            j = i - jj
            t = jnp.minimum(jj, N_BIAS_TABLES - 1)
            s = jnp.dot(k_ref[j], qs_ref[h], preferred_element_type=F32)
            s = s + mask_ref[h, pl.ds(j, 1), :] + bias_ref[t, h]
            m_new = jnp.maximum(m, jnp.max(s, axis=0, keepdims=True))
            alpha = jnp.exp(m - m_new)
            p = jnp.exp(s - m_new)
            l = alpha * l + jnp.sum(p, axis=0, keepdims=True)
            pv = jnp.dot(vt_ref[j, h * AT_DIM:(h + 1) * AT_DIM, :], p.astype(BF16),
                         preferred_element_type=F32)
            return m_new, l, alpha * acc + pv

        init = (jnp.full((1, MOBA_BLOCK), NEG, F32), jnp.zeros((1, MOBA_BLOCK), F32),
                jnp.zeros((AT_DIM, MOBA_BLOCK), F32))
        _, l, acc = lax.fori_loop(0, i + 1, body, init)
        ot_ref[h * AT_DIM:(h + 1) * AT_DIM, :] = acc / l
    o_ref[...] = ot_ref[...].T.astype(o_ref.dtype)


def _moba(q, k, vt, ksum, bias):
    s = q.shape[0]
    nblk = s // MOBA_BLOCK
    return pl.pallas_call(
        _moba_kernel,
        grid=(nblk,),
        in_specs=[
            pl.BlockSpec((MOBA_BLOCK, AT_WIDTH), lambda i: (i, 0)),
            _const_spec((nblk, MOBA_BLOCK, AT_WIDTH)),
            _const_spec((nblk, AT_WIDTH, MOBA_BLOCK)),
            _const_spec((nblk * KSUM_ROWS, AT_WIDTH)),
            _const_spec((N_BIAS_TABLES, AT_HEADS, MOBA_BLOCK, MOBA_BLOCK)),
        ],
        out_specs=pl.BlockSpec((MOBA_BLOCK, AT_WIDTH), lambda i: (i, 0)),
        out_shape=jax.ShapeDtypeStruct((s, AT_WIDTH), BF16),
        scratch_shapes=[
            pltpu.VMEM((nblk, AT_WIDTH), F32),
            pltpu.VMEM((AT_HEADS, nblk, MOBA_BLOCK), F32),
            pltpu.VMEM((AT_HEADS, AT_WIDTH, MOBA_BLOCK), BF16),
            pltpu.VMEM((AT_WIDTH, MOBA_BLOCK), F32),
        ],
        compiler_params=_params("arbitrary"),
        name="moba",
    )(q, k, vt, ksum, bias)


def _outproj_kernel(x_ref, ohg_ref, oat_ref, zcv_ref, halo_ref, cw_ref, w_ref, nw_ref, o_ref):
    i = pl.program_id(0)
    zcv = zcv_ref[...]
    bgate = zcv[:, 0:CV_WIDTH]
    u = zcv[:, CV_WIDTH:2 * CV_WIDTH] * zcv[:, 2 * CV_WIDTH:3 * CV_WIDTH]
    halo = halo_ref[...]
    uh = halo[:, CV_WIDTH:2 * CV_WIDTH] * halo[:, 2 * CV_WIDTH:3 * CV_WIDTH]
    uh = jnp.where(i > 0, uh, 0.0)
    row = lax.broadcasted_iota(jnp.int32, u.shape, 0)
    u1 = jnp.where(row == 0, uh[7:8, :], pltpu.roll(u, 1, axis=0))
    u2 = jnp.where(row == 0, uh[6:7, :], jnp.where(row == 1, uh[7:8, :], pltpu.roll(u, 2, axis=0)))
    cw = cw_ref[...]
    ocv = bgate * (cw[0:1, :] * u2 + cw[1:2, :] * u1 + cw[2:3, :] * u)
    h = jnp.dot(ohg_ref[...], w_ref[0:HG_WIDTH, :], preferred_element_type=F32)
    h = h + jnp.dot(oat_ref[...], w_ref[HG_WIDTH:HG_WIDTH + AT_WIDTH, :], preferred_element_type=F32)
    h = h + jnp.dot(ocv.astype(BF16), w_ref[HG_WIDTH + AT_WIDTH:D_MIX, :], preferred_element_type=F32)
    o_ref[...] = x_ref[...] + _rms(h, nw_ref[...])


def _outproj(x, ohg, oat, zcv, conv_w, w_out, nw):
    s = x.shape[0]
    halo_blocks = ROW_TILE // 8
    return pl.pallas_call(
        _outproj_kernel,
        grid=(s // ROW_TILE,),
        in_specs=[
            pl.BlockSpec((ROW_TILE, D_MODEL), lambda i: (i, 0)),
            pl.BlockSpec((ROW_TILE, HG_WIDTH), lambda i: (i, 0)),
            pl.BlockSpec((ROW_TILE, AT_WIDTH), lambda i: (i, 0)),
            pl.BlockSpec((ROW_TILE, 3 * CV_WIDTH), lambda i: (i, 0)),
            pl.BlockSpec((8, 3 * CV_WIDTH), lambda i: (jnp.maximum(i * halo_blocks - 1, 0), 0)),
            _const_spec((CV_KERNEL, CV_WIDTH)),
            _const_spec((D_MIX, D_MODEL)),
            _const_spec((1, D_MODEL)),
        ],
        out_specs=pl.BlockSpec((ROW_TILE, D_MODEL), lambda i: (i, 0)),
        out_shape=jax.ShapeDtypeStruct((s, D_MODEL), F32),
        compiler_params=_params("parallel"),
        name="outproj",
    )(x, ohg, oat, zcv, zcv, conv_w, w_out, nw)


def kernel(x, norm_w, ffn1_wg, ffn1_wu, ffn1_wd, mix_w_in, mix_w_out, hg_lb, hg_norm_w, conv_w,
           ffn2_wg, ffn2_wu, ffn2_wd, rel_bias):
    batch, seq, _ = x.shape
    depth = norm_w.shape[0]
    assert batch == 1 and seq % ROW_TILE == 0 and seq % MOBA_BLOCK == 0
    bias = _bias_tables(rel_bias.astype(F32))
    y = x.reshape(seq, D_MODEL)
    for l in range(depth):
        y = _ffn(y, norm_w[l, 0:2], ffn1_wg[l].astype(BF16), ffn1_wu[l].astype(BF16),
                 ffn1_wd[l].astype(BF16))
        zhg, q, k, vt, ksum, zcv = _inproj(y, norm_w[l, 2:3], mix_w_in[l].astype(BF16))
        ohg = _hgrn(zhg, hg_lb, hg_norm_w[l:l + 1], l)
        oat = _moba(q, k, vt, ksum, bias)
        y = _outproj(y, ohg, oat, zcv, conv_w[l], mix_w_out[l].astype(BF16), norm_w[l, 3:4])
        y = _ffn(y, norm_w[l, 4:6], ffn2_wg[l].astype(BF16), ffn2_wu[l].astype(BF16),
                 ffn2_wd[l].astype(BF16))
    return y.reshape(batch, seq, D_MODEL)
```

```python
import functools
import math

import numpy as np
import jax
import jax.numpy as jnp
from jax import lax
from jax.experimental import pallas as pl
from jax.experimental.pallas import tpu as pltpu

F32 = jnp.float32
BF16 = jnp.bfloat16

D_MODEL = 1024
D_FF = 2816
HG_HEADS = 4
HG_DIM = 128
HG_WIDTH = HG_HEADS * HG_DIM
AT_HEADS = 4
AT_DIM = 64
AT_WIDTH = AT_HEADS * AT_DIM
MOBA_BLOCK = 256
MOBA_TOPK = 3
REL_BUCKETS = 32
REL_MAX_DIST = 1024
CV_WIDTH = 256
CV_KERNEL = 3
D_MIX = HG_WIDTH + AT_WIDTH + CV_WIDTH
D_IN = 4 * HG_WIDTH + 3 * AT_WIDTH + 3 * CV_WIDTH
EPS = 1e-6

ROW_TILE = 512
HG_TILE = 256
KSUM_ROWS = 8
V_ROWS = AT_DIM + 16
LOG2E = math.log2(math.e)
FAR_GROUP = 4
NEG = -1e30
VMEM_LIMIT = 56 * 1024 * 1024
FF_CHUNKS = ((0, 768), (768, 1536), (1536, 2304), (2304, 2816))

_NT = (((1,), (1,)), ((), ()))
_TN = (((0,), (0,)), ((), ()))


def _rms(x, w):
    ms = jnp.mean(x * x, axis=-1, keepdims=True)
    return x * lax.rsqrt(ms + EPS) * w


def _const_spec(shape):
    nd = len(shape)
    return pl.BlockSpec(shape, lambda *_: (0,) * nd, pipeline_mode=pl.Buffered(1))


def _params(*sem, flags=None):
    return pltpu.CompilerParams(dimension_semantics=sem, vmem_limit_bytes=VMEM_LIMIT, flags=flags)


def _ffn_kernel(x_ref, nw_ref, wg_ref, wu_ref, wd_ref, o_ref):
    x = x_ref[...]
    xn = _rms(x, nw_ref[0:1, :]).astype(BF16)
    h = None
    for c0, c1 in FF_CHUNKS:
        g = jnp.dot(xn, wg_ref[:, c0:c1], preferred_element_type=F32)
        u = jnp.dot(xn, wu_ref[:, c0:c1], preferred_element_type=F32)
        a = (g * jax.nn.sigmoid(g) * u).astype(BF16)
        part = jnp.dot(a, wd_ref[c0:c1, :], preferred_element_type=F32)
        h = part if h is None else h + part
    o_ref[...] = x + 0.5 * _rms(h, nw_ref[1:2, :])


def _ffn(x, nw2, wg, wu, wd):
    s = x.shape[0]
    return pl.pallas_call(
        _ffn_kernel,
        grid=(s // ROW_TILE,),
        in_specs=[
            pl.BlockSpec((ROW_TILE, D_MODEL), lambda i: (i, 0)),
            _const_spec((2, D_MODEL)),
            _const_spec((D_MODEL, D_FF)),
            _const_spec((D_MODEL, D_FF)),
            _const_spec((D_FF, D_MODEL)),
        ],
        out_specs=pl.BlockSpec((ROW_TILE, D_MODEL), lambda i: (i, 0)),
        out_shape=jax.ShapeDtypeStruct((s, D_MODEL), F32),
        compiler_params=_params("parallel"),
        name="ffn",
    )(x, nw2, wg, wu, wd)


def _inproj_kernel(x_ref, nw_ref, w_ref, zhg_ref, q_ref, k_ref, vt_ref, ksum_ref, zcv_ref):
    xn = _rms(x_ref[...], nw_ref[...]).astype(BF16)
    c = 4 * HG_WIDTH
    zhg_ref[...] = jnp.dot(xn, w_ref[:, 0:c], preferred_element_type=F32)
    q_ref[...] = jnp.dot(xn, w_ref[:, c:c + AT_WIDTH], preferred_element_type=F32)
    k = jnp.dot(xn, w_ref[:, c + AT_WIDTH:c + 2 * AT_WIDTH], preferred_element_type=F32)
    v = jnp.dot(xn, w_ref[:, c + 2 * AT_WIDTH:c + 3 * AT_WIDTH], preferred_element_type=F32)
    c += 3 * AT_WIDTH
    zcv_ref[...] = jnp.dot(xn, w_ref[:, c:c + 3 * CV_WIDTH], preferred_element_type=F32)
    ones = jnp.ones((V_ROWS - AT_DIM, MOBA_BLOCK), F32)
    for b in range(ROW_TILE // MOBA_BLOCK):
        kb = k[b * MOBA_BLOCK:(b + 1) * MOBA_BLOCK, :]
        vbt = v[b * MOBA_BLOCK:(b + 1) * MOBA_BLOCK, :].T
        k_ref[b] = kb.astype(BF16)
        for h in range(AT_HEADS):
            vt_ref[b, h] = jnp.concatenate(
                [vbt[h * AT_DIM:(h + 1) * AT_DIM, :], ones], axis=0).astype(BF16)
        ksum_ref[b * KSUM_ROWS:(b + 1) * KSUM_ROWS, :] = jnp.sum(
            kb.reshape(MOBA_BLOCK // KSUM_ROWS, KSUM_ROWS, AT_WIDTH), axis=0)


def _inproj(x, nw, w_in):
    s = x.shape[0]
    nblk = s // MOBA_BLOCK
    bpt = ROW_TILE // MOBA_BLOCK
    return pl.pallas_call(
        _inproj_kernel,
        grid=(s // ROW_TILE,),
        in_specs=[
            pl.BlockSpec((ROW_TILE, D_MODEL), lambda i: (i, 0)),
            _const_spec((1, D_MODEL)),
            _const_spec((D_MODEL, D_IN)),
        ],
        out_specs=[
            pl.BlockSpec((ROW_TILE, 4 * HG_WIDTH), lambda i: (i, 0)),
            pl.BlockSpec((ROW_TILE, AT_WIDTH), lambda i: (i, 0)),
            pl.BlockSpec((bpt, MOBA_BLOCK, AT_WIDTH), lambda i: (i, 0, 0)),
            pl.BlockSpec((bpt, AT_HEADS, V_ROWS, MOBA_BLOCK), lambda i: (i, 0, 0, 0)),
            pl.BlockSpec((bpt * KSUM_ROWS, AT_WIDTH), lambda i: (i, 0)),
            pl.BlockSpec((ROW_TILE, 3 * CV_WIDTH), lambda i: (i, 0)),
        ],
        out_shape=[
            jax.ShapeDtypeStruct((s, 4 * HG_WIDTH), F32),
            jax.ShapeDtypeStruct((s, AT_WIDTH), F32),
            jax.ShapeDtypeStruct((nblk, MOBA_BLOCK, AT_WIDTH), BF16),
            jax.ShapeDtypeStruct((nblk, AT_HEADS, V_ROWS, MOBA_BLOCK), BF16),
            jax.ShapeDtypeStruct((nblk * KSUM_ROWS, AT_WIDTH), F32),
            jax.ShapeDtypeStruct((s, 3 * CV_WIDTH), F32),
        ],
        compiler_params=_params("parallel"),
        name="inproj",
    )(x, nw, w_in)


def _hg_levels():
    levels = []
    n = HG_TILE
    while n >= 2:
        levels.append(n)
        n //= 2
    return levels


def _hgrn_kernel(layer, q_ref, f_ref, i_ref, g_ref, lb_ref, nw_ref, o_ref, st_ref):
    t_idx = pl.program_id(1)

    @pl.when(t_idx == 0)
    def _():
        st_ref[...] = jnp.zeros_like(st_ref)

    lbraw = lb_ref[...]
    e = jnp.exp(lbraw - jnp.max(lbraw, axis=0, keepdims=True))
    soft = e / jnp.sum(e, axis=0, keepdims=True)
    lb = jnp.sum(soft[0:layer + 1, :], axis=0, keepdims=True) - soft[0:1, :]

    fp = f_ref[...]
    logf = jnp.log(lb + (1.0 - lb) * jax.nn.sigmoid(fp))
    kk = (1.0 - lb) * jax.nn.sigmoid(-fp)
    q = q_ref[...]
    v = i_ref[...]
    vb = v.astype(BF16)

    row = lax.broadcasted_iota(jnp.int32, (HG_TILE, HG_DIM), 0)
    b = logf
    sh = 1
    while sh < HG_TILE:
        b = b + jnp.where(row >= sh, pltpu.roll(b, sh, axis=0), 0.0)
        sh *= 2

    ti = lax.broadcasted_iota(jnp.int32, (HG_TILE, HG_TILE), 0)
    si = lax.broadcasted_iota(jnp.int32, (HG_TILE, HG_TILE), 1)
    scores = jnp.where(
        ti == si,
        lax.dot_general(q.astype(BF16), kk.astype(BF16), _NT, preferred_element_type=F32),
        0.0)
    for n in _hg_levels():
        half = n // 2
        if half >= 8:
            pieces = []
            for blk in range(HG_TILE // n):
                m = blk * n + half - 1
                pieces.append(jnp.broadcast_to(b[m:m + 1, :], (n, HG_DIM)))
            bm = pieces[0] if len(pieces) == 1 else jnp.concatenate(pieces, axis=0)
        else:
            b3 = b.reshape(HG_TILE // 8, 8, HG_DIM)
            sub = lax.broadcasted_iota(jnp.int32, (HG_TILE // 8, 8, HG_DIM), 1)
            bm3 = None
            for blk in range(8 // n):
                m = blk * n + half - 1
                piece = jnp.broadcast_to(b3[:, m:m + 1, :], b3.shape)
                bm3 = piece if bm3 is None else jnp.where(sub >= blk * n, piece, bm3)
            bm = bm3.reshape(HG_TILE, HG_DIM)
        upper = (row & (n - 1)) >= half
        ex = jnp.exp(jnp.where(upper, b - bm, bm - b))
        qs = jnp.where(upper, q * ex, 0.0).astype(BF16)
        ks = jnp.where(upper, 0.0, kk * ex).astype(BF16)
        lvl = lax.dot_general(qs, ks, _NT, preferred_element_type=F32)
        if n == HG_TILE:
            scores = scores + lvl
        else:
            scores = jnp.where((ti & -n) == (si & -n), scores + lvl, scores)

    st = st_ref[...]
    b_last = b[HG_TILE - 1:HG_TILE, :]
    o = jnp.dot(scores.astype(BF16), vb, preferred_element_type=F32)
    o = o + lax.dot_general((q * jnp.exp(b)).astype(BF16), st.astype(BF16), _NT,
                            preferred_element_type=F32)
    kdec = (kk * jnp.exp(b_last - b)).astype(BF16)
    st_ref[...] = st * jnp.exp(b_last) + lax.dot_general(vb, kdec, _TN, preferred_element_type=F32)

    gate = g_ref[...]
    o_ref[...] = (_rms(o, nw_ref[...]) * (gate * jax.nn.sigmoid(gate))).astype(o_ref.dtype)


def _hgrn(zhg, hg_lb, hg_nw, layer):
    s = zhg.shape[0]
    depth = hg_lb.shape[0]

    def col(k):
        return pl.BlockSpec((HG_TILE, HG_DIM), lambda h, t: (t, k * HG_HEADS + h))

    return pl.pallas_call(
        functools.partial(_hgrn_kernel, layer),
        grid=(HG_HEADS, s // HG_TILE),
        in_specs=[
            col(0), col(1), col(2), col(3),
            pl.BlockSpec((depth, HG_DIM), lambda h, t: (0, h)),
            pl.BlockSpec((1, HG_DIM), lambda h, t: (0, 0)),
        ],
        out_specs=pl.BlockSpec((HG_TILE, HG_DIM), lambda h, t: (t, h)),
        out_shape=jax.ShapeDtypeStruct((s, HG_WIDTH), BF16),
        scratch_shapes=[pltpu.VMEM((HG_DIM, HG_DIM), F32)],
        compiler_params=_params("parallel", "arbitrary"),
        name="hgrn2",
    )(zhg, zhg, zhg, zhg, hg_lb, hg_nw)


N_BIAS_TABLES = 5


def _bucket_thresholds():
    max_exact = REL_BUCKETS // 2
    d = np.arange(1, 2 * REL_MAX_DIST, dtype=np.float64)
    large = max_exact + (np.log(d / max_exact) / math.log(REL_MAX_DIST / max_exact)
                         * (REL_BUCKETS - max_exact)).astype(np.int64)
    large = np.minimum(large, REL_BUCKETS - 1)
    bucket = np.where(d < max_exact, d.astype(np.int64), large)
    thr = [0] * REL_BUCKETS
    for bkt in range(1, REL_BUCKETS):
        thr[bkt] = int(d[np.argmax(bucket >= bkt)])
    assert (N_BIAS_TABLES - 1) * MOBA_BLOCK + 1 >= thr[REL_BUCKETS - 1]
    return thr


def _bias_kernel(rb_ref, o_ref):
    thr = _bucket_thresholds()
    key = lax.broadcasted_iota(jnp.int32, (MOBA_BLOCK, MOBA_BLOCK), 0)
    qry = lax.broadcasted_iota(jnp.int32, (MOBA_BLOCK, MOBA_BLOCK), 1)
    o_ref[N_BIAS_TABLES:] = jnp.zeros((N_BIAS_TABLES - 1,) + o_ref.shape[1:], F32)
    for u in range(N_BIAS_TABLES):
        t = N_BIAS_TABLES - 1 - u
        dist = qry - key + t * MOBA_BLOCK
        for h in range(AT_HEADS):
            val = jnp.full((MOBA_BLOCK, MOBA_BLOCK), rb_ref[REL_BUCKETS - 1, h], F32)
            for bkt in range(REL_BUCKETS - 2, -1, -1):
                val = jnp.where(dist < thr[bkt + 1], rb_ref[bkt, h], val)
            val = val * LOG2E
            if t == 0:
                val = jnp.where(dist < 0, NEG, val)
            o_ref[u, :, h * MOBA_BLOCK:(h + 1) * MOBA_BLOCK] = val


def _bias_tables(rel_bias):
    return pl.pallas_call(
        _bias_kernel,
        in_specs=[pl.BlockSpec(memory_space=pltpu.SMEM)],
        out_shape=jax.ShapeDtypeStruct((2 * N_BIAS_TABLES - 1, MOBA_BLOCK, AT_HEADS * MOBA_BLOCK), F32),
        name="moba_bias",
    )(rel_bias)


def _moba_kernel(rb_ref, q_ref, k_ref, vt_ref, ksum_ref, bias_ref, o_ref,
                 kmean_ref, mnear_ref, mfar_ref, qs_ref, m_ref, acc_ref):
    i = pl.program_id(0)
    nblk = k_ref.shape[0]

    @pl.when(i == 0)
    def _():
        ks = ksum_ref[...].reshape(nblk, KSUM_ROWS, AT_WIDTH)
        kmean_ref[...] = jnp.sum(ks, axis=1) * (1.0 / MOBA_BLOCK)

    q = q_ref[...]
    lane_head = lax.broadcasted_iota(jnp.int32, (MOBA_BLOCK, AT_WIDTH), 1) // AT_DIM
    jio = lax.broadcasted_iota(jnp.int32, (nblk, MOBA_BLOCK), 0).astype(F32)
    fi = i.astype(F32)
    n_far = jnp.maximum(i - (N_BIAS_TABLES - 1), 0)
    kmean = kmean_ref[...]
    for h in range(AT_HEADS):
        qm = jnp.where(lane_head == h, q, 0.0)
        gate = lax.dot_general(kmean, qm, _NT, precision=lax.Precision.HIGHEST,
                               preferred_element_type=F32)
        gate = jnp.where(jio < fi, gate, -jnp.inf)
        sel = jio == fi
        for _ in range(MOBA_TOPK):
            mx = jnp.max(gate, axis=0, keepdims=True)
            cand = jnp.where(gate == mx, jio, float(nblk))
            idx = jnp.min(cand, axis=0, keepdims=True)
            pick = (jio == idx) & (mx > -jnp.inf)
            sel = sel | pick
            gate = jnp.where(pick, -jnp.inf, gate)
        cols = slice(h * MOBA_BLOCK, (h + 1) * MOBA_BLOCK)
        mnear_ref[:, cols] = jnp.where(sel, 0.0, NEG)
        mfar_ref[:, cols] = jnp.where(sel & (jio < n_far.astype(F32)),
                                      rb_ref[REL_BUCKETS - 1, h] * LOG2E, NEG)
        qs_ref[:, cols] = (qm * (AT_DIM ** -0.5 * LOG2E)).T.astype(BF16)
    m_ref[...] = jnp.full(m_ref.shape, NEG, F32)
    acc_ref[...] = jnp.zeros(acc_ref.shape, F32)

    def attend(j0, nb, mask_ref, table):
        if nb == 1:
            kg = k_ref[j0]
        else:
            kg = k_ref[pl.ds(j0, nb)].reshape(nb * MOBA_BLOCK, AT_WIDTH)
        s_all = jnp.dot(kg, qs_ref[...], preferred_element_type=F32)
        parts = []
        for g in range(nb):
            s = s_all[g * MOBA_BLOCK:(g + 1) * MOBA_BLOCK, :] + mask_ref[pl.ds(j0 + g, 1), :]
            if table is not None:
                s = s + bias_ref[table + g]
            mb = jnp.max(s, axis=0, keepdims=True)
            p = jnp.exp2(s - mb).astype(BF16)
            pv = jnp.concatenate(
                [jnp.dot(vt_ref[j0 + g, h], p[:, h * MOBA_BLOCK:(h + 1) * MOBA_BLOCK],
                         preferred_element_type=F32) for h in range(AT_HEADS)], axis=1)
            parts.append((mb, pv))
        m = m_ref[...]
        acc = acc_ref[...]
        for mb, pv in parts:
            m_new = jnp.maximum(m, mb)
            acc = jnp.exp2(m - m_new) * acc + jnp.exp2(mb - m_new) * pv
            m = m_new
        m_ref[...] = m
        acc_ref[...] = acc

    near0 = jnp.maximum(i - (N_BIAS_TABLES - 1), 0)
    attend(near0, N_BIAS_TABLES, mnear_ref, near0 - (i - (N_BIAS_TABLES - 1)))

    def far(jg, carry):
        attend(FAR_GROUP * jg, FAR_GROUP, mfar_ref, None)
        return carry

    lax.fori_loop(0, (n_far + FAR_GROUP - 1) // FAR_GROUP, far, 0)
    acc = acc_ref[...]
    ot = acc[0:AT_DIM, :] / acc[AT_DIM:AT_DIM + 1, :]
    o_ref[...] = jnp.concatenate(
        [ot[:, h * MOBA_BLOCK:(h + 1) * MOBA_BLOCK] for h in range(AT_HEADS)], axis=0
    ).T.astype(o_ref.dtype)


def _moba(rel_bias, q, k, vt, ksum, bias):
    s = q.shape[0]
    nblk = s // MOBA_BLOCK
    assert nblk >= N_BIAS_TABLES and FAR_GROUP <= N_BIAS_TABLES
    return pl.pallas_call(
        _moba_kernel,
        grid=(nblk,),
        in_specs=[
            pl.BlockSpec(memory_space=pltpu.SMEM),
            pl.BlockSpec((MOBA_BLOCK, AT_WIDTH), lambda i: (i, 0)),
            _const_spec((nblk, MOBA_BLOCK, AT_WIDTH)),
            _const_spec((nblk, AT_HEADS, V_ROWS, MOBA_BLOCK)),
            _const_spec((nblk * KSUM_ROWS, AT_WIDTH)),
            _const_spec((2 * N_BIAS_TABLES - 1, MOBA_BLOCK, AT_HEADS * MOBA_BLOCK)),
        ],
        out_specs=pl.BlockSpec((MOBA_BLOCK, AT_WIDTH), lambda i: (i, 0)),
        out_shape=jax.ShapeDtypeStruct((s, AT_WIDTH), BF16),
        scratch_shapes=[
            pltpu.VMEM((nblk, AT_WIDTH), F32),
            pltpu.VMEM((nblk, AT_HEADS * MOBA_BLOCK), F32),
            pltpu.VMEM((nblk, AT_HEADS * MOBA_BLOCK), F32),
            pltpu.VMEM((AT_WIDTH, AT_HEADS * MOBA_BLOCK), BF16),
            pltpu.VMEM((1, AT_HEADS * MOBA_BLOCK), F32),
            pltpu.VMEM((V_ROWS, AT_HEADS * MOBA_BLOCK), F32),
        ],
        compiler_params=_params("arbitrary"),
        name="moba",
    )(rel_bias, q, k, vt, ksum, bias)


def _outproj_kernel(x_ref, ohg_ref, oat_ref, zcv_ref, halo_ref, cw_ref, w_ref, nw_ref, o_ref):
    i = pl.program_id(0)
    zcv = zcv_ref[...]
    bgate = zcv[:, 0:CV_WIDTH]
    u = zcv[:, CV_WIDTH:2 * CV_WIDTH] * zcv[:, 2 * CV_WIDTH:3 * CV_WIDTH]
    halo = halo_ref[...]
    uh = halo[:, CV_WIDTH:2 * CV_WIDTH] * halo[:, 2 * CV_WIDTH:3 * CV_WIDTH]
    uh = jnp.where(i > 0, uh, 0.0)
    row = lax.broadcasted_iota(jnp.int32, u.shape, 0)
    u1 = jnp.where(row == 0, uh[7:8, :], pltpu.roll(u, 1, axis=0))
    u2 = jnp.where(row == 0, uh[6:7, :], jnp.where(row == 1, uh[7:8, :], pltpu.roll(u, 2, axis=0)))
    cw = cw_ref[...]
    ocv = bgate * (cw[0:1, :] * u2 + cw[1:2, :] * u1 + cw[2:3, :] * u)
    h = jnp.dot(ohg_ref[...], w_ref[0:HG_WIDTH, :], preferred_element_type=F32)
    h = h + jnp.dot(oat_ref[...], w_ref[HG_WIDTH:HG_WIDTH + AT_WIDTH, :], preferred_element_type=F32)
    h = h + jnp.dot(ocv.astype(BF16), w_ref[HG_WIDTH + AT_WIDTH:D_MIX, :], preferred_element_type=F32)
    o_ref[...] = x_ref[...] + _rms(h, nw_ref[...])


def _outproj(x, ohg, oat, zcv, conv_w, w_out, nw):
    s = x.shape[0]
    halo_blocks = ROW_TILE // 8
    return pl.pallas_call(
        _outproj_kernel,
        grid=(s // ROW_TILE,),
        in_specs=[
            pl.BlockSpec((ROW_TILE, D_MODEL), lambda i: (i, 0)),
            pl.BlockSpec((ROW_TILE, HG_WIDTH), lambda i: (i, 0)),
            pl.BlockSpec((ROW_TILE, AT_WIDTH), lambda i: (i, 0)),
            pl.BlockSpec((ROW_TILE, 3 * CV_WIDTH), lambda i: (i, 0)),
            pl.BlockSpec((8, 3 * CV_WIDTH), lambda i: (jnp.maximum(i * halo_blocks - 1, 0), 0)),
            _const_spec((CV_KERNEL, CV_WIDTH)),
            _const_spec((D_MIX, D_MODEL)),
            _const_spec((1, D_MODEL)),
        ],
        out_specs=pl.BlockSpec((ROW_TILE, D_MODEL), lambda i: (i, 0)),
        out_shape=jax.ShapeDtypeStruct((s, D_MODEL), F32),
        compiler_params=_params("parallel"),
        name="outproj",
    )(x, ohg, oat, zcv, zcv, conv_w, w_out, nw)


def kernel(x, norm_w, ffn1_wg, ffn1_wu, ffn1_wd, mix_w_in, mix_w_out, hg_lb, hg_norm_w, conv_w,
           ffn2_wg, ffn2_wu, ffn2_wd, rel_bias):
    batch, seq, _ = x.shape
    depth = norm_w.shape[0]
    assert batch == 1 and seq % ROW_TILE == 0 and seq % MOBA_BLOCK == 0
    rel_bias = rel_bias.astype(F32)
    bias = _bias_tables(rel_bias)
    y = x.reshape(seq, D_MODEL)
    for l in range(depth):
        y = _ffn(y, norm_w[l, 0:2], ffn1_wg[l].astype(BF16), ffn1_wu[l].astype(BF16),
                 ffn1_wd[l].astype(BF16))
        zhg, q, k, vt, ksum, zcv = _inproj(y, norm_w[l, 2:3], mix_w_in[l].astype(BF16))
        ohg = _hgrn(zhg, hg_lb, hg_norm_w[l:l + 1], l)
        oat = _moba(rel_bias, q, k, vt, ksum, bias)
        y = _outproj(y, ohg, oat, zcv, conv_w[l], mix_w_out[l].astype(BF16), norm_w[l, 3:4])
        y = _ffn(y, norm_w[l, 4:6], ffn2_wg[l].astype(BF16), ffn2_wu[l].astype(BF16),
                 ffn2_wd[l].astype(BF16))
    return y.reshape(batch, seq, D_MODEL)
```

```python
import functools
import math

import numpy as np
import jax
import jax.numpy as jnp
from jax import lax
from jax.experimental import pallas as pl
from jax.experimental.pallas import tpu as pltpu

F32 = jnp.float32
BF16 = jnp.bfloat16

D_MODEL = 1024
D_FF = 2816
HG_HEADS = 4
HG_DIM = 128
HG_WIDTH = HG_HEADS * HG_DIM
AT_HEADS = 4
AT_DIM = 64
AT_WIDTH = AT_HEADS * AT_DIM
MOBA_BLOCK = 256
MOBA_TOPK = 3
REL_BUCKETS = 32
REL_MAX_DIST = 1024
CV_WIDTH = 256
CV_KERNEL = 3
D_MIX = HG_WIDTH + AT_WIDTH + CV_WIDTH
D_IN = 4 * HG_WIDTH + 3 * AT_WIDTH + 3 * CV_WIDTH
EPS = 1e-6

ROW_TILE = 512
HG_TILE = 256
KSUM_ROWS = 8
V_ROWS = AT_DIM + 16
LOG2E = math.log2(math.e)
FAR_GROUP = 4
NEG = -1e30
VMEM_LIMIT = 56 * 1024 * 1024
FF_CHUNKS = ((0, 768), (768, 1536), (1536, 2304), (2304, 2816))

_NT = (((1,), (1,)), ((), ()))
_TN = (((0,), (0,)), ((), ()))


def _rms(x, w):
    ms = jnp.mean(x * x, axis=-1, keepdims=True)
    return x * lax.rsqrt(ms + EPS) * w


def _const_spec(shape):
    nd = len(shape)
    return pl.BlockSpec(shape, lambda *_: (0,) * nd, pipeline_mode=pl.Buffered(1))


def _params(*sem, flags=None):
    return pltpu.CompilerParams(dimension_semantics=sem, vmem_limit_bytes=VMEM_LIMIT, flags=flags)


def _ffn_kernel(x_ref, nw_ref, wg_ref, wu_ref, wd_ref, o_ref):
    x = x_ref[...]
    xn = _rms(x, nw_ref[0:1, :]).astype(BF16)
    h = None
    for c0, c1 in FF_CHUNKS:
        g = jnp.dot(xn, wg_ref[:, c0:c1], preferred_element_type=F32)
        u = jnp.dot(xn, wu_ref[:, c0:c1], preferred_element_type=F32)
        a = (g * jax.nn.sigmoid(g) * u).astype(BF16)
        part = jnp.dot(a, wd_ref[c0:c1, :], preferred_element_type=F32)
        h = part if h is None else h + part
    o_ref[...] = x + 0.5 * _rms(h, nw_ref[1:2, :])


def _ffn(x, nw2, wg, wu, wd):
    s = x.shape[0]
    return pl.pallas_call(
        _ffn_kernel,
        grid=(s // ROW_TILE,),
        in_specs=[
            pl.BlockSpec((ROW_TILE, D_MODEL), lambda i: (i, 0)),
            _const_spec((2, D_MODEL)),
            _const_spec((D_MODEL, D_FF)),
            _const_spec((D_MODEL, D_FF)),
            _const_spec((D_FF, D_MODEL)),
        ],
        out_specs=pl.BlockSpec((ROW_TILE, D_MODEL), lambda i: (i, 0)),
        out_shape=jax.ShapeDtypeStruct((s, D_MODEL), F32),
        compiler_params=_params("parallel"),
        name="ffn",
    )(x, nw2, wg, wu, wd)


def _inproj_kernel(x_ref, nw_ref, w_ref, zhg_ref, q_ref, k_ref, vt_ref, ksum_ref, zcv_ref):
    xn = _rms(x_ref[...], nw_ref[...]).astype(BF16)
    c = 4 * HG_WIDTH
    zhg_ref[...] = jnp.dot(xn, w_ref[:, 0:c], preferred_element_type=F32)
    q_ref[...] = jnp.dot(xn, w_ref[:, c:c + AT_WIDTH], preferred_element_type=F32)
    k = jnp.dot(xn, w_ref[:, c + AT_WIDTH:c + 2 * AT_WIDTH], preferred_element_type=F32)
    v = jnp.dot(xn, w_ref[:, c + 2 * AT_WIDTH:c + 3 * AT_WIDTH], preferred_element_type=F32)
    c += 3 * AT_WIDTH
    zcv_ref[...] = jnp.dot(xn, w_ref[:, c:c + 3 * CV_WIDTH], preferred_element_type=F32)
    ones = jnp.ones((V_ROWS - AT_DIM, MOBA_BLOCK), F32)
    for b in range(ROW_TILE // MOBA_BLOCK):
        kb = k[b * MOBA_BLOCK:(b + 1) * MOBA_BLOCK, :]
        vbt = v[b * MOBA_BLOCK:(b + 1) * MOBA_BLOCK, :].T
        k_ref[b] = kb.astype(BF16)
        for h in range(AT_HEADS):
            vt_ref[b, h] = jnp.concatenate(
                [vbt[h * AT_DIM:(h + 1) * AT_DIM, :], ones], axis=0).astype(BF16)
        ksum_ref[b * KSUM_ROWS:(b + 1) * KSUM_ROWS, :] = jnp.sum(
            kb.reshape(MOBA_BLOCK // KSUM_ROWS, KSUM_ROWS, AT_WIDTH), axis=0)


def _inproj(x, nw, w_in):
    s = x.shape[0]
    nblk = s // MOBA_BLOCK
    bpt = ROW_TILE // MOBA_BLOCK
    return pl.pallas_call(
        _inproj_kernel,
        grid=(s // ROW_TILE,),
        in_specs=[
            pl.BlockSpec((ROW_TILE, D_MODEL), lambda i: (i, 0)),
            _const_spec((1, D_MODEL)),
            _const_spec((D_MODEL, D_IN)),
        ],
        out_specs=[
            pl.BlockSpec((ROW_TILE, 4 * HG_WIDTH), lambda i: (i, 0)),
            pl.BlockSpec((ROW_TILE, AT_WIDTH), lambda i: (i, 0)),
            pl.BlockSpec((bpt, MOBA_BLOCK, AT_WIDTH), lambda i: (i, 0, 0)),
            pl.BlockSpec((bpt, AT_HEADS, V_ROWS, MOBA_BLOCK), lambda i: (i, 0, 0, 0)),
            pl.BlockSpec((bpt * KSUM_ROWS, AT_WIDTH), lambda i: (i, 0)),
            pl.BlockSpec((ROW_TILE, 3 * CV_WIDTH), lambda i: (i, 0)),
        ],
        out_shape=[
            jax.ShapeDtypeStruct((s, 4 * HG_WIDTH), F32),
            jax.ShapeDtypeStruct((s, AT_WIDTH), F32),
            jax.ShapeDtypeStruct((nblk, MOBA_BLOCK, AT_WIDTH), BF16),
            jax.ShapeDtypeStruct((nblk, AT_HEADS, V_ROWS, MOBA_BLOCK), BF16),
            jax.ShapeDtypeStruct((nblk * KSUM_ROWS, AT_WIDTH), F32),
            jax.ShapeDtypeStruct((s, 3 * CV_WIDTH), F32),
        ],
        compiler_params=_params("parallel"),
        name="inproj",
    )(x, nw, w_in)


def _hg_levels():
    levels = []
    n = HG_TILE
    while n >= 2:
        levels.append(n)
        n //= 2
    return levels


def _hgrn_kernel(layer, q_ref, f_ref, i_ref, g_ref, lb_ref, nw_ref, o_ref, st_ref):
    t_idx = pl.program_id(1)

    @pl.when(t_idx == 0)
    def _():
        st_ref[...] = jnp.zeros_like(st_ref)

    lbraw = lb_ref[...]
    e = jnp.exp(lbraw - jnp.max(lbraw, axis=0, keepdims=True))
    soft = e / jnp.sum(e, axis=0, keepdims=True)
    lb = jnp.sum(soft[0:layer + 1, :], axis=0, keepdims=True) - soft[0:1, :]

    fp = f_ref[...]
    logf = jnp.log(lb + (1.0 - lb) * jax.nn.sigmoid(fp))
    kk = (1.0 - lb) * jax.nn.sigmoid(-fp)
    q = q_ref[...]
    v = i_ref[...]
    vb = v.astype(BF16)

    row = lax.broadcasted_iota(jnp.int32, (HG_TILE, HG_DIM), 0)
    b = logf
    sh = 1
    while sh < HG_TILE:
        b = b + jnp.where(row >= sh, pltpu.roll(b, sh, axis=0), 0.0)
        sh *= 2

    ti = lax.broadcasted_iota(jnp.int32, (HG_TILE, HG_TILE), 0)
    si = lax.broadcasted_iota(jnp.int32, (HG_TILE, HG_TILE), 1)
    scores = jnp.where(
        ti == si,
        lax.dot_general(q.astype(BF16), kk.astype(BF16), _NT, preferred_element_type=F32),
        0.0)
    for n in _hg_levels():
        half = n // 2
        if half >= 8:
            pieces = []
            for blk in range(HG_TILE // n):
                m = blk * n + half - 1
                pieces.append(jnp.broadcast_to(b[m:m + 1, :], (n, HG_DIM)))
            bm = pieces[0] if len(pieces) == 1 else jnp.concatenate(pieces, axis=0)
        else:
            b3 = b.reshape(HG_TILE // 8, 8, HG_DIM)
            sub = lax.broadcasted_iota(jnp.int32, (HG_TILE // 8, 8, HG_DIM), 1)
            bm3 = None
            for blk in range(8 // n):
                m = blk * n + half - 1
                piece = jnp.broadcast_to(b3[:, m:m + 1, :], b3.shape)
                bm3 = piece if bm3 is None else jnp.where(sub >= blk * n, piece, bm3)
            bm = bm3.reshape(HG_TILE, HG_DIM)
        upper = (row & (n - 1)) >= half
        ex = jnp.exp(jnp.where(upper, b - bm, bm - b))
        qs = jnp.where(upper, q * ex, 0.0).astype(BF16)
        ks = jnp.where(upper, 0.0, kk * ex).astype(BF16)
        lvl = lax.dot_general(qs, ks, _NT, preferred_element_type=F32)
        if n == HG_TILE:
            scores = scores + lvl
        else:
            scores = jnp.where((ti & -n) == (si & -n), scores + lvl, scores)

    st = st_ref[...]
    b_last = b[HG_TILE - 1:HG_TILE, :]
    o = jnp.dot(scores.astype(BF16), vb, preferred_element_type=F32)
    o = o + lax.dot_general((q * jnp.exp(b)).astype(BF16), st.astype(BF16), _NT,
                            preferred_element_type=F32)
    kdec = (kk * jnp.exp(b_last - b)).astype(BF16)
    st_ref[...] = st * jnp.exp(b_last) + lax.dot_general(vb, kdec, _TN, preferred_element_type=F32)

    gate = g_ref[...]
    o_ref[...] = (_rms(o, nw_ref[...]) * (gate * jax.nn.sigmoid(gate))).astype(o_ref.dtype)


def _hgrn(zhg, hg_lb, hg_nw, layer):
    s = zhg.shape[0]
    depth = hg_lb.shape[0]

    def col(k):
        return pl.BlockSpec((HG_TILE, HG_DIM), lambda h, t: (t, k * HG_HEADS + h))

    return pl.pallas_call(
        functools.partial(_hgrn_kernel, layer),
        grid=(HG_HEADS, s // HG_TILE),
        in_specs=[
            col(0), col(1), col(2), col(3),
            pl.BlockSpec((depth, HG_DIM), lambda h, t: (0, h)),
            pl.BlockSpec((1, HG_DIM), lambda h, t: (0, 0)),
        ],
        out_specs=pl.BlockSpec((HG_TILE, HG_DIM), lambda h, t: (t, h)),
        out_shape=jax.ShapeDtypeStruct((s, HG_WIDTH), BF16),
        scratch_shapes=[pltpu.VMEM((HG_DIM, HG_DIM), F32)],
        compiler_params=_params("parallel", "arbitrary"),
        name="hgrn2",
    )(zhg, zhg, zhg, zhg, hg_lb, hg_nw)


N_BIAS_TABLES = 5


def _bucket_thresholds():
    max_exact = REL_BUCKETS // 2
    d = np.arange(1, 2 * REL_MAX_DIST, dtype=np.float64)
    large = max_exact + (np.log(d / max_exact) / math.log(REL_MAX_DIST / max_exact)
                         * (REL_BUCKETS - max_exact)).astype(np.int64)
    large = np.minimum(large, REL_BUCKETS - 1)
    bucket = np.where(d < max_exact, d.astype(np.int64), large)
    thr = [0] * REL_BUCKETS
    for bkt in range(1, REL_BUCKETS):
        thr[bkt] = int(d[np.argmax(bucket >= bkt)])
    assert (N_BIAS_TABLES - 1) * MOBA_BLOCK + 1 >= thr[REL_BUCKETS - 1]
    return thr


def _bias_kernel(rb_ref, o_ref):
    thr = _bucket_thresholds()
    key = lax.broadcasted_iota(jnp.int32, (MOBA_BLOCK, MOBA_BLOCK), 0)
    qry = lax.broadcasted_iota(jnp.int32, (MOBA_BLOCK, MOBA_BLOCK), 1)
    o_ref[N_BIAS_TABLES:] = jnp.zeros((N_BIAS_TABLES - 1,) + o_ref.shape[1:], F32)
    for u in range(N_BIAS_TABLES):
        t = N_BIAS_TABLES - 1 - u
        dist = qry - key + t * MOBA_BLOCK
        for h in range(AT_HEADS):
            val = jnp.full((MOBA_BLOCK, MOBA_BLOCK), rb_ref[REL_BUCKETS - 1, h], F32)
            for bkt in range(REL_BUCKETS - 2, -1, -1):
                val = jnp.where(dist < thr[bkt + 1], rb_ref[bkt, h], val)
            val = val * LOG2E
            if t == 0:
                val = jnp.where(dist < 0, NEG, val)
            o_ref[u, :, h * MOBA_BLOCK:(h + 1) * MOBA_BLOCK] = val


def _bias_tables(rel_bias):
    return pl.pallas_call(
        _bias_kernel,
        in_specs=[pl.BlockSpec(memory_space=pltpu.SMEM)],
        out_shape=jax.ShapeDtypeStruct((2 * N_BIAS_TABLES - 1, MOBA_BLOCK, AT_HEADS * MOBA_BLOCK), F32),
        name="moba_bias",
    )(rel_bias)


def _moba_kernel(rb_ref, q_ref, k_ref, vt_ref, ksum_ref, bias_ref, o_ref,
                 kmean_ref, mnear_ref, mfar_ref, qs_ref, m_ref, acc_ref, sa_ref, sb_ref):
    i = pl.program_id(0)
    nblk = k_ref.shape[0]

    @pl.when(i == 0)
    def _():
        ks = ksum_ref[...].reshape(nblk, KSUM_ROWS, AT_WIDTH)
        kmean_ref[...] = jnp.sum(ks, axis=1) * (1.0 / MOBA_BLOCK)

    q = q_ref[...]
    lane_head = lax.broadcasted_iota(jnp.int32, (MOBA_BLOCK, AT_WIDTH), 1) // AT_DIM
    jio = lax.broadcasted_iota(jnp.int32, (nblk, MOBA_BLOCK), 0).astype(F32)
    fi = i.astype(F32)
    n_far = jnp.maximum(i - (N_BIAS_TABLES - 1), 0)
    kmean = kmean_ref[...]
    for h in range(AT_HEADS):
        qm = jnp.where(lane_head == h, q, 0.0)
        gate = lax.dot_general(kmean, qm, _NT, precision=lax.Precision.HIGHEST,
                               preferred_element_type=F32)
        gate = jnp.where(jio < fi, gate, -jnp.inf)
        sel = jio == fi
        for _ in range(MOBA_TOPK):
            mx = jnp.max(gate, axis=0, keepdims=True)
            cand = jnp.where(gate == mx, jio, float(nblk))
            idx = jnp.min(cand, axis=0, keepdims=True)
            pick = (jio == idx) & (mx > -jnp.inf)
            sel = sel | pick
            gate = jnp.where(pick, -jnp.inf, gate)
        cols = slice(h * MOBA_BLOCK, (h + 1) * MOBA_BLOCK)
        mnear_ref[:, cols] = jnp.where(sel, 0.0, NEG)
        mfar_ref[:, cols] = jnp.where(sel & (jio < n_far.astype(F32)),
                                      rb_ref[REL_BUCKETS - 1, h] * LOG2E, NEG)
        qs_ref[:, cols] = (qm * (AT_DIM ** -0.5 * LOG2E)).T.astype(BF16)
    m_ref[...] = jnp.full(m_ref.shape, NEG, F32)
    acc_ref[...] = jnp.zeros(acc_ref.shape, F32)

    def logits_into(s_ref, j0, nb):
        kg = k_ref[pl.ds(j0, nb)].reshape(nb * MOBA_BLOCK, AT_WIDTH)
        s_ref[0:nb * MOBA_BLOCK, :] = jnp.dot(kg, qs_ref[...], preferred_element_type=F32)

    def fold(s_ref, j0, nb, mask_ref, table):
        parts = []
        for g in range(nb):
            s = s_ref[g * MOBA_BLOCK:(g + 1) * MOBA_BLOCK, :]
            mrow = mask_ref[pl.ds(j0 + g, 1), :]
            if table is not None:
                s = s + mrow + bias_ref[table + g]
            mb = jnp.max(s, axis=0, keepdims=True)
            p = jnp.exp2(s - mb).astype(BF16)
            if table is None:
                mb = mb + mrow
            pv = jnp.concatenate(
                [jnp.dot(vt_ref[j0 + g, h], p[:, h * MOBA_BLOCK:(h + 1) * MOBA_BLOCK],
                         preferred_element_type=F32) for h in range(AT_HEADS)], axis=1)
            parts.append((mb, pv))
        m = m_ref[...]
        acc = acc_ref[...]
        for mb, pv in parts:
            m_new = jnp.maximum(m, mb)
            acc = jnp.exp2(m - m_new) * acc + jnp.exp2(mb - m_new) * pv
            m = m_new
        m_ref[...] = m
        acc_ref[...] = acc

    near0 = jnp.maximum(i - (N_BIAS_TABLES - 1), 0)
    n_groups = (n_far + FAR_GROUP - 1) // FAR_GROUP
    last_group = nblk // FAR_GROUP - 1
    logits_into(sa_ref, near0, N_BIAS_TABLES)
    logits_into(sb_ref, 0, FAR_GROUP)
    fold(sa_ref, near0, N_BIAS_TABLES, mnear_ref, near0 - (i - (N_BIAS_TABLES - 1)))

    def far_pair(gp, carry):
        ga = 2 * gp
        logits_into(sa_ref, FAR_GROUP * (ga + 1), FAR_GROUP)
        fold(sb_ref, FAR_GROUP * ga, FAR_GROUP, mfar_ref, None)
        logits_into(sb_ref, FAR_GROUP * jnp.minimum(ga + 2, last_group), FAR_GROUP)
        fold(sa_ref, FAR_GROUP * (ga + 1), FAR_GROUP, mfar_ref, None)
        return carry

    lax.fori_loop(0, (n_groups + 1) // 2, far_pair, 0)
    acc = acc_ref[...]
    ot = acc[0:AT_DIM, :] / acc[AT_DIM:AT_DIM + 1, :]
    o_ref[...] = jnp.concatenate(
        [ot[:, h * MOBA_BLOCK:(h + 1) * MOBA_BLOCK] for h in range(AT_HEADS)], axis=0
    ).T.astype(o_ref.dtype)


def _moba(rel_bias, q, k, vt, ksum, bias):
    s = q.shape[0]
    nblk = s // MOBA_BLOCK
    assert nblk >= N_BIAS_TABLES and FAR_GROUP <= N_BIAS_TABLES and nblk % FAR_GROUP == 0
    return pl.pallas_call(
        _moba_kernel,
        grid=(nblk,),
        in_specs=[
            pl.BlockSpec(memory_space=pltpu.SMEM),
            pl.BlockSpec((MOBA_BLOCK, AT_WIDTH), lambda i: (i, 0)),
            _const_spec((nblk, MOBA_BLOCK, AT_WIDTH)),
            _const_spec((nblk, AT_HEADS, V_ROWS, MOBA_BLOCK)),
            _const_spec((nblk * KSUM_ROWS, AT_WIDTH)),
            _const_spec((2 * N_BIAS_TABLES - 1, MOBA_BLOCK, AT_HEADS * MOBA_BLOCK)),
        ],
        out_specs=pl.BlockSpec((MOBA_BLOCK, AT_WIDTH), lambda i: (i, 0)),
        out_shape=jax.ShapeDtypeStruct((s, AT_WIDTH), BF16),
        scratch_shapes=[
            pltpu.VMEM((nblk, AT_WIDTH), F32),
            pltpu.VMEM((nblk, AT_HEADS * MOBA_BLOCK), F32),
            pltpu.VMEM((nblk, AT_HEADS * MOBA_BLOCK), F32),
            pltpu.VMEM((AT_WIDTH, AT_HEADS * MOBA_BLOCK), BF16),
            pltpu.VMEM((1, AT_HEADS * MOBA_BLOCK), F32),
            pltpu.VMEM((V_ROWS, AT_HEADS * MOBA_BLOCK), F32),
            pltpu.VMEM((N_BIAS_TABLES * MOBA_BLOCK, AT_HEADS * MOBA_BLOCK), F32),
            pltpu.VMEM((FAR_GROUP * MOBA_BLOCK, AT_HEADS * MOBA_BLOCK), F32),
        ],
        compiler_params=_params("arbitrary"),
        name="moba",
    )(rel_bias, q, k, vt, ksum, bias)


def _outproj_kernel(x_ref, ohg_ref, oat_ref, zcv_ref, halo_ref, cw_ref, w_ref, nw_ref, o_ref):
    i = pl.program_id(0)
    zcv = zcv_ref[...]
    bgate = zcv[:, 0:CV_WIDTH]
    u = zcv[:, CV_WIDTH:2 * CV_WIDTH] * zcv[:, 2 * CV_WIDTH:3 * CV_WIDTH]
    halo = halo_ref[...]
    uh = halo[:, CV_WIDTH:2 * CV_WIDTH] * halo[:, 2 * CV_WIDTH:3 * CV_WIDTH]
    uh = jnp.where(i > 0, uh, 0.0)
    row = lax.broadcasted_iota(jnp.int32, u.shape, 0)
    u1 = jnp.where(row == 0, uh[7:8, :], pltpu.roll(u, 1, axis=0))
    u2 = jnp.where(row == 0, uh[6:7, :], jnp.where(row == 1, uh[7:8, :], pltpu.roll(u, 2, axis=0)))
    cw = cw_ref[...]
    ocv = bgate * (cw[0:1, :] * u2 + cw[1:2, :] * u1 + cw[2:3, :] * u)
    h = jnp.dot(ohg_ref[...], w_ref[0:HG_WIDTH, :], preferred_element_type=F32)
    h = h + jnp.dot(oat_ref[...], w_ref[HG_WIDTH:HG_WIDTH + AT_WIDTH, :], preferred_element_type=F32)
    h = h + jnp.dot(ocv.astype(BF16), w_ref[HG_WIDTH + AT_WIDTH:D_MIX, :], preferred_element_type=F32)
    o_ref[...] = x_ref[...] + _rms(h, nw_ref[...])


def _outproj(x, ohg, oat, zcv, conv_w, w_out, nw):
    s = x.shape[0]
    halo_blocks = ROW_TILE // 8
    return pl.pallas_call(
        _outproj_kernel,
        grid=(s // ROW_TILE,),
        in_specs=[
            pl.BlockSpec((ROW_TILE, D_MODEL), lambda i: (i, 0)),
            pl.BlockSpec((ROW_TILE, HG_WIDTH), lambda i: (i, 0)),
            pl.BlockSpec((ROW_TILE, AT_WIDTH), lambda i: (i, 0)),
            pl.BlockSpec((ROW_TILE, 3 * CV_WIDTH), lambda i: (i, 0)),
            pl.BlockSpec((8, 3 * CV_WIDTH), lambda i: (jnp.maximum(i * halo_blocks - 1, 0), 0)),
            _const_spec((CV_KERNEL, CV_WIDTH)),
            _const_spec((D_MIX, D_MODEL)),
            _const_spec((1, D_MODEL)),
        ],
        out_specs=pl.BlockSpec((ROW_TILE, D_MODEL), lambda i: (i, 0)),
        out_shape=jax.ShapeDtypeStruct((s, D_MODEL), F32),
        compiler_params=_params("parallel"),
        name="outproj",
    )(x, ohg, oat, zcv, zcv, conv_w, w_out, nw)


def kernel(x, norm_w, ffn1_wg, ffn1_wu, ffn1_wd, mix_w_in, mix_w_out, hg_lb, hg_norm_w, conv_w,
           ffn2_wg, ffn2_wu, ffn2_wd, rel_bias):
    batch, seq, _ = x.shape
    depth = norm_w.shape[0]
    assert batch == 1 and seq % ROW_TILE == 0 and seq % MOBA_BLOCK == 0
    rel_bias = rel_bias.astype(F32)
    bias = _bias_tables(rel_bias)
    y = x.reshape(seq, D_MODEL)
    for l in range(depth):
        y = _ffn(y, norm_w[l, 0:2], ffn1_wg[l].astype(BF16), ffn1_wu[l].astype(BF16),
                 ffn1_wd[l].astype(BF16))
        zhg, q, k, vt, ksum, zcv = _inproj(y, norm_w[l, 2:3], mix_w_in[l].astype(BF16))
        ohg = _hgrn(zhg, hg_lb, hg_norm_w[l:l + 1], l)
        oat = _moba(rel_bias, q, k, vt, ksum, bias)
        y = _outproj(y, ohg, oat, zcv, conv_w[l], mix_w_out[l].astype(BF16), norm_w[l, 3:4])
        y = _ffn(y, norm_w[l, 4:6], ffn2_wg[l].astype(BF16), ffn2_wu[l].astype(BF16),
                 ffn2_wd[l].astype(BF16))
    return y.reshape(batch, seq, D_MODEL)
```

```python
import functools
import math

import numpy as np
import jax
import jax.numpy as jnp
from jax import lax
from jax.experimental import pallas as pl
from jax.experimental.pallas import tpu as pltpu

F32 = jnp.float32
BF16 = jnp.bfloat16

D_MODEL = 1024
D_FF = 2816
HG_HEADS = 4
HG_DIM = 128
HG_WIDTH = HG_HEADS * HG_DIM
AT_HEADS = 4
AT_DIM = 64
AT_WIDTH = AT_HEADS * AT_DIM
MOBA_BLOCK = 256
MOBA_TOPK = 3
REL_BUCKETS = 32
REL_MAX_DIST = 1024
CV_WIDTH = 256
CV_KERNEL = 3
D_MIX = HG_WIDTH + AT_WIDTH + CV_WIDTH
D_IN = 4 * HG_WIDTH + 3 * AT_WIDTH + 3 * CV_WIDTH
EPS = 1e-6

ROW_TILE = 512
HG_TILE = 256
KSUM_ROWS = 8
V_ROWS = AT_DIM + 16
LOG2E = math.log2(math.e)
FAR_GROUP = 4
NEG = -1e30
VMEM_LIMIT = 56 * 1024 * 1024
FF_CHUNKS = ((0, 768), (768, 1536), (1536, 2304), (2304, 2816))

_NT = (((1,), (1,)), ((), ()))
_TN = (((0,), (0,)), ((), ()))


def _rms(x, w):
    ms = jnp.mean(x * x, axis=-1, keepdims=True)
    return x * lax.rsqrt(ms + EPS) * w


def _const_spec(shape):
    nd = len(shape)
    return pl.BlockSpec(shape, lambda *_: (0,) * nd, pipeline_mode=pl.Buffered(1))


def _params(*sem, flags=None):
    return pltpu.CompilerParams(dimension_semantics=sem, vmem_limit_bytes=VMEM_LIMIT, flags=flags)


def _ffn_kernel(x_ref, nw_ref, wg_ref, wu_ref, wd_ref, o_ref):
    x = x_ref[...]
    xn = _rms(x, nw_ref[0:1, :]).astype(BF16)
    h = None
    for c0, c1 in FF_CHUNKS:
        g = jnp.dot(xn, wg_ref[:, c0:c1], preferred_element_type=F32)
        u = jnp.dot(xn, wu_ref[:, c0:c1], preferred_element_type=F32)
        a = (g * jax.nn.sigmoid(g) * u).astype(BF16)
        part = jnp.dot(a, wd_ref[c0:c1, :], preferred_element_type=F32)
        h = part if h is None else h + part
    o_ref[...] = x + 0.5 * _rms(h, nw_ref[1:2, :])


def _ffn(x, nw2, wg, wu, wd):
    s = x.shape[0]
    return pl.pallas_call(
        _ffn_kernel,
        grid=(s // ROW_TILE,),
        in_specs=[
            pl.BlockSpec((ROW_TILE, D_MODEL), lambda i: (i, 0)),
            _const_spec((2, D_MODEL)),
            _const_spec((D_MODEL, D_FF)),
            _const_spec((D_MODEL, D_FF)),
            _const_spec((D_FF, D_MODEL)),
        ],
        out_specs=pl.BlockSpec((ROW_TILE, D_MODEL), lambda i: (i, 0)),
        out_shape=jax.ShapeDtypeStruct((s, D_MODEL), F32),
        compiler_params=_params("parallel"),
        name="ffn",
    )(x, nw2, wg, wu, wd)


def _inproj_kernel(x_ref, nw_ref, w_ref, zhg_ref, q_ref, k_ref, vt_ref, ksum_ref, zcv_ref):
    xn = _rms(x_ref[...], nw_ref[...]).astype(BF16)
    c = 4 * HG_WIDTH
    zhg_ref[...] = jnp.dot(xn, w_ref[:, 0:c], preferred_element_type=F32)
    q_ref[...] = jnp.dot(xn, w_ref[:, c:c + AT_WIDTH], preferred_element_type=F32)
    k = jnp.dot(xn, w_ref[:, c + AT_WIDTH:c + 2 * AT_WIDTH], preferred_element_type=F32)
    v = jnp.dot(xn, w_ref[:, c + 2 * AT_WIDTH:c + 3 * AT_WIDTH], preferred_element_type=F32)
    c += 3 * AT_WIDTH
    zcv_ref[...] = jnp.dot(xn, w_ref[:, c:c + 3 * CV_WIDTH], preferred_element_type=F32)
    ones = jnp.ones((V_ROWS - AT_DIM, MOBA_BLOCK), F32)
    for b in range(ROW_TILE // MOBA_BLOCK):
        kb = k[b * MOBA_BLOCK:(b + 1) * MOBA_BLOCK, :]
        vbt = v[b * MOBA_BLOCK:(b + 1) * MOBA_BLOCK, :].T
        k_ref[b] = kb.astype(BF16)
        for h in range(AT_HEADS):
            vt_ref[b, h] = jnp.concatenate(
                [vbt[h * AT_DIM:(h + 1) * AT_DIM, :], ones], axis=0).astype(BF16)
        ksum_ref[b * KSUM_ROWS:(b + 1) * KSUM_ROWS, :] = jnp.sum(
            kb.reshape(MOBA_BLOCK // KSUM_ROWS, KSUM_ROWS, AT_WIDTH), axis=0)


def _inproj(x, nw, w_in):
    s = x.shape[0]
    nblk = s // MOBA_BLOCK
    bpt = ROW_TILE // MOBA_BLOCK
    return pl.pallas_call(
        _inproj_kernel,
        grid=(s // ROW_TILE,),
        in_specs=[
            pl.BlockSpec((ROW_TILE, D_MODEL), lambda i: (i, 0)),
            _const_spec((1, D_MODEL)),
            _const_spec((D_MODEL, D_IN)),
        ],
        out_specs=[
            pl.BlockSpec((ROW_TILE, 4 * HG_WIDTH), lambda i: (i, 0)),
            pl.BlockSpec((ROW_TILE, AT_WIDTH), lambda i: (i, 0)),
            pl.BlockSpec((bpt, MOBA_BLOCK, AT_WIDTH), lambda i: (i, 0, 0)),
            pl.BlockSpec((bpt, AT_HEADS, V_ROWS, MOBA_BLOCK), lambda i: (i, 0, 0, 0)),
            pl.BlockSpec((bpt * KSUM_ROWS, AT_WIDTH), lambda i: (i, 0)),
            pl.BlockSpec((ROW_TILE, 3 * CV_WIDTH), lambda i: (i, 0)),
        ],
        out_shape=[
            jax.ShapeDtypeStruct((s, 4 * HG_WIDTH), F32),
            jax.ShapeDtypeStruct((s, AT_WIDTH), F32),
            jax.ShapeDtypeStruct((nblk, MOBA_BLOCK, AT_WIDTH), BF16),
            jax.ShapeDtypeStruct((nblk, AT_HEADS, V_ROWS, MOBA_BLOCK), BF16),
            jax.ShapeDtypeStruct((nblk * KSUM_ROWS, AT_WIDTH), F32),
            jax.ShapeDtypeStruct((s, 3 * CV_WIDTH), F32),
        ],
        compiler_params=_params("parallel"),
        name="inproj",
    )(x, nw, w_in)


def _hg_levels():
    levels = []
    n = HG_TILE
    while n >= 2:
        levels.append(n)
        n //= 2
    return levels


def _hgrn_kernel(layer, q_ref, f_ref, i_ref, g_ref, lb_ref, nw_ref, o_ref, st_ref):
    t_idx = pl.program_id(1)

    @pl.when(t_idx == 0)
    def _():
        st_ref[...] = jnp.zeros_like(st_ref)

    lbraw = lb_ref[...]
    e = jnp.exp(lbraw - jnp.max(lbraw, axis=0, keepdims=True))
    soft = e / jnp.sum(e, axis=0, keepdims=True)
    lb = jnp.sum(soft[0:layer + 1, :], axis=0, keepdims=True) - soft[0:1, :]

    fp = f_ref[...]
    logf = jnp.log(lb + (1.0 - lb) * jax.nn.sigmoid(fp))
    kk = (1.0 - lb) * jax.nn.sigmoid(-fp)
    q = q_ref[...]
    v = i_ref[...]
    vb = v.astype(BF16)

    row = lax.broadcasted_iota(jnp.int32, (HG_TILE, HG_DIM), 0)
    b = logf
    sh = 1
    while sh < HG_TILE:
        b = b + jnp.where(row >= sh, pltpu.roll(b, sh, axis=0), 0.0)
        sh *= 2

    ti = lax.broadcasted_iota(jnp.int32, (HG_TILE, HG_TILE), 0)
    si = lax.broadcasted_iota(jnp.int32, (HG_TILE, HG_TILE), 1)
    scores = jnp.where(
        ti == si,
        lax.dot_general(q.astype(BF16), kk.astype(BF16), _NT, preferred_element_type=F32),
        0.0)
    for n in _hg_levels():
        half = n // 2
        if half >= 8:
            pieces = []
            for blk in range(HG_TILE // n):
                m = blk * n + half - 1
                pieces.append(jnp.broadcast_to(b[m:m + 1, :], (n, HG_DIM)))
            bm = pieces[0] if len(pieces) == 1 else jnp.concatenate(pieces, axis=0)
        else:
            b3 = b.reshape(HG_TILE // 8, 8, HG_DIM)
            sub = lax.broadcasted_iota(jnp.int32, (HG_TILE // 8, 8, HG_DIM), 1)
            bm3 = None
            for blk in range(8 // n):
                m = blk * n + half - 1
                piece = jnp.broadcast_to(b3[:, m:m + 1, :], b3.shape)
                bm3 = piece if bm3 is None else jnp.where(sub >= blk * n, piece, bm3)
            bm = bm3.reshape(HG_TILE, HG_DIM)
        upper = (row & (n - 1)) >= half
        ex = jnp.exp(jnp.where(upper, b - bm, bm - b))
        qs = jnp.where(upper, q * ex, 0.0).astype(BF16)
        ks = jnp.where(upper, 0.0, kk * ex).astype(BF16)
        lvl = lax.dot_general(qs, ks, _NT, preferred_element_type=F32)
        if n == HG_TILE:
            scores = scores + lvl
        else:
            scores = jnp.where((ti & -n) == (si & -n), scores + lvl, scores)

    st = st_ref[...]
    b_last = b[HG_TILE - 1:HG_TILE, :]
    o = jnp.dot(scores.astype(BF16), vb, preferred_element_type=F32)
    o = o + lax.dot_general((q * jnp.exp(b)).astype(BF16), st.astype(BF16), _NT,
                            preferred_element_type=F32)
    kdec = (kk * jnp.exp(b_last - b)).astype(BF16)
    st_ref[...] = st * jnp.exp(b_last) + lax.dot_general(vb, kdec, _TN, preferred_element_type=F32)

    gate = g_ref[...]
    o_ref[...] = (_rms(o, nw_ref[...]) * (gate * jax.nn.sigmoid(gate))).astype(o_ref.dtype)


def _hgrn(zhg, hg_lb, hg_nw, layer):
    s = zhg.shape[0]
    depth = hg_lb.shape[0]

    def col(k):
        return pl.BlockSpec((HG_TILE, HG_DIM), lambda h, t: (t, k * HG_HEADS + h))

    return pl.pallas_call(
        functools.partial(_hgrn_kernel, layer),
        grid=(HG_HEADS, s // HG_TILE),
        in_specs=[
            col(0), col(1), col(2), col(3),
            pl.BlockSpec((depth, HG_DIM), lambda h, t: (0, h)),
            pl.BlockSpec((1, HG_DIM), lambda h, t: (0, 0)),
        ],
        out_specs=pl.BlockSpec((HG_TILE, HG_DIM), lambda h, t: (t, h)),
        out_shape=jax.ShapeDtypeStruct((s, HG_WIDTH), BF16),
        scratch_shapes=[pltpu.VMEM((HG_DIM, HG_DIM), F32)],
        compiler_params=_params("parallel", "arbitrary"),
        name="hgrn2",
    )(zhg, zhg, zhg, zhg, hg_lb, hg_nw)


N_BIAS_TABLES = 5


def _bucket_thresholds():
    max_exact = REL_BUCKETS // 2
    d = np.arange(1, 2 * REL_MAX_DIST, dtype=np.float64)
    large = max_exact + (np.log(d / max_exact) / math.log(REL_MAX_DIST / max_exact)
                         * (REL_BUCKETS - max_exact)).astype(np.int64)
    large = np.minimum(large, REL_BUCKETS - 1)
    bucket = np.where(d < max_exact, d.astype(np.int64), large)
    thr = [0] * REL_BUCKETS
    for bkt in range(1, REL_BUCKETS):
        thr[bkt] = int(d[np.argmax(bucket >= bkt)])
    assert (N_BIAS_TABLES - 1) * MOBA_BLOCK + 1 >= thr[REL_BUCKETS - 1]
    return thr


def _bias_kernel(rb_ref, o_ref):
    thr = _bucket_thresholds()
    key = lax.broadcasted_iota(jnp.int32, (MOBA_BLOCK, MOBA_BLOCK), 0)
    qry = lax.broadcasted_iota(jnp.int32, (MOBA_BLOCK, MOBA_BLOCK), 1)
    o_ref[N_BIAS_TABLES:] = jnp.zeros((N_BIAS_TABLES - 1,) + o_ref.shape[1:], F32)
    for u in range(N_BIAS_TABLES):
        t = N_BIAS_TABLES - 1 - u
        dist = qry - key + t * MOBA_BLOCK
        for h in range(AT_HEADS):
            val = jnp.full((MOBA_BLOCK, MOBA_BLOCK), rb_ref[REL_BUCKETS - 1, h], F32)
            for bkt in range(REL_BUCKETS - 2, -1, -1):
                val = jnp.where(dist < thr[bkt + 1], rb_ref[bkt, h], val)
            val = val * LOG2E
            if t == 0:
                val = jnp.where(dist < 0, NEG, val)
            o_ref[u, :, h * MOBA_BLOCK:(h + 1) * MOBA_BLOCK] = val


def _bias_tables(rel_bias):
    return pl.pallas_call(
        _bias_kernel,
        in_specs=[pl.BlockSpec(memory_space=pltpu.SMEM)],
        out_shape=jax.ShapeDtypeStruct((2 * N_BIAS_TABLES - 1, MOBA_BLOCK, AT_HEADS * MOBA_BLOCK), F32),
        name="moba_bias",
    )(rel_bias)


def _moba_kernel(rb_ref, q_ref, k_ref, vt_ref, ksum_ref, bias_ref, o_ref,
                 kmean_ref, mnear_ref, mfar_ref, qs_ref, m_ref, acc_ref,
                 sa_ref, sb_ref, pa_ref, pb_ref, mba_ref, mbb_ref):
    i = pl.program_id(0)
    nblk = k_ref.shape[0]

    @pl.when(i == 0)
    def _():
        ks = ksum_ref[...].reshape(nblk, KSUM_ROWS, AT_WIDTH)
        kmean_ref[...] = jnp.sum(ks, axis=1) * (1.0 / MOBA_BLOCK)

    q = q_ref[...]
    lane_head = lax.broadcasted_iota(jnp.int32, (MOBA_BLOCK, AT_WIDTH), 1) // AT_DIM
    jio = lax.broadcasted_iota(jnp.int32, (nblk, MOBA_BLOCK), 0).astype(F32)
    fi = i.astype(F32)
    n_far = jnp.maximum(i - (N_BIAS_TABLES - 1), 0)
    kmean = kmean_ref[...]
    for h in range(AT_HEADS):
        qm = jnp.where(lane_head == h, q, 0.0)
        gate = lax.dot_general(kmean, qm, _NT, precision=lax.Precision.HIGHEST,
                               preferred_element_type=F32)
        gate = jnp.where(jio < fi, gate, -jnp.inf)
        sel = jio == fi
        for _ in range(MOBA_TOPK):
            mx = jnp.max(gate, axis=0, keepdims=True)
            cand = jnp.where(gate == mx, jio, float(nblk))
            idx = jnp.min(cand, axis=0, keepdims=True)
            pick = (jio == idx) & (mx > -jnp.inf)
            sel = sel | pick
            gate = jnp.where(pick, -jnp.inf, gate)
        cols = slice(h * MOBA_BLOCK, (h + 1) * MOBA_BLOCK)
        mnear_ref[:, cols] = jnp.where(sel, 0.0, NEG)
        mfar_ref[:, cols] = jnp.where(sel & (jio < n_far.astype(F32)),
                                      rb_ref[REL_BUCKETS - 1, h] * LOG2E, NEG)
        qs_ref[:, cols] = (qm * (AT_DIM ** -0.5 * LOG2E)).T.astype(BF16)
    m_ref[...] = jnp.full(m_ref.shape, NEG, F32)
    acc_ref[...] = jnp.zeros(acc_ref.shape, F32)

    def logits_into(s_ref, j0, nb):
        kg = k_ref[pl.ds(j0, nb)].reshape(nb * MOBA_BLOCK, AT_WIDTH)
        s_ref[0:nb * MOBA_BLOCK, :] = jnp.dot(kg, qs_ref[...], preferred_element_type=F32)

    def weights_into(s_ref, p_ref, mb_ref, j0, nb, mask_ref, table):
        for g in range(nb):
            rows = slice(g * MOBA_BLOCK, (g + 1) * MOBA_BLOCK)
            s = s_ref[rows, :]
            mrow = mask_ref[pl.ds(j0 + g, 1), :]
            if table is not None:
                s = s + mrow + bias_ref[table + g]
            mb = jnp.max(s, axis=0, keepdims=True)
            p_ref[rows, :] = jnp.exp2(s - mb).astype(BF16)
            if table is None:
                mb = mb + mrow
            mb_ref[g:g + 1, :] = mb

    def fold(p_ref, mb_ref, j0, nb):
        m = m_ref[...]
        acc = acc_ref[...]
        for g in range(nb):
            pv = jnp.concatenate(
                [jnp.dot(vt_ref[j0 + g, h],
                         p_ref[g * MOBA_BLOCK:(g + 1) * MOBA_BLOCK, h * MOBA_BLOCK:(h + 1) * MOBA_BLOCK],
                         preferred_element_type=F32) for h in range(AT_HEADS)], axis=1)
            mb = mb_ref[g:g + 1, :]
            m_new = jnp.maximum(m, mb)
            acc = jnp.exp2(m - m_new) * acc + jnp.exp2(mb - m_new) * pv
            m = m_new
        m_ref[...] = m
        acc_ref[...] = acc

    near0 = jnp.maximum(i - (N_BIAS_TABLES - 1), 0)
    n_groups = (n_far + FAR_GROUP - 1) // FAR_GROUP
    last_group = nblk // FAR_GROUP - 1

    def start(g):
        return FAR_GROUP * jnp.minimum(g, last_group)

    logits_into(sa_ref, near0, N_BIAS_TABLES)
    logits_into(sb_ref, start(0), FAR_GROUP)
    weights_into(sa_ref, pa_ref, mba_ref, near0, N_BIAS_TABLES, mnear_ref,
                 near0 - (i - (N_BIAS_TABLES - 1)))
    fold(pa_ref, mba_ref, near0, N_BIAS_TABLES)
    logits_into(sa_ref, start(1), FAR_GROUP)
    weights_into(sb_ref, pb_ref, mbb_ref, start(0), FAR_GROUP, mfar_ref, None)

    def far_pair(gp, carry):
        g = 2 * gp
        logits_into(sb_ref, start(g + 2), FAR_GROUP)
        weights_into(sa_ref, pa_ref, mba_ref, start(g + 1), FAR_GROUP, mfar_ref, None)
        fold(pb_ref, mbb_ref, start(g), FAR_GROUP)
        logits_into(sa_ref, start(g + 3), FAR_GROUP)
        weights_into(sb_ref, pb_ref, mbb_ref, start(g + 2), FAR_GROUP, mfar_ref, None)
        fold(pa_ref, mba_ref, start(g + 1), FAR_GROUP)
        return carry

    lax.fori_loop(0, (n_groups + 1) // 2, far_pair, 0)
    acc = acc_ref[...]
    ot = acc[0:AT_DIM, :] / acc[AT_DIM:AT_DIM + 1, :]
    o_ref[...] = jnp.concatenate(
        [ot[:, h * MOBA_BLOCK:(h + 1) * MOBA_BLOCK] for h in range(AT_HEADS)], axis=0
    ).T.astype(o_ref.dtype)


def _moba(rel_bias, q, k, vt, ksum, bias):
    s = q.shape[0]
    nblk = s // MOBA_BLOCK
    assert nblk >= N_BIAS_TABLES and FAR_GROUP <= N_BIAS_TABLES and nblk % FAR_GROUP == 0
    return pl.pallas_call(
        _moba_kernel,
        grid=(nblk,),
        in_specs=[
            pl.BlockSpec(memory_space=pltpu.SMEM),
            pl.BlockSpec((MOBA_BLOCK, AT_WIDTH), lambda i: (i, 0)),
            _const_spec((nblk, MOBA_BLOCK, AT_WIDTH)),
            _const_spec((nblk, AT_HEADS, V_ROWS, MOBA_BLOCK)),
            _const_spec((nblk * KSUM_ROWS, AT_WIDTH)),
            _const_spec((2 * N_BIAS_TABLES - 1, MOBA_BLOCK, AT_HEADS * MOBA_BLOCK)),
        ],
        out_specs=pl.BlockSpec((MOBA_BLOCK, AT_WIDTH), lambda i: (i, 0)),
        out_shape=jax.ShapeDtypeStruct((s, AT_WIDTH), BF16),
        scratch_shapes=[
            pltpu.VMEM((nblk, AT_WIDTH), F32),
            pltpu.VMEM((nblk, AT_HEADS * MOBA_BLOCK), F32),
            pltpu.VMEM((nblk, AT_HEADS * MOBA_BLOCK), F32),
            pltpu.VMEM((AT_WIDTH, AT_HEADS * MOBA_BLOCK), BF16),
            pltpu.VMEM((1, AT_HEADS * MOBA_BLOCK), F32),
            pltpu.VMEM((V_ROWS, AT_HEADS * MOBA_BLOCK), F32),
            pltpu.VMEM((N_BIAS_TABLES * MOBA_BLOCK, AT_HEADS * MOBA_BLOCK), F32),
            pltpu.VMEM((FAR_GROUP * MOBA_BLOCK, AT_HEADS * MOBA_BLOCK), F32),
            pltpu.VMEM((N_BIAS_TABLES * MOBA_BLOCK, AT_HEADS * MOBA_BLOCK), BF16),
            pltpu.VMEM((FAR_GROUP * MOBA_BLOCK, AT_HEADS * MOBA_BLOCK), BF16),
            pltpu.VMEM((8, AT_HEADS * MOBA_BLOCK), F32),
            pltpu.VMEM((8, AT_HEADS * MOBA_BLOCK), F32),
        ],
        compiler_params=_params("arbitrary"),
        name="moba",
    )(rel_bias, q, k, vt, ksum, bias)


def _outproj_kernel(x_ref, ohg_ref, oat_ref, zcv_ref, halo_ref, cw_ref, w_ref, nw_ref, o_ref):
    i = pl.program_id(0)
    zcv = zcv_ref[...]
    bgate = zcv[:, 0:CV_WIDTH]
    u = zcv[:, CV_WIDTH:2 * CV_WIDTH] * zcv[:, 2 * CV_WIDTH:3 * CV_WIDTH]
    halo = halo_ref[...]
    uh = halo[:, CV_WIDTH:2 * CV_WIDTH] * halo[:, 2 * CV_WIDTH:3 * CV_WIDTH]
    uh = jnp.where(i > 0, uh, 0.0)
    row = lax.broadcasted_iota(jnp.int32, u.shape, 0)
    u1 = jnp.where(row == 0, uh[7:8, :], pltpu.roll(u, 1, axis=0))
    u2 = jnp.where(row == 0, uh[6:7, :], jnp.where(row == 1, uh[7:8, :], pltpu.roll(u, 2, axis=0)))
    cw = cw_ref[...]
    ocv = bgate * (cw[0:1, :] * u2 + cw[1:2, :] * u1 + cw[2:3, :] * u)
    h = jnp.dot(ohg_ref[...], w_ref[0:HG_WIDTH, :], preferred_element_type=F32)
    h = h + jnp.dot(oat_ref[...], w_ref[HG_WIDTH:HG_WIDTH + AT_WIDTH, :], preferred_element_type=F32)
    h = h + jnp.dot(ocv.astype(BF16), w_ref[HG_WIDTH + AT_WIDTH:D_MIX, :], preferred_element_type=F32)
    o_ref[...] = x_ref[...] + _rms(h, nw_ref[...])


def _outproj(x, ohg, oat, zcv, conv_w, w_out, nw):
    s = x.shape[0]
    halo_blocks = ROW_TILE // 8
    return pl.pallas_call(
        _outproj_kernel,
        grid=(s // ROW_TILE,),
        in_specs=[
            pl.BlockSpec((ROW_TILE, D_MODEL), lambda i: (i, 0)),
            pl.BlockSpec((ROW_TILE, HG_WIDTH), lambda i: (i, 0)),
            pl.BlockSpec((ROW_TILE, AT_WIDTH), lambda i: (i, 0)),
            pl.BlockSpec((ROW_TILE, 3 * CV_WIDTH), lambda i: (i, 0)),
            pl.BlockSpec((8, 3 * CV_WIDTH), lambda i: (jnp.maximum(i * halo_blocks - 1, 0), 0)),
            _const_spec((CV_KERNEL, CV_WIDTH)),
            _const_spec((D_MIX, D_MODEL)),
            _const_spec((1, D_MODEL)),
        ],
        out_specs=pl.BlockSpec((ROW_TILE, D_MODEL), lambda i: (i, 0)),
        out_shape=jax.ShapeDtypeStruct((s, D_MODEL), F32),
        compiler_params=_params("parallel"),
        name="outproj",
    )(x, ohg, oat, zcv, zcv, conv_w, w_out, nw)


def kernel(x, norm_w, ffn1_wg, ffn1_wu, ffn1_wd, mix_w_in, mix_w_out, hg_lb, hg_norm_w, conv_w,
           ffn2_wg, ffn2_wu, ffn2_wd, rel_bias):
    batch, seq, _ = x.shape
    depth = norm_w.shape[0]
    assert batch == 1 and seq % ROW_TILE == 0 and seq % MOBA_BLOCK == 0
    rel_bias = rel_bias.astype(F32)
    bias = _bias_tables(rel_bias)
    y = x.reshape(seq, D_MODEL)
    for l in range(depth):
        y = _ffn(y, norm_w[l, 0:2], ffn1_wg[l].astype(BF16), ffn1_wu[l].astype(BF16),
                 ffn1_wd[l].astype(BF16))
        zhg, q, k, vt, ksum, zcv = _inproj(y, norm_w[l, 2:3], mix_w_in[l].astype(BF16))
        ohg = _hgrn(zhg, hg_lb, hg_norm_w[l:l + 1], l)
        oat = _moba(rel_bias, q, k, vt, ksum, bias)
        y = _outproj(y, ohg, oat, zcv, conv_w[l], mix_w_out[l].astype(BF16), norm_w[l, 3:4])
        y = _ffn(y, norm_w[l, 4:6], ffn2_wg[l].astype(BF16), ffn2_wu[l].astype(BF16),
                 ffn2_wd[l].astype(BF16))
    return y.reshape(batch, seq, D_MODEL)
```

```python
import functools
import math

import numpy as np
import jax
import jax.numpy as jnp
from jax import lax
from jax.experimental import pallas as pl
from jax.experimental.pallas import tpu as pltpu

F32 = jnp.float32
BF16 = jnp.bfloat16

D_MODEL = 1024
D_FF = 2816
HG_HEADS = 4
HG_DIM = 128
HG_WIDTH = HG_HEADS * HG_DIM
AT_HEADS = 4
AT_DIM = 64
AT_WIDTH = AT_HEADS * AT_DIM
MOBA_BLOCK = 256
MOBA_TOPK = 3
REL_BUCKETS = 32
REL_MAX_DIST = 1024
CV_WIDTH = 256
CV_KERNEL = 3
D_MIX = HG_WIDTH + AT_WIDTH + CV_WIDTH
D_IN = 4 * HG_WIDTH + 3 * AT_WIDTH + 3 * CV_WIDTH
EPS = 1e-6

ROW_TILE = 512
HG_TILE = 256
KSUM_ROWS = 8
V_ROWS = AT_DIM + 16
LOG2E = math.log2(math.e)
FAR_GROUP = 4
NEG = -1e30
VMEM_LIMIT = 56 * 1024 * 1024
FF_CHUNKS = ((0, 768), (768, 1536), (1536, 2304), (2304, 2816))

_NT = (((1,), (1,)), ((), ()))
_TN = (((0,), (0,)), ((), ()))


def _rms(x, w):
    ms = jnp.mean(x * x, axis=-1, keepdims=True)
    return x * lax.rsqrt(ms + EPS) * w


def _const_spec(shape):
    nd = len(shape)
    return pl.BlockSpec(shape, lambda *_: (0,) * nd, pipeline_mode=pl.Buffered(1))


def _params(*sem, flags=None):
    return pltpu.CompilerParams(dimension_semantics=sem, vmem_limit_bytes=VMEM_LIMIT, flags=flags)


def _ffn_kernel(x_ref, nw_ref, wg_ref, wu_ref, wd_ref, o_ref):
    x = x_ref[...]
    xn = _rms(x, nw_ref[0:1, :]).astype(BF16)
    h = None
    for c0, c1 in FF_CHUNKS:
        g = jnp.dot(xn, wg_ref[:, c0:c1], preferred_element_type=F32)
        u = jnp.dot(xn, wu_ref[:, c0:c1], preferred_element_type=F32)
        a = (g * jax.nn.sigmoid(g) * u).astype(BF16)
        part = jnp.dot(a, wd_ref[c0:c1, :], preferred_element_type=F32)
        h = part if h is None else h + part
    o_ref[...] = x + 0.5 * _rms(h, nw_ref[1:2, :])


def _ffn(x, nw2, wg, wu, wd):
    s = x.shape[0]
    return pl.pallas_call(
        _ffn_kernel,
        grid=(s // ROW_TILE,),
        in_specs=[
            pl.BlockSpec((ROW_TILE, D_MODEL), lambda i: (i, 0)),
            _const_spec((2, D_MODEL)),
            _const_spec((D_MODEL, D_FF)),
            _const_spec((D_MODEL, D_FF)),
            _const_spec((D_FF, D_MODEL)),
        ],
        out_specs=pl.BlockSpec((ROW_TILE, D_MODEL), lambda i: (i, 0)),
        out_shape=jax.ShapeDtypeStruct((s, D_MODEL), F32),
        compiler_params=_params("parallel"),
        name="ffn",
    )(x, nw2, wg, wu, wd)


def _inproj_kernel(x_ref, nw_ref, w_ref, zhg_ref, q_ref, k_ref, vt_ref, ksum_ref, zcv_ref):
    xn = _rms(x_ref[...], nw_ref[...]).astype(BF16)
    c = 4 * HG_WIDTH
    zhg_ref[...] = jnp.dot(xn, w_ref[:, 0:c], preferred_element_type=F32)
    q_ref[...] = jnp.dot(xn, w_ref[:, c:c + AT_WIDTH], preferred_element_type=F32)
    k = jnp.dot(xn, w_ref[:, c + AT_WIDTH:c + 2 * AT_WIDTH], preferred_element_type=F32)
    v = jnp.dot(xn, w_ref[:, c + 2 * AT_WIDTH:c + 3 * AT_WIDTH], preferred_element_type=F32)
    c += 3 * AT_WIDTH
    zcv_ref[...] = jnp.dot(xn, w_ref[:, c:c + 3 * CV_WIDTH], preferred_element_type=F32)
    ones = jnp.ones((V_ROWS - AT_DIM, MOBA_BLOCK), F32)
    for b in range(ROW_TILE // MOBA_BLOCK):
        kb = k[b * MOBA_BLOCK:(b + 1) * MOBA_BLOCK, :]
        vbt = v[b * MOBA_BLOCK:(b + 1) * MOBA_BLOCK, :].T
        k_ref[b] = kb.astype(BF16)
        for h in range(AT_HEADS):
            vt_ref[b, h] = jnp.concatenate(
                [vbt[h * AT_DIM:(h + 1) * AT_DIM, :], ones], axis=0).astype(BF16)
        ksum_ref[b * KSUM_ROWS:(b + 1) * KSUM_ROWS, :] = jnp.sum(
            kb.reshape(MOBA_BLOCK // KSUM_ROWS, KSUM_ROWS, AT_WIDTH), axis=0)


def _inproj(x, nw, w_in):
    s = x.shape[0]
    nblk = s // MOBA_BLOCK
    bpt = ROW_TILE // MOBA_BLOCK
    return pl.pallas_call(
        _inproj_kernel,
        grid=(s // ROW_TILE,),
        in_specs=[
            pl.BlockSpec((ROW_TILE, D_MODEL), lambda i: (i, 0)),
            _const_spec((1, D_MODEL)),
            _const_spec((D_MODEL, D_IN)),
        ],
        out_specs=[
            pl.BlockSpec((ROW_TILE, 4 * HG_WIDTH), lambda i: (i, 0)),
            pl.BlockSpec((ROW_TILE, AT_WIDTH), lambda i: (i, 0)),
            pl.BlockSpec((bpt, MOBA_BLOCK, AT_WIDTH), lambda i: (i, 0, 0)),
            pl.BlockSpec((bpt, AT_HEADS, V_ROWS, MOBA_BLOCK), lambda i: (i, 0, 0, 0)),
            pl.BlockSpec((bpt * KSUM_ROWS, AT_WIDTH), lambda i: (i, 0)),
            pl.BlockSpec((ROW_TILE, 3 * CV_WIDTH), lambda i: (i, 0)),
        ],
        out_shape=[
            jax.ShapeDtypeStruct((s, 4 * HG_WIDTH), F32),
            jax.ShapeDtypeStruct((s, AT_WIDTH), F32),
            jax.ShapeDtypeStruct((nblk, MOBA_BLOCK, AT_WIDTH), BF16),
            jax.ShapeDtypeStruct((nblk, AT_HEADS, V_ROWS, MOBA_BLOCK), BF16),
            jax.ShapeDtypeStruct((nblk * KSUM_ROWS, AT_WIDTH), F32),
            jax.ShapeDtypeStruct((s, 3 * CV_WIDTH), F32),
        ],
        compiler_params=_params("parallel"),
        name="inproj",
    )(x, nw, w_in)


def _hg_levels():
    levels = []
    n = HG_TILE
    while n >= 2:
        levels.append(n)
        n //= 2
    return levels


def _hgrn_kernel(layer, q_ref, f_ref, i_ref, g_ref, lb_ref, nw_ref, o_ref, st_ref):
    t_idx = pl.program_id(1)

    @pl.when(t_idx == 0)
    def _():
        st_ref[...] = jnp.zeros_like(st_ref)

    lbraw = lb_ref[...]
    e = jnp.exp(lbraw - jnp.max(lbraw, axis=0, keepdims=True))
    soft = e / jnp.sum(e, axis=0, keepdims=True)
    lb = jnp.sum(soft[0:layer + 1, :], axis=0, keepdims=True) - soft[0:1, :]

    fp = f_ref[...]
    logf = jnp.log(lb + (1.0 - lb) * jax.nn.sigmoid(fp))
    kk = (1.0 - lb) * jax.nn.sigmoid(-fp)
    q = q_ref[...]
    v = i_ref[...]
    vb = v.astype(BF16)

    row = lax.broadcasted_iota(jnp.int32, (HG_TILE, HG_DIM), 0)
    b = logf
    sh = 1
    while sh < HG_TILE:
        b = b + jnp.where(row >= sh, pltpu.roll(b, sh, axis=0), 0.0)
        sh *= 2

    ti = lax.broadcasted_iota(jnp.int32, (HG_TILE, HG_TILE), 0)
    si = lax.broadcasted_iota(jnp.int32, (HG_TILE, HG_TILE), 1)
    scores = jnp.where(
        ti == si,
        lax.dot_general(q.astype(BF16), kk.astype(BF16), _NT, preferred_element_type=F32),
        0.0)
    for n in _hg_levels():
        half = n // 2
        if half >= 8:
            pieces = []
            for blk in range(HG_TILE // n):
                m = blk * n + half - 1
                pieces.append(jnp.broadcast_to(b[m:m + 1, :], (n, HG_DIM)))
            bm = pieces[0] if len(pieces) == 1 else jnp.concatenate(pieces, axis=0)
        else:
            b3 = b.reshape(HG_TILE // 8, 8, HG_DIM)
            sub = lax.broadcasted_iota(jnp.int32, (HG_TILE // 8, 8, HG_DIM), 1)
            bm3 = None
            for blk in range(8 // n):
                m = blk * n + half - 1
                piece = jnp.broadcast_to(b3[:, m:m + 1, :], b3.shape)
                bm3 = piece if bm3 is None else jnp.where(sub >= blk * n, piece, bm3)
            bm = bm3.reshape(HG_TILE, HG_DIM)
        upper = (row & (n - 1)) >= half
        ex = jnp.exp(jnp.where(upper, b - bm, bm - b))
        qs = jnp.where(upper, q * ex, 0.0).astype(BF16)
        ks = jnp.where(upper, 0.0, kk * ex).astype(BF16)
        lvl = lax.dot_general(qs, ks, _NT, preferred_element_type=F32)
        if n == HG_TILE:
            scores = scores + lvl
        else:
            scores = jnp.where((ti & -n) == (si & -n), scores + lvl, scores)

    st = st_ref[...]
    b_last = b[HG_TILE - 1:HG_TILE, :]
    o = jnp.dot(scores.astype(BF16), vb, preferred_element_type=F32)
    o = o + lax.dot_general((q * jnp.exp(b)).astype(BF16), st.astype(BF16), _NT,
                            preferred_element_type=F32)
    kdec = (kk * jnp.exp(b_last - b)).astype(BF16)
    st_ref[...] = st * jnp.exp(b_last) + lax.dot_general(vb, kdec, _TN, preferred_element_type=F32)

    gate = g_ref[...]
    o_ref[...] = (_rms(o, nw_ref[...]) * (gate * jax.nn.sigmoid(gate))).astype(o_ref.dtype)


def _hgrn(zhg, hg_lb, hg_nw, layer):
    s = zhg.shape[0]
    depth = hg_lb.shape[0]

    def col(k):
        return pl.BlockSpec((HG_TILE, HG_DIM), lambda h, t: (t, k * HG_HEADS + h))

    return pl.pallas_call(
        functools.partial(_hgrn_kernel, layer),
        grid=(HG_HEADS, s // HG_TILE),
        in_specs=[
            col(0), col(1), col(2), col(3),
            pl.BlockSpec((depth, HG_DIM), lambda h, t: (0, h)),
            pl.BlockSpec((1, HG_DIM), lambda h, t: (0, 0)),
        ],
        out_specs=pl.BlockSpec((HG_TILE, HG_DIM), lambda h, t: (t, h)),
        out_shape=jax.ShapeDtypeStruct((s, HG_WIDTH), BF16),
        scratch_shapes=[pltpu.VMEM((HG_DIM, HG_DIM), F32)],
        compiler_params=_params("parallel", "arbitrary"),
        name="hgrn2",
    )(zhg, zhg, zhg, zhg, hg_lb, hg_nw)


N_BIAS_TABLES = 5


def _bucket_thresholds():
    max_exact = REL_BUCKETS // 2
    d = np.arange(1, 2 * REL_MAX_DIST, dtype=np.float64)
    large = max_exact + (np.log(d / max_exact) / math.log(REL_MAX_DIST / max_exact)
                         * (REL_BUCKETS - max_exact)).astype(np.int64)
    large = np.minimum(large, REL_BUCKETS - 1)
    bucket = np.where(d < max_exact, d.astype(np.int64), large)
    thr = [0] * REL_BUCKETS
    for bkt in range(1, REL_BUCKETS):
        thr[bkt] = int(d[np.argmax(bucket >= bkt)])
    assert (N_BIAS_TABLES - 1) * MOBA_BLOCK + 1 >= thr[REL_BUCKETS - 1]
    return thr


def _bias_kernel(rb_ref, o_ref):
    thr = _bucket_thresholds()
    key = lax.broadcasted_iota(jnp.int32, (MOBA_BLOCK, MOBA_BLOCK), 0)
    qry = lax.broadcasted_iota(jnp.int32, (MOBA_BLOCK, MOBA_BLOCK), 1)
    o_ref[N_BIAS_TABLES:] = jnp.zeros((N_BIAS_TABLES - 1,) + o_ref.shape[1:], F32)
    for u in range(N_BIAS_TABLES):
        t = N_BIAS_TABLES - 1 - u
        dist = qry - key + t * MOBA_BLOCK
        for h in range(AT_HEADS):
            val = jnp.full((MOBA_BLOCK, MOBA_BLOCK), rb_ref[REL_BUCKETS - 1, h], F32)
            for bkt in range(REL_BUCKETS - 2, -1, -1):
                val = jnp.where(dist < thr[bkt + 1], rb_ref[bkt, h], val)
            val = val * LOG2E
            if t == 0:
                val = jnp.where(dist < 0, NEG, val)
            o_ref[u, :, h * MOBA_BLOCK:(h + 1) * MOBA_BLOCK] = val


def _bias_tables(rel_bias):
    return pl.pallas_call(
        _bias_kernel,
        in_specs=[pl.BlockSpec(memory_space=pltpu.SMEM)],
        out_shape=jax.ShapeDtypeStruct((2 * N_BIAS_TABLES - 1, MOBA_BLOCK, AT_HEADS * MOBA_BLOCK), F32),
        name="moba_bias",
    )(rel_bias)


def _moba_kernel(rb_ref, q_ref, k_ref, vt_ref, ksum_ref, bias_ref, o_ref,
                 kmean_ref, mnear_ref, mfar_ref, qs_ref, m_ref, acc_ref, sa_ref, sb_ref):
    i = pl.program_id(0)
    nblk = k_ref.shape[0]

    @pl.when(i == 0)
    def _():
        ks = ksum_ref[...].reshape(nblk, KSUM_ROWS, AT_WIDTH)
        kmean_ref[...] = jnp.sum(ks, axis=1) * (1.0 / MOBA_BLOCK)

    qt = q_ref[...].T
    row_head = lax.broadcasted_iota(jnp.int32, (AT_WIDTH, MOBA_BLOCK), 0) // AT_DIM
    qmt = jnp.concatenate([jnp.where(row_head == h, qt, 0.0) for h in range(AT_HEADS)], axis=1)
    qs_ref[...] = (qmt * (AT_DIM ** -0.5 * LOG2E)).astype(BF16)

    ncol = AT_HEADS * MOBA_BLOCK
    jio = lax.broadcasted_iota(jnp.int32, (nblk, ncol), 0).astype(F32)
    fi = i.astype(F32)
    n_far = jnp.maximum(i - (N_BIAS_TABLES - 1), 0)
    gate = jnp.dot(kmean_ref[...], qmt, precision=lax.Precision.HIGHEST,
                   preferred_element_type=F32)
    gate = jnp.where(jio < fi, gate, -jnp.inf)
    sel = jio == fi
    for _ in range(MOBA_TOPK):
        mx = jnp.max(gate, axis=0, keepdims=True)
        cand = jnp.where(gate == mx, jio, float(nblk))
        idx = jnp.min(cand, axis=0, keepdims=True)
        pick = (jio == idx) & (mx > -jnp.inf)
        sel = sel | pick
        gate = jnp.where(pick, -jnp.inf, gate)
    col_head = lax.broadcasted_iota(jnp.int32, (1, ncol), 1) // MOBA_BLOCK
    far_bias = jnp.zeros((1, ncol), F32)
    for h in range(AT_HEADS):
        far_bias = jnp.where(col_head == h, rb_ref[REL_BUCKETS - 1, h] * LOG2E, far_bias)
    mnear_ref[...] = jnp.where(sel, 0.0, NEG)
    mfar_ref[...] = jnp.where(sel & (jio < n_far.astype(F32)), far_bias, NEG)
    m_ref[...] = jnp.full(m_ref.shape, NEG, F32)
    acc_ref[...] = jnp.zeros(acc_ref.shape, F32)

    def logits_into(s_ref, j0, nb):
        kg = k_ref[pl.ds(j0, nb)].reshape(nb * MOBA_BLOCK, AT_WIDTH)
        s_ref[0:nb * MOBA_BLOCK, :] = jnp.dot(kg, qs_ref[...], preferred_element_type=F32)

    def fold(s_ref, j0, nb, mask_ref, table):
        for h in range(AT_HEADS):
            cols = slice(h * MOBA_BLOCK, (h + 1) * MOBA_BLOCK)
            m = m_ref[:, cols]
            acc = acc_ref[:, cols]
            for g in range(nb):
                rows = slice(g * MOBA_BLOCK, (g + 1) * MOBA_BLOCK)
                mrow = mask_ref[pl.ds(j0 + g, 1), cols]
                s = s_ref[rows, cols]
                if table is not None:
                    s = s + mrow + bias_ref[table + g, :, cols]
                mb = jnp.max(s, axis=0, keepdims=True)
                p = jnp.exp2(s - mb).astype(BF16)
                if table is None:
                    mb = mb + mrow
                pv = jnp.dot(vt_ref[j0 + g, h], p, preferred_element_type=F32)
                m_new = jnp.maximum(m, mb)
                acc = jnp.exp2(m - m_new) * acc + jnp.exp2(mb - m_new) * pv
                m = m_new
            m_ref[:, cols] = m
            acc_ref[:, cols] = acc

    near0 = jnp.maximum(i - (N_BIAS_TABLES - 1), 0)
    n_groups = (n_far + FAR_GROUP - 1) // FAR_GROUP
    last_group = nblk // FAR_GROUP - 1
    logits_into(sa_ref, near0, N_BIAS_TABLES)
    logits_into(sb_ref, 0, FAR_GROUP)
    fold(sa_ref, near0, N_BIAS_TABLES, mnear_ref, near0 - (i - (N_BIAS_TABLES - 1)))

    def far_pair(gp, carry):
        ga = 2 * gp
        logits_into(sa_ref, FAR_GROUP * (ga + 1), FAR_GROUP)
        fold(sb_ref, FAR_GROUP * ga, FAR_GROUP, mfar_ref, None)
        logits_into(sb_ref, FAR_GROUP * jnp.minimum(ga + 2, last_group), FAR_GROUP)
        fold(sa_ref, FAR_GROUP * (ga + 1), FAR_GROUP, mfar_ref, None)
        return carry

    lax.fori_loop(0, (n_groups + 1) // 2, far_pair, 0)
    acc = acc_ref[...]
    ot = acc[0:AT_DIM, :] / acc[AT_DIM:AT_DIM + 1, :]
    o_ref[...] = jnp.concatenate(
        [ot[:, h * MOBA_BLOCK:(h + 1) * MOBA_BLOCK] for h in range(AT_HEADS)], axis=0
    ).T.astype(o_ref.dtype)


def _moba(rel_bias, q, k, vt, ksum, bias):
    s = q.shape[0]
    nblk = s // MOBA_BLOCK
    assert nblk >= N_BIAS_TABLES and FAR_GROUP <= N_BIAS_TABLES and nblk % FAR_GROUP == 0
    return pl.pallas_call(
        _moba_kernel,
        grid=(nblk,),
        in_specs=[
            pl.BlockSpec(memory_space=pltpu.SMEM),
            pl.BlockSpec((MOBA_BLOCK, AT_WIDTH), lambda i: (i, 0)),
            _const_spec((nblk, MOBA_BLOCK, AT_WIDTH)),
            _const_spec((nblk, AT_HEADS, V_ROWS, MOBA_BLOCK)),
            _const_spec((nblk * KSUM_ROWS, AT_WIDTH)),
            _const_spec((2 * N_BIAS_TABLES - 1, MOBA_BLOCK, AT_HEADS * MOBA_BLOCK)),
        ],
        out_specs=pl.BlockSpec((MOBA_BLOCK, AT_WIDTH), lambda i: (i, 0)),
        out_shape=jax.ShapeDtypeStruct((s, AT_WIDTH), BF16),
        scratch_shapes=[
            pltpu.VMEM((nblk, AT_WIDTH), F32),
            pltpu.VMEM((nblk, AT_HEADS * MOBA_BLOCK), F32),
            pltpu.VMEM((nblk, AT_HEADS * MOBA_BLOCK), F32),
            pltpu.VMEM((AT_WIDTH, AT_HEADS * MOBA_BLOCK), BF16),
            pltpu.VMEM((1, AT_HEADS * MOBA_BLOCK), F32),
            pltpu.VMEM((V_ROWS, AT_HEADS * MOBA_BLOCK), F32),
            pltpu.VMEM((N_BIAS_TABLES * MOBA_BLOCK, AT_HEADS * MOBA_BLOCK), F32),
            pltpu.VMEM((FAR_GROUP * MOBA_BLOCK, AT_HEADS * MOBA_BLOCK), F32),
        ],
        compiler_params=_params("arbitrary"),
        name="moba",
    )(rel_bias, q, k, vt, ksum, bias)


def _outproj_kernel(x_ref, ohg_ref, oat_ref, zcv_ref, halo_ref, cw_ref, w_ref, nw_ref, o_ref):
    i = pl.program_id(0)
    zcv = zcv_ref[...]
    bgate = zcv[:, 0:CV_WIDTH]
    u = zcv[:, CV_WIDTH:2 * CV_WIDTH] * zcv[:, 2 * CV_WIDTH:3 * CV_WIDTH]
    halo = halo_ref[...]
    uh = halo[:, CV_WIDTH:2 * CV_WIDTH] * halo[:, 2 * CV_WIDTH:3 * CV_WIDTH]
    uh = jnp.where(i > 0, uh, 0.0)
    row = lax.broadcasted_iota(jnp.int32, u.shape, 0)
    u1 = jnp.where(row == 0, uh[7:8, :], pltpu.roll(u, 1, axis=0))
    u2 = jnp.where(row == 0, uh[6:7, :], jnp.where(row == 1, uh[7:8, :], pltpu.roll(u, 2, axis=0)))
    cw = cw_ref[...]
    ocv = bgate * (cw[0:1, :] * u2 + cw[1:2, :] * u1 + cw[2:3, :] * u)
    h = jnp.dot(ohg_ref[...], w_ref[0:HG_WIDTH, :], preferred_element_type=F32)
    h = h + jnp.dot(oat_ref[...], w_ref[HG_WIDTH:HG_WIDTH + AT_WIDTH, :], preferred_element_type=F32)
    h = h + jnp.dot(ocv.astype(BF16), w_ref[HG_WIDTH + AT_WIDTH:D_MIX, :], preferred_element_type=F32)
    o_ref[...] = x_ref[...] + _rms(h, nw_ref[...])


def _outproj(x, ohg, oat, zcv, conv_w, w_out, nw):
    s = x.shape[0]
    halo_blocks = ROW_TILE // 8
    return pl.pallas_call(
        _outproj_kernel,
        grid=(s // ROW_TILE,),
        in_specs=[
            pl.BlockSpec((ROW_TILE, D_MODEL), lambda i: (i, 0)),
            pl.BlockSpec((ROW_TILE, HG_WIDTH), lambda i: (i, 0)),
            pl.BlockSpec((ROW_TILE, AT_WIDTH), lambda i: (i, 0)),
            pl.BlockSpec((ROW_TILE, 3 * CV_WIDTH), lambda i: (i, 0)),
            pl.BlockSpec((8, 3 * CV_WIDTH), lambda i: (jnp.maximum(i * halo_blocks - 1, 0), 0)),
            _const_spec((CV_KERNEL, CV_WIDTH)),
            _const_spec((D_MIX, D_MODEL)),
            _const_spec((1, D_MODEL)),
        ],
        out_specs=pl.BlockSpec((ROW_TILE, D_MODEL), lambda i: (i, 0)),
        out_shape=jax.ShapeDtypeStruct((s, D_MODEL), F32),
        compiler_params=_params("parallel"),
        name="outproj",
    )(x, ohg, oat, zcv, zcv, conv_w, w_out, nw)


def kernel(x, norm_w, ffn1_wg, ffn1_wu, ffn1_wd, mix_w_in, mix_w_out, hg_lb, hg_norm_w, conv_w,
           ffn2_wg, ffn2_wu, ffn2_wd, rel_bias):
    batch, seq, _ = x.shape
    depth = norm_w.shape[0]
    assert batch == 1 and seq % ROW_TILE == 0 and seq % MOBA_BLOCK == 0
    rel_bias = rel_bias.astype(F32)
    bias = _bias_tables(rel_bias)
    y = x.reshape(seq, D_MODEL)
    for l in range(depth):
        y = _ffn(y, norm_w[l, 0:2], ffn1_wg[l].astype(BF16), ffn1_wu[l].astype(BF16),
                 ffn1_wd[l].astype(BF16))
        zhg, q, k, vt, ksum, zcv = _inproj(y, norm_w[l, 2:3], mix_w_in[l].astype(BF16))
        ohg = _hgrn(zhg, hg_lb, hg_norm_w[l:l + 1], l)
        oat = _moba(rel_bias, q, k, vt, ksum, bias)
        y = _outproj(y, ohg, oat, zcv, conv_w[l], mix_w_out[l].astype(BF16), norm_w[l, 3:4])
        y = _ffn(y, norm_w[l, 4:6], ffn2_wg[l].astype(BF16), ffn2_wu[l].astype(BF16),
                 ffn2_wd[l].astype(BF16))
    return y.reshape(batch, seq, D_MODEL)
```

```python
import functools
import math

import numpy as np
import jax
import jax.numpy as jnp
from jax import lax
from jax.experimental import pallas as pl
from jax.experimental.pallas import tpu as pltpu

F32 = jnp.float32
BF16 = jnp.bfloat16

D_MODEL = 1024
D_FF = 2816
HG_HEADS = 4
HG_DIM = 128
HG_WIDTH = HG_HEADS * HG_DIM
AT_HEADS = 4
AT_DIM = 64
AT_WIDTH = AT_HEADS * AT_DIM
MOBA_BLOCK = 256
MOBA_TOPK = 3
REL_BUCKETS = 32
REL_MAX_DIST = 1024
CV_WIDTH = 256
CV_KERNEL = 3
D_MIX = HG_WIDTH + AT_WIDTH + CV_WIDTH
D_IN = 4 * HG_WIDTH + 3 * AT_WIDTH + 3 * CV_WIDTH
EPS = 1e-6

ROW_TILE = 512
HG_TILE = 256
KSUM_ROWS = 8
V_ROWS = AT_DIM + 16
LOG2E = math.log2(math.e)
FAR_GROUP = 4
NEG = -1e30
VMEM_LIMIT = 56 * 1024 * 1024
FF_CHUNKS = ((0, 768), (768, 1536), (1536, 2304), (2304, 2816))

_NT = (((1,), (1,)), ((), ()))
_TN = (((0,), (0,)), ((), ()))


def _rms(x, w):
    ms = jnp.mean(x * x, axis=-1, keepdims=True)
    return x * lax.rsqrt(ms + EPS) * w


def _const_spec(shape):
    nd = len(shape)
    return pl.BlockSpec(shape, lambda *_: (0,) * nd, pipeline_mode=pl.Buffered(1))


def _params(*sem, flags=None):
    return pltpu.CompilerParams(dimension_semantics=sem, vmem_limit_bytes=VMEM_LIMIT, flags=flags)


def _ffn_kernel(x_ref, nw_ref, wg_ref, wu_ref, wd_ref, o_ref):
    x = x_ref[...]
    xn = _rms(x, nw_ref[0:1, :]).astype(BF16)
    h = None
    for c0, c1 in FF_CHUNKS:
        g = jnp.dot(xn, wg_ref[:, c0:c1], preferred_element_type=F32)
        u = jnp.dot(xn, wu_ref[:, c0:c1], preferred_element_type=F32)
        a = (g * jax.nn.sigmoid(g) * u).astype(BF16)
        part = jnp.dot(a, wd_ref[c0:c1, :], preferred_element_type=F32)
        h = part if h is None else h + part
    o_ref[...] = x + 0.5 * _rms(h, nw_ref[1:2, :])


def _ffn(x, nw2, wg, wu, wd):
    s = x.shape[0]
    return pl.pallas_call(
        _ffn_kernel,
        grid=(s // ROW_TILE,),
        in_specs=[
            pl.BlockSpec((ROW_TILE, D_MODEL), lambda i: (i, 0)),
            _const_spec((2, D_MODEL)),
            _const_spec((D_MODEL, D_FF)),
            _const_spec((D_MODEL, D_FF)),
            _const_spec((D_FF, D_MODEL)),
        ],
        out_specs=pl.BlockSpec((ROW_TILE, D_MODEL), lambda i: (i, 0)),
        out_shape=jax.ShapeDtypeStruct((s, D_MODEL), F32),
        compiler_params=_params("parallel"),
        name="ffn",
    )(x, nw2, wg, wu, wd)


def _inproj_kernel(x_ref, nw_ref, w_ref, zhg_ref, q_ref, k_ref, vt_ref, ksum_ref, zcv_ref):
    xn = _rms(x_ref[...], nw_ref[...]).astype(BF16)
    c = 4 * HG_WIDTH
    zhg_ref[...] = jnp.dot(xn, w_ref[:, 0:c], preferred_element_type=F32)
    q_ref[...] = jnp.dot(xn, w_ref[:, c:c + AT_WIDTH], preferred_element_type=F32)
    k = jnp.dot(xn, w_ref[:, c + AT_WIDTH:c + 2 * AT_WIDTH], preferred_element_type=F32)
    v = jnp.dot(xn, w_ref[:, c + 2 * AT_WIDTH:c + 3 * AT_WIDTH], preferred_element_type=F32)
    c += 3 * AT_WIDTH
    zcv_ref[...] = jnp.dot(xn, w_ref[:, c:c + 3 * CV_WIDTH], preferred_element_type=F32)
    ones = jnp.ones((V_ROWS - AT_DIM, MOBA_BLOCK), F32)
    for b in range(ROW_TILE // MOBA_BLOCK):
        kb = k[b * MOBA_BLOCK:(b + 1) * MOBA_BLOCK, :]
        vbt = v[b * MOBA_BLOCK:(b + 1) * MOBA_BLOCK, :].T
        k_ref[b] = kb.astype(BF16)
        for h in range(AT_HEADS):
            vt_ref[b, h] = jnp.concatenate(
                [vbt[h * AT_DIM:(h + 1) * AT_DIM, :], ones], axis=0).astype(BF16)
        ksum_ref[b * KSUM_ROWS:(b + 1) * KSUM_ROWS, :] = jnp.sum(
            kb.reshape(MOBA_BLOCK // KSUM_ROWS, KSUM_ROWS, AT_WIDTH), axis=0)


def _inproj(x, nw, w_in):
    s = x.shape[0]
    nblk = s // MOBA_BLOCK
    bpt = ROW_TILE // MOBA_BLOCK
    return pl.pallas_call(
        _inproj_kernel,
        grid=(s // ROW_TILE,),
        in_specs=[
            pl.BlockSpec((ROW_TILE, D_MODEL), lambda i: (i, 0)),
            _const_spec((1, D_MODEL)),
            _const_spec((D_MODEL, D_IN)),
        ],
        out_specs=[
            pl.BlockSpec((ROW_TILE, 4 * HG_WIDTH), lambda i: (i, 0)),
            pl.BlockSpec((ROW_TILE, AT_WIDTH), lambda i: (i, 0)),
            pl.BlockSpec((bpt, MOBA_BLOCK, AT_WIDTH), lambda i: (i, 0, 0)),
            pl.BlockSpec((bpt, AT_HEADS, V_ROWS, MOBA_BLOCK), lambda i: (i, 0, 0, 0)),
            pl.BlockSpec((bpt * KSUM_ROWS, AT_WIDTH), lambda i: (i, 0)),
            pl.BlockSpec((ROW_TILE, 3 * CV_WIDTH), lambda i: (i, 0)),
        ],
        out_shape=[
            jax.ShapeDtypeStruct((s, 4 * HG_WIDTH), F32),
            jax.ShapeDtypeStruct((s, AT_WIDTH), F32),
            jax.ShapeDtypeStruct((nblk, MOBA_BLOCK, AT_WIDTH), BF16),
            jax.ShapeDtypeStruct((nblk, AT_HEADS, V_ROWS, MOBA_BLOCK), BF16),
            jax.ShapeDtypeStruct((nblk * KSUM_ROWS, AT_WIDTH), F32),
            jax.ShapeDtypeStruct((s, 3 * CV_WIDTH), F32),
        ],
        compiler_params=_params("parallel"),
        name="inproj",
    )(x, nw, w_in)


def _hg_levels():
    levels = []
    n = HG_TILE
    while n >= 2:
        levels.append(n)
        n //= 2
    return levels


def _hgrn_head(q, fp, v, gate, lb, nw, st_ref, row, masks):
    u = jnp.exp2(jnp.minimum(fp * -LOG2E, 126.0))
    r = 1.0 / (1.0 + u)
    logf = jnp.log2(lb + (1.0 - lb) * r)
    kk = (1.0 - lb) * (u * r)
    vb = v.astype(BF16)

    b = logf
    sh = 1
    while sh < 8:
        b = b + jnp.where(row >= sh, pltpu.roll(b, sh, axis=0), 0.0)
        sh *= 2
    while sh < HG_TILE:
        b = jnp.concatenate([b[:sh], b[sh:] + b[:-sh]], axis=0)
        sh *= 2

    diag, same = masks
    scores = jnp.where(
        diag, lax.dot_general(q.astype(BF16), kk.astype(BF16), _NT, preferred_element_type=F32), 0.0)
    for n in _hg_levels():
        half = n // 2
        if half >= 8:
            qp, kp = [], []
            zero = jnp.zeros((half, HG_DIM), F32)
            for lo in range(0, HG_TILE, n):
                mid, hi = lo + half, lo + n
                bm = b[mid - 1:mid, :]
                qp += [zero, q[mid:hi] * jnp.exp2(b[mid:hi] - bm)]
                kp += [kk[lo:mid] * jnp.exp2(bm - b[lo:mid]), zero]
            qs = jnp.concatenate(qp, axis=0)
            ks = jnp.concatenate(kp, axis=0)
        else:
            upper = (row & (n - 1)) >= half
            if n == 2:
                ex = jnp.exp2(jnp.where(upper, logf, 0.0))
            else:
                b3 = b.reshape(HG_TILE // 8, 8, HG_DIM)
                sub = lax.broadcasted_iota(jnp.int32, (HG_TILE // 8, 8, HG_DIM), 1)
                bm3 = None
                for lo in range(0, 8, n):
                    piece = jnp.broadcast_to(b3[:, lo + half - 1:lo + half, :], b3.shape)
                    bm3 = piece if bm3 is None else jnp.where(sub >= lo, piece, bm3)
                ex = jnp.exp2(-jnp.abs(b - bm3.reshape(HG_TILE, HG_DIM)))
            qs = jnp.where(upper, q * ex, 0.0)
            ks = jnp.where(upper, 0.0, kk * ex)
        lvl = lax.dot_general(qs.astype(BF16), ks.astype(BF16), _NT, preferred_element_type=F32)
        scores = scores + (lvl if n == HG_TILE else jnp.where(same[n], lvl, 0.0))

    st = st_ref[...]
    b_last = b[HG_TILE - 1:HG_TILE, :]
    o = jnp.dot(scores.astype(BF16), vb, preferred_element_type=F32)
    o = o + lax.dot_general((q * jnp.exp2(b)).astype(BF16), st.astype(BF16), _NT,
                            preferred_element_type=F32)
    kdec = (kk * jnp.exp2(b_last - b)).astype(BF16)
    st_ref[...] = st * jnp.exp2(b_last) + lax.dot_general(vb, kdec, _TN, preferred_element_type=F32)
    return _rms(o, nw) * (gate * jax.nn.sigmoid(gate))


def _hgrn_kernel(layer, q_ref, f_ref, i_ref, g_ref, lb_ref, nw_ref, o_ref, st_ref):
    @pl.when(pl.program_id(0) == 0)
    def _():
        st_ref[...] = jnp.zeros_like(st_ref)

    lbraw = lb_ref[...]
    e = jnp.exp(lbraw - jnp.max(lbraw, axis=0, keepdims=True))
    soft = e / jnp.sum(e, axis=0, keepdims=True)
    lb = jnp.sum(soft[0:layer + 1, :], axis=0, keepdims=True) - soft[0:1, :]

    row = lax.broadcasted_iota(jnp.int32, (HG_TILE, HG_DIM), 0)
    ti = lax.broadcasted_iota(jnp.int32, (HG_TILE, HG_TILE), 0)
    si = lax.broadcasted_iota(jnp.int32, (HG_TILE, HG_TILE), 1)
    masks = (ti == si, {n: (ti & -n) == (si & -n) for n in _hg_levels() if n < HG_TILE})
    for h in range(HG_HEADS):
        cols = slice(h * HG_DIM, (h + 1) * HG_DIM)
        o_ref[:, cols] = _hgrn_head(q_ref[:, cols], f_ref[:, cols], i_ref[:, cols], g_ref[:, cols],
                                    lb[:, cols], nw_ref[...], st_ref.at[h], row, masks
                                    ).astype(o_ref.dtype)


def _hgrn(zhg, hg_lb, hg_nw, layer):
    s = zhg.shape[0]
    depth = hg_lb.shape[0]

    def col(k):
        return pl.BlockSpec((HG_TILE, HG_WIDTH), lambda t: (t, k))

    return pl.pallas_call(
        functools.partial(_hgrn_kernel, layer),
        grid=(s // HG_TILE,),
        in_specs=[
            col(0), col(1), col(2), col(3),
            pl.BlockSpec((depth, HG_WIDTH), lambda t: (0, 0)),
            pl.BlockSpec((1, HG_DIM), lambda t: (0, 0)),
        ],
        out_specs=pl.BlockSpec((HG_TILE, HG_WIDTH), lambda t: (t, 0)),
        out_shape=jax.ShapeDtypeStruct((s, HG_WIDTH), BF16),
        scratch_shapes=[pltpu.VMEM((HG_HEADS, HG_DIM, HG_DIM), F32)],
        compiler_params=_params("arbitrary"),
        name="hgrn2",
    )(zhg, zhg, zhg, zhg, hg_lb, hg_nw)


N_BIAS_TABLES = 5


def _bucket_thresholds():
    max_exact = REL_BUCKETS // 2
    d = np.arange(1, 2 * REL_MAX_DIST, dtype=np.float64)
    large = max_exact + (np.log(d / max_exact) / math.log(REL_MAX_DIST / max_exact)
                         * (REL_BUCKETS - max_exact)).astype(np.int64)
    large = np.minimum(large, REL_BUCKETS - 1)
    bucket = np.where(d < max_exact, d.astype(np.int64), large)
    thr = [0] * REL_BUCKETS
    for bkt in range(1, REL_BUCKETS):
        thr[bkt] = int(d[np.argmax(bucket >= bkt)])
    assert (N_BIAS_TABLES - 1) * MOBA_BLOCK + 1 >= thr[REL_BUCKETS - 1]
    return thr


def _bias_kernel(rb_ref, o_ref):
    thr = _bucket_thresholds()
    key = lax.broadcasted_iota(jnp.int32, (MOBA_BLOCK, MOBA_BLOCK), 0)
    qry = lax.broadcasted_iota(jnp.int32, (MOBA_BLOCK, MOBA_BLOCK), 1)
    o_ref[N_BIAS_TABLES:] = jnp.zeros((N_BIAS_TABLES - 1,) + o_ref.shape[1:], F32)
    for u in range(N_BIAS_TABLES):
        t = N_BIAS_TABLES - 1 - u
        dist = qry - key + t * MOBA_BLOCK
        for h in range(AT_HEADS):
            val = jnp.full((MOBA_BLOCK, MOBA_BLOCK), rb_ref[REL_BUCKETS - 1, h], F32)
            for bkt in range(REL_BUCKETS - 2, -1, -1):
                val = jnp.where(dist < thr[bkt + 1], rb_ref[bkt, h], val)
            val = val * LOG2E
            if t == 0:
                val = jnp.where(dist < 0, NEG, val)
            o_ref[u, :, h * MOBA_BLOCK:(h + 1) * MOBA_BLOCK] = val


def _bias_tables(rel_bias):
    return pl.pallas_call(
        _bias_kernel,
        in_specs=[pl.BlockSpec(memory_space=pltpu.SMEM)],
        out_shape=jax.ShapeDtypeStruct((2 * N_BIAS_TABLES - 1, MOBA_BLOCK, AT_HEADS * MOBA_BLOCK), F32),
        name="moba_bias",
    )(rel_bias)


def _moba_kernel(rb_ref, q_ref, k_ref, vt_ref, ksum_ref, bias_ref, o_ref,
                 kmean_ref, mnear_ref, mfar_ref, qs_ref, m_ref, acc_ref, sa_ref, sb_ref):
    i = pl.program_id(0)
    nblk = k_ref.shape[0]

    @pl.when(i == 0)
    def _():
        ks = ksum_ref[...].reshape(nblk, KSUM_ROWS, AT_WIDTH)
        kmean_ref[...] = jnp.sum(ks, axis=1) * (1.0 / MOBA_BLOCK)

    qt = q_ref[...].T
    row_head = lax.broadcasted_iota(jnp.int32, (AT_WIDTH, MOBA_BLOCK), 0) // AT_DIM
    qmt = jnp.concatenate([jnp.where(row_head == h, qt, 0.0) for h in range(AT_HEADS)], axis=1)
    qs_ref[...] = (qmt * (AT_DIM ** -0.5 * LOG2E)).astype(BF16)

    ncol = AT_HEADS * MOBA_BLOCK
    jio = lax.broadcasted_iota(jnp.int32, (nblk, ncol), 0).astype(F32)
    fi = i.astype(F32)
    n_far = jnp.maximum(i - (N_BIAS_TABLES - 1), 0)
    gate = jnp.dot(kmean_ref[...], qmt, precision=lax.Precision.HIGHEST,
                   preferred_element_type=F32)
    gate = jnp.where(jio < fi, gate, -jnp.inf)
    sel = jio == fi
    for _ in range(MOBA_TOPK):
        mx = jnp.max(gate, axis=0, keepdims=True)
        cand = jnp.where(gate == mx, jio, float(nblk))
        idx = jnp.min(cand, axis=0, keepdims=True)
        pick = (jio == idx) & (mx > -jnp.inf)
        sel = sel | pick
        gate = jnp.where(pick, -jnp.inf, gate)
    col_head = lax.broadcasted_iota(jnp.int32, (1, ncol), 1) // MOBA_BLOCK
    far_bias = jnp.zeros((1, ncol), F32)
    for h in range(AT_HEADS):
        far_bias = jnp.where(col_head == h, rb_ref[REL_BUCKETS - 1, h] * LOG2E, far_bias)
    mnear_ref[...] = jnp.where(sel, 0.0, NEG)
    mfar_ref[...] = jnp.where(sel & (jio < n_far.astype(F32)), far_bias, NEG)
    m_ref[...] = jnp.full(m_ref.shape, NEG, F32)
    acc_ref[...] = jnp.zeros(acc_ref.shape, F32)

    def logits_into(s_ref, j0, nb):
        kg = k_ref[pl.ds(j0, nb)].reshape(nb * MOBA_BLOCK, AT_WIDTH)
        s_ref[0:nb * MOBA_BLOCK, :] = jnp.dot(kg, qs_ref[...], preferred_element_type=F32)

    def fold(s_ref, j0, nb, mask_ref, table):
        for h in range(AT_HEADS):
            cols = slice(h * MOBA_BLOCK, (h + 1) * MOBA_BLOCK)
            m = m_ref[:, cols]
            acc = acc_ref[:, cols]
            for g in range(nb):
                rows = slice(g * MOBA_BLOCK, (g + 1) * MOBA_BLOCK)
                mrow = mask_ref[pl.ds(j0 + g, 1), cols]
                s = s_ref[rows, cols]
                if table is not None:
                    s = s + mrow + bias_ref[table + g, :, cols]
                mb = jnp.max(s, axis=0, keepdims=True)
                p = jnp.exp2(s - mb).astype(BF16)
                if table is None:
                    mb = mb + mrow
                pv = jnp.dot(vt_ref[j0 + g, h], p, preferred_element_type=F32)
                m_new = jnp.maximum(m, mb)
                acc = jnp.exp2(m - m_new) * acc + jnp.exp2(mb - m_new) * pv
                m = m_new
            m_ref[:, cols] = m
            acc_ref[:, cols] = acc

    near0 = jnp.maximum(i - (N_BIAS_TABLES - 1), 0)
    n_groups = (n_far + FAR_GROUP - 1) // FAR_GROUP
    last_group = nblk // FAR_GROUP - 1
    logits_into(sa_ref, near0, N_BIAS_TABLES)
    logits_into(sb_ref, 0, FAR_GROUP)
    fold(sa_ref, near0, N_BIAS_TABLES, mnear_ref, near0 - (i - (N_BIAS_TABLES - 1)))

    def far_pair(gp, carry):
        ga = 2 * gp
        logits_into(sa_ref, FAR_GROUP * (ga + 1), FAR_GROUP)
        fold(sb_ref, FAR_GROUP * ga, FAR_GROUP, mfar_ref, None)
        logits_into(sb_ref, FAR_GROUP * jnp.minimum(ga + 2, last_group), FAR_GROUP)
        fold(sa_ref, FAR_GROUP * (ga + 1), FAR_GROUP, mfar_ref, None)
        return carry

    lax.fori_loop(0, (n_groups + 1) // 2, far_pair, 0)
    acc = acc_ref[...]
    ot = acc[0:AT_DIM, :] / acc[AT_DIM:AT_DIM + 1, :]
    o_ref[...] = jnp.concatenate(
        [ot[:, h * MOBA_BLOCK:(h + 1) * MOBA_BLOCK] for h in range(AT_HEADS)], axis=0
    ).T.astype(o_ref.dtype)


def _moba(rel_bias, q, k, vt, ksum, bias):
    s = q.shape[0]
    nblk = s // MOBA_BLOCK
    assert nblk >= N_BIAS_TABLES and FAR_GROUP <= N_BIAS_TABLES and nblk % FAR_GROUP == 0
    return pl.pallas_call(
        _moba_kernel,
        grid=(nblk,),
        in_specs=[
            pl.BlockSpec(memory_space=pltpu.SMEM),
            pl.BlockSpec((MOBA_BLOCK, AT_WIDTH), lambda i: (i, 0)),
            _const_spec((nblk, MOBA_BLOCK, AT_WIDTH)),
            _const_spec((nblk, AT_HEADS, V_ROWS, MOBA_BLOCK)),
            _const_spec((nblk * KSUM_ROWS, AT_WIDTH)),
            _const_spec((2 * N_BIAS_TABLES - 1, MOBA_BLOCK, AT_HEADS * MOBA_BLOCK)),
        ],
        out_specs=pl.BlockSpec((MOBA_BLOCK, AT_WIDTH), lambda i: (i, 0)),
        out_shape=jax.ShapeDtypeStruct((s, AT_WIDTH), BF16),
        scratch_shapes=[
            pltpu.VMEM((nblk, AT_WIDTH), F32),
            pltpu.VMEM((nblk, AT_HEADS * MOBA_BLOCK), F32),
            pltpu.VMEM((nblk, AT_HEADS * MOBA_BLOCK), F32),
            pltpu.VMEM((AT_WIDTH, AT_HEADS * MOBA_BLOCK), BF16),
            pltpu.VMEM((1, AT_HEADS * MOBA_BLOCK), F32),
            pltpu.VMEM((V_ROWS, AT_HEADS * MOBA_BLOCK), F32),
            pltpu.VMEM((N_BIAS_TABLES * MOBA_BLOCK, AT_HEADS * MOBA_BLOCK), F32),
            pltpu.VMEM((FAR_GROUP * MOBA_BLOCK, AT_HEADS * MOBA_BLOCK), F32),
        ],
        compiler_params=_params("arbitrary"),
        name="moba",
    )(rel_bias, q, k, vt, ksum, bias)


def _outproj_kernel(x_ref, ohg_ref, oat_ref, zcv_ref, halo_ref, cw_ref, w_ref, nw_ref, o_ref):
    i = pl.program_id(0)
    zcv = zcv_ref[...]
    bgate = zcv[:, 0:CV_WIDTH]
    u = zcv[:, CV_WIDTH:2 * CV_WIDTH] * zcv[:, 2 * CV_WIDTH:3 * CV_WIDTH]
    halo = halo_ref[...]
    uh = halo[:, CV_WIDTH:2 * CV_WIDTH] * halo[:, 2 * CV_WIDTH:3 * CV_WIDTH]
    uh = jnp.where(i > 0, uh, 0.0)
    row = lax.broadcasted_iota(jnp.int32, u.shape, 0)
    u1 = jnp.where(row == 0, uh[7:8, :], pltpu.roll(u, 1, axis=0))
    u2 = jnp.where(row == 0, uh[6:7, :], jnp.where(row == 1, uh[7:8, :], pltpu.roll(u, 2, axis=0)))
    cw = cw_ref[...]
    ocv = bgate * (cw[0:1, :] * u2 + cw[1:2, :] * u1 + cw[2:3, :] * u)
    h = jnp.dot(ohg_ref[...], w_ref[0:HG_WIDTH, :], preferred_element_type=F32)
    h = h + jnp.dot(oat_ref[...], w_ref[HG_WIDTH:HG_WIDTH + AT_WIDTH, :], preferred_element_type=F32)
    h = h + jnp.dot(ocv.astype(BF16), w_ref[HG_WIDTH + AT_WIDTH:D_MIX, :], preferred_element_type=F32)
    o_ref[...] = x_ref[...] + _rms(h, nw_ref[...])


def _outproj(x, ohg, oat, zcv, conv_w, w_out, nw):
    s = x.shape[0]
    halo_blocks = ROW_TILE // 8
    return pl.pallas_call(
        _outproj_kernel,
        grid=(s // ROW_TILE,),
        in_specs=[
            pl.BlockSpec((ROW_TILE, D_MODEL), lambda i: (i, 0)),
            pl.BlockSpec((ROW_TILE, HG_WIDTH), lambda i: (i, 0)),
            pl.BlockSpec((ROW_TILE, AT_WIDTH), lambda i: (i, 0)),
            pl.BlockSpec((ROW_TILE, 3 * CV_WIDTH), lambda i: (i, 0)),
            pl.BlockSpec((8, 3 * CV_WIDTH), lambda i: (jnp.maximum(i * halo_blocks - 1, 0), 0)),
            _const_spec((CV_KERNEL, CV_WIDTH)),
            _const_spec((D_MIX, D_MODEL)),
            _const_spec((1, D_MODEL)),
        ],
        out_specs=pl.BlockSpec((ROW_TILE, D_MODEL), lambda i: (i, 0)),
        out_shape=jax.ShapeDtypeStruct((s, D_MODEL), F32),
        compiler_params=_params("parallel"),
        name="outproj",
    )(x, ohg, oat, zcv, zcv, conv_w, w_out, nw)


def kernel(x, norm_w, ffn1_wg, ffn1_wu, ffn1_wd, mix_w_in, mix_w_out, hg_lb, hg_norm_w, conv_w,
           ffn2_wg, ffn2_wu, ffn2_wd, rel_bias):
    batch, seq, _ = x.shape
    depth = norm_w.shape[0]
    assert batch == 1 and seq % ROW_TILE == 0 and seq % MOBA_BLOCK == 0
    rel_bias = rel_bias.astype(F32)
    bias = _bias_tables(rel_bias)
    y = x.reshape(seq, D_MODEL)
    for l in range(depth):
        y = _ffn(y, norm_w[l, 0:2], ffn1_wg[l].astype(BF16), ffn1_wu[l].astype(BF16),
                 ffn1_wd[l].astype(BF16))
        zhg, q, k, vt, ksum, zcv = _inproj(y, norm_w[l, 2:3], mix_w_in[l].astype(BF16))
        ohg = _hgrn(zhg, hg_lb, hg_norm_w[l:l + 1], l)
        oat = _moba(rel_bias, q, k, vt, ksum, bias)
        y = _outproj(y, ohg, oat, zcv, conv_w[l], mix_w_out[l].astype(BF16), norm_w[l, 3:4])
        y = _ffn(y, norm_w[l, 4:6], ffn2_wg[l].astype(BF16), ffn2_wu[l].astype(BF16),
                 ffn2_wd[l].astype(BF16))
    return y.reshape(batch, seq, D_MODEL)
```

```python
import functools
import math

import numpy as np
import jax
import jax.numpy as jnp
from jax import lax
from jax.experimental import pallas as pl
from jax.experimental.pallas import tpu as pltpu

F32 = jnp.float32
BF16 = jnp.bfloat16

D_MODEL = 1024
D_FF = 2816
HG_HEADS = 4
HG_DIM = 128
HG_WIDTH = HG_HEADS * HG_DIM
AT_HEADS = 4
AT_DIM = 64
AT_WIDTH = AT_HEADS * AT_DIM
MOBA_BLOCK = 256
MOBA_TOPK = 3
REL_BUCKETS = 32
REL_MAX_DIST = 1024
CV_WIDTH = 256
CV_KERNEL = 3
D_MIX = HG_WIDTH + AT_WIDTH + CV_WIDTH
D_IN = 4 * HG_WIDTH + 3 * AT_WIDTH + 3 * CV_WIDTH
EPS = 1e-6

ROW_TILE = 512
HG_TILE = 256
KSUM_ROWS = 8
V_ROWS = AT_DIM + 16
LOG2E = math.log2(math.e)
FAR_GROUP = 4
NEG = -1e30
VMEM_LIMIT = 56 * 1024 * 1024
FF_CHUNKS = ((0, 768), (768, 1536), (1536, 2304), (2304, 2816))

_NT = (((1,), (1,)), ((), ()))
_TN = (((0,), (0,)), ((), ()))


def _rms(x, w):
    ms = jnp.mean(x * x, axis=-1, keepdims=True)
    return x * lax.rsqrt(ms + EPS) * w


def _const_spec(shape):
    nd = len(shape)
    return pl.BlockSpec(shape, lambda *_: (0,) * nd, pipeline_mode=pl.Buffered(1))


def _layer_spec(shape, layer):
    nd = len(shape)
    return pl.BlockSpec((None,) + tuple(shape), lambda *_: (layer,) + (0,) * nd,
                        pipeline_mode=pl.Buffered(1))


def _params(*sem, flags=None):
    return pltpu.CompilerParams(dimension_semantics=sem, vmem_limit_bytes=VMEM_LIMIT, flags=flags)


def _ffn_kernel(x_ref, nw_ref, wg_ref, wu_ref, wd_ref, o_ref):
    x = x_ref[...]
    xn = _rms(x, nw_ref[0:1, :]).astype(BF16)
    h = None
    for c0, c1 in FF_CHUNKS:
        g = jnp.dot(xn, wg_ref[:, c0:c1], preferred_element_type=F32)
        u = jnp.dot(xn, wu_ref[:, c0:c1], preferred_element_type=F32)
        a = (g * jax.nn.sigmoid(g) * u).astype(BF16)
        part = jnp.dot(a, wd_ref[c0:c1, :], preferred_element_type=F32)
        h = part if h is None else h + part
    o_ref[...] = x + 0.5 * _rms(h, nw_ref[1:2, :])


def _ffn(x, nw2, wg, wu, wd, layer):
    s = x.shape[0]
    return pl.pallas_call(
        _ffn_kernel,
        grid=(s // ROW_TILE,),
        in_specs=[
            pl.BlockSpec((ROW_TILE, D_MODEL), lambda i: (i, 0)),
            _const_spec((2, D_MODEL)),
            _layer_spec((D_MODEL, D_FF), layer),
            _layer_spec((D_MODEL, D_FF), layer),
            _layer_spec((D_FF, D_MODEL), layer),
        ],
        out_specs=pl.BlockSpec((ROW_TILE, D_MODEL), lambda i: (i, 0)),
        out_shape=jax.ShapeDtypeStruct((s, D_MODEL), F32),
        compiler_params=_params("parallel"),
        name="ffn",
    )(x, nw2, wg, wu, wd)


def _inproj_kernel(x_ref, nw_ref, w_ref, zhg_ref, q_ref, k_ref, vt_ref, ksum_ref, zcv_ref):
    xn = _rms(x_ref[...], nw_ref[...]).astype(BF16)
    c = 4 * HG_WIDTH
    zhg_ref[...] = jnp.dot(xn, w_ref[:, 0:c], preferred_element_type=F32)
    q_ref[...] = jnp.dot(xn, w_ref[:, c:c + AT_WIDTH], preferred_element_type=F32)
    k = jnp.dot(xn, w_ref[:, c + AT_WIDTH:c + 2 * AT_WIDTH], preferred_element_type=F32)
    v = jnp.dot(xn, w_ref[:, c + 2 * AT_WIDTH:c + 3 * AT_WIDTH], preferred_element_type=F32)
    c += 3 * AT_WIDTH
    zcv_ref[...] = jnp.dot(xn, w_ref[:, c:c + 3 * CV_WIDTH], preferred_element_type=F32)
    ones = jnp.ones((V_ROWS - AT_DIM, MOBA_BLOCK), F32)
    for b in range(ROW_TILE // MOBA_BLOCK):
        kb = k[b * MOBA_BLOCK:(b + 1) * MOBA_BLOCK, :]
        vbt = v[b * MOBA_BLOCK:(b + 1) * MOBA_BLOCK, :].T
        k_ref[b] = kb.astype(BF16)
        for h in range(AT_HEADS):
            vt_ref[b, h] = jnp.concatenate(
                [vbt[h * AT_DIM:(h + 1) * AT_DIM, :], ones], axis=0).astype(BF16)
        ksum_ref[b * KSUM_ROWS:(b + 1) * KSUM_ROWS, :] = jnp.sum(
            kb.reshape(MOBA_BLOCK // KSUM_ROWS, KSUM_ROWS, AT_WIDTH), axis=0)


def _inproj(x, nw, w_in, layer):
    s = x.shape[0]
    nblk = s // MOBA_BLOCK
    bpt = ROW_TILE // MOBA_BLOCK
    return pl.pallas_call(
        _inproj_kernel,
        grid=(s // ROW_TILE,),
        in_specs=[
            pl.BlockSpec((ROW_TILE, D_MODEL), lambda i: (i, 0)),
            _const_spec((1, D_MODEL)),
            _layer_spec((D_MODEL, D_IN), layer),
        ],
        out_specs=[
            pl.BlockSpec((ROW_TILE, 4 * HG_WIDTH), lambda i: (i, 0)),
            pl.BlockSpec((ROW_TILE, AT_WIDTH), lambda i: (i, 0)),
            pl.BlockSpec((bpt, MOBA_BLOCK, AT_WIDTH), lambda i: (i, 0, 0)),
            pl.BlockSpec((bpt, AT_HEADS, V_ROWS, MOBA_BLOCK), lambda i: (i, 0, 0, 0)),
            pl.BlockSpec((bpt * KSUM_ROWS, AT_WIDTH), lambda i: (i, 0)),
            pl.BlockSpec((ROW_TILE, 3 * CV_WIDTH), lambda i: (i, 0)),
        ],
        out_shape=[
            jax.ShapeDtypeStruct((s, 4 * HG_WIDTH), F32),
            jax.ShapeDtypeStruct((s, AT_WIDTH), F32),
            jax.ShapeDtypeStruct((nblk, MOBA_BLOCK, AT_WIDTH), BF16),
            jax.ShapeDtypeStruct((nblk, AT_HEADS, V_ROWS, MOBA_BLOCK), BF16),
            jax.ShapeDtypeStruct((nblk * KSUM_ROWS, AT_WIDTH), F32),
            jax.ShapeDtypeStruct((s, 3 * CV_WIDTH), F32),
        ],
        compiler_params=_params("parallel"),
        name="inproj",
    )(x, nw, w_in)


def _hg_levels():
    levels = []
    n = HG_TILE
    while n >= 2:
        levels.append(n)
        n //= 2
    return levels


def _hgrn_head(q, fp, v, gate, lb, nw, st_ref, row, masks):
    u = jnp.exp2(jnp.minimum(fp * -LOG2E, 126.0))
    r = 1.0 / (1.0 + u)
    logf = jnp.log2(lb + (1.0 - lb) * r)
    kk = (1.0 - lb) * (u * r)
    vb = v.astype(BF16)

    b = logf
    sh = 1
    while sh < 8:
        b = b + jnp.where(row >= sh, pltpu.roll(b, sh, axis=0), 0.0)
        sh *= 2
    while sh < HG_TILE:
        b = jnp.concatenate([b[:sh], b[sh:] + b[:-sh]], axis=0)
        sh *= 2

    diag, same = masks
    scores = jnp.where(
        diag, lax.dot_general(q.astype(BF16), kk.astype(BF16), _NT, preferred_element_type=F32), 0.0)
    for n in _hg_levels():
        half = n // 2
        if half >= 8:
            qp, kp = [], []
            zero = jnp.zeros((half, HG_DIM), F32)
            for lo in range(0, HG_TILE, n):
                mid, hi = lo + half, lo + n
                bm = b[mid - 1:mid, :]
                qp += [zero, q[mid:hi] * jnp.exp2(b[mid:hi] - bm)]
                kp += [kk[lo:mid] * jnp.exp2(bm - b[lo:mid]), zero]
            qs = jnp.concatenate(qp, axis=0)
            ks = jnp.concatenate(kp, axis=0)
        else:
            upper = (row & (n - 1)) >= half
            if n == 2:
                ex = jnp.exp2(jnp.where(upper, logf, 0.0))
            else:
                b3 = b.reshape(HG_TILE // 8, 8, HG_DIM)
                sub = lax.broadcasted_iota(jnp.int32, (HG_TILE // 8, 8, HG_DIM), 1)
                bm3 = None
                for lo in range(0, 8, n):
                    piece = jnp.broadcast_to(b3[:, lo + half - 1:lo + half, :], b3.shape)
                    bm3 = piece if bm3 is None else jnp.where(sub >= lo, piece, bm3)
                ex = jnp.exp2(-jnp.abs(b - bm3.reshape(HG_TILE, HG_DIM)))
            qs = jnp.where(upper, q * ex, 0.0)
            ks = jnp.where(upper, 0.0, kk * ex)
        lvl = lax.dot_general(qs.astype(BF16), ks.astype(BF16), _NT, preferred_element_type=F32)
        scores = scores + (lvl if n == HG_TILE else jnp.where(same[n], lvl, 0.0))

    st = st_ref[...]
    b_last = b[HG_TILE - 1:HG_TILE, :]
    o = jnp.dot(scores.astype(BF16), vb, preferred_element_type=F32)
    o = o + lax.dot_general((q * jnp.exp2(b)).astype(BF16), st.astype(BF16), _NT,
                            preferred_element_type=F32)
    kdec = (kk * jnp.exp2(b_last - b)).astype(BF16)
    st_ref[...] = st * jnp.exp2(b_last) + lax.dot_general(vb, kdec, _TN, preferred_element_type=F32)
    return _rms(o, nw) * (gate * jax.nn.sigmoid(gate))


def _hgrn_kernel(layer, q_ref, f_ref, i_ref, g_ref, lb_ref, nw_ref, o_ref, st_ref):
    @pl.when(pl.program_id(0) == 0)
    def _():
        st_ref[...] = jnp.zeros_like(st_ref)

    lbraw = lb_ref[...]
    e = jnp.exp(lbraw - jnp.max(lbraw, axis=0, keepdims=True))
    soft = e / jnp.sum(e, axis=0, keepdims=True)
    lb = jnp.sum(soft[0:layer + 1, :], axis=0, keepdims=True) - soft[0:1, :]

    row = lax.broadcasted_iota(jnp.int32, (HG_TILE, HG_DIM), 0)
    ti = lax.broadcasted_iota(jnp.int32, (HG_TILE, HG_TILE), 0)
    si = lax.broadcasted_iota(jnp.int32, (HG_TILE, HG_TILE), 1)
    masks = (ti == si, {n: (ti & -n) == (si & -n) for n in _hg_levels() if n < HG_TILE})
    for h in range(HG_HEADS):
        cols = slice(h * HG_DIM, (h + 1) * HG_DIM)
        o_ref[:, cols] = _hgrn_head(q_ref[:, cols], f_ref[:, cols], i_ref[:, cols], g_ref[:, cols],
                                    lb[:, cols], nw_ref[...], st_ref.at[h], row, masks
                                    ).astype(o_ref.dtype)


def _hgrn(zhg, hg_lb, hg_nw, layer):
    s = zhg.shape[0]
    depth = hg_lb.shape[0]

    def col(k):
        return pl.BlockSpec((HG_TILE, HG_WIDTH), lambda t: (t, k))

    return pl.pallas_call(
        functools.partial(_hgrn_kernel, layer),
        grid=(s // HG_TILE,),
        in_specs=[
            col(0), col(1), col(2), col(3),
            pl.BlockSpec((depth, HG_WIDTH), lambda t: (0, 0)),
            pl.BlockSpec((1, HG_DIM), lambda t: (0, 0)),
        ],
        out_specs=pl.BlockSpec((HG_TILE, HG_WIDTH), lambda t: (t, 0)),
        out_shape=jax.ShapeDtypeStruct((s, HG_WIDTH), BF16),
        scratch_shapes=[pltpu.VMEM((HG_HEADS, HG_DIM, HG_DIM), F32)],
        compiler_params=_params("arbitrary"),
        name="hgrn2",
    )(zhg, zhg, zhg, zhg, hg_lb, hg_nw)


N_BIAS_TABLES = 5


def _bucket_thresholds():
    max_exact = REL_BUCKETS // 2
    d = np.arange(1, 2 * REL_MAX_DIST, dtype=np.float64)
    large = max_exact + (np.log(d / max_exact) / math.log(REL_MAX_DIST / max_exact)
                         * (REL_BUCKETS - max_exact)).astype(np.int64)
    large = np.minimum(large, REL_BUCKETS - 1)
    bucket = np.where(d < max_exact, d.astype(np.int64), large)
    thr = [0] * REL_BUCKETS
    for bkt in range(1, REL_BUCKETS):
        thr[bkt] = int(d[np.argmax(bucket >= bkt)])
    assert (N_BIAS_TABLES - 1) * MOBA_BLOCK + 1 >= thr[REL_BUCKETS - 1]
    return thr


def _bias_kernel(rb_ref, o_ref):
    thr = _bucket_thresholds()
    key = lax.broadcasted_iota(jnp.int32, (MOBA_BLOCK, MOBA_BLOCK), 0)
    qry = lax.broadcasted_iota(jnp.int32, (MOBA_BLOCK, MOBA_BLOCK), 1)
    o_ref[N_BIAS_TABLES:] = jnp.zeros((N_BIAS_TABLES - 1,) + o_ref.shape[1:], F32)
    for u in range(N_BIAS_TABLES):
        t = N_BIAS_TABLES - 1 - u
        dist = qry - key + t * MOBA_BLOCK
        for h in range(AT_HEADS):
            val = jnp.full((MOBA_BLOCK, MOBA_BLOCK), rb_ref[REL_BUCKETS - 1, h], F32)
            for bkt in range(REL_BUCKETS - 2, -1, -1):
                val = jnp.where(dist < thr[bkt + 1], rb_ref[bkt, h], val)
            val = val * LOG2E
            if t == 0:
                val = jnp.where(dist < 0, NEG, val)
            o_ref[u, :, h * MOBA_BLOCK:(h + 1) * MOBA_BLOCK] = val


def _bias_tables(rel_bias):
    return pl.pallas_call(
        _bias_kernel,
        in_specs=[pl.BlockSpec(memory_space=pltpu.SMEM)],
        out_shape=jax.ShapeDtypeStruct((2 * N_BIAS_TABLES - 1, MOBA_BLOCK, AT_HEADS * MOBA_BLOCK), F32),
        name="moba_bias",
    )(rel_bias)


def _moba_kernel(rb_ref, q_ref, k_ref, vt_ref, ksum_ref, bias_ref, o_ref,
                 kmean_ref, mnear_ref, mfar_ref, qs_ref, m_ref, acc_ref, sa_ref, sb_ref):
    i = pl.program_id(0)
    nblk = k_ref.shape[0]

    @pl.when(i == 0)
    def _():
        ks = ksum_ref[...].reshape(nblk, KSUM_ROWS, AT_WIDTH)
        kmean_ref[...] = jnp.sum(ks, axis=1) * (1.0 / MOBA_BLOCK)

    qt = q_ref[...].T
    row_head = lax.broadcasted_iota(jnp.int32, (AT_WIDTH, MOBA_BLOCK), 0) // AT_DIM
    qmt = jnp.concatenate([jnp.where(row_head == h, qt, 0.0) for h in range(AT_HEADS)], axis=1)
    qs_ref[...] = (qmt * (AT_DIM ** -0.5 * LOG2E)).astype(BF16)

    ncol = AT_HEADS * MOBA_BLOCK
    jio = lax.broadcasted_iota(jnp.int32, (nblk, ncol), 0).astype(F32)
    fi = i.astype(F32)
    n_far = jnp.maximum(i - (N_BIAS_TABLES - 1), 0)
    gate = jnp.dot(kmean_ref[...], qmt, precision=lax.Precision.HIGHEST,
                   preferred_element_type=F32)
    gate = jnp.where(jio < fi, gate, -jnp.inf)
    sel = jio == fi
    for _ in range(MOBA_TOPK):
        mx = jnp.max(gate, axis=0, keepdims=True)
        cand = jnp.where(gate == mx, jio, float(nblk))
        idx = jnp.min(cand, axis=0, keepdims=True)
        pick = (jio == idx) & (mx > -jnp.inf)
        sel = sel | pick
        gate = jnp.where(pick, -jnp.inf, gate)
    col_head = lax.broadcasted_iota(jnp.int32, (1, ncol), 1) // MOBA_BLOCK
    far_bias = jnp.zeros((1, ncol), F32)
    for h in range(AT_HEADS):
        far_bias = jnp.where(col_head == h, rb_ref[REL_BUCKETS - 1, h] * LOG2E, far_bias)
    mnear_ref[...] = jnp.where(sel, 0.0, NEG)
    mfar_ref[...] = jnp.where(sel & (jio < n_far.astype(F32)), far_bias, NEG)
    m_ref[...] = jnp.full(m_ref.shape, NEG, F32)
    acc_ref[...] = jnp.zeros(acc_ref.shape, F32)

    def logits_into(s_ref, j0, nb):
        kg = k_ref[pl.ds(j0, nb)].reshape(nb * MOBA_BLOCK, AT_WIDTH)
        s_ref[0:nb * MOBA_BLOCK, :] = jnp.dot(kg, qs_ref[...], preferred_element_type=F32)

    def fold(s_ref, j0, nb, mask_ref, table, nxt_ref=None, nxt_j0=None):
        for g in range(nb):
            rows = slice(g * MOBA_BLOCK, (g + 1) * MOBA_BLOCK)
            if nxt_ref is not None and g < FAR_GROUP:
                nxt_ref[rows, :] = jnp.dot(k_ref[nxt_j0 + g], qs_ref[...], preferred_element_type=F32)
            for h in range(AT_HEADS):
                cols = slice(h * MOBA_BLOCK, (h + 1) * MOBA_BLOCK)
                mrow = mask_ref[pl.ds(j0 + g, 1), cols]
                s = s_ref[rows, cols]
                if table is not None:
                    s = s + mrow + bias_ref[table + g, :, cols]
                mb = jnp.max(s, axis=0, keepdims=True)
                p = jnp.exp2(s - mb).astype(BF16)
                if table is None:
                    mb = mb + mrow
                pv = jnp.dot(vt_ref[j0 + g, h], p, preferred_element_type=F32)
                m = m_ref[:, cols]
                m_new = jnp.maximum(m, mb)
                acc_ref[:, cols] = (jnp.exp2(m - m_new) * acc_ref[:, cols]
                                    + jnp.exp2(mb - m_new) * pv)
                m_ref[:, cols] = m_new

    near0 = jnp.maximum(i - (N_BIAS_TABLES - 1), 0)
    n_groups = (n_far + FAR_GROUP - 1) // FAR_GROUP
    last_group = nblk // FAR_GROUP - 1
    logits_into(sa_ref, near0, N_BIAS_TABLES)
    fold(sa_ref, near0, N_BIAS_TABLES, mnear_ref, near0 - (i - (N_BIAS_TABLES - 1)), sb_ref, 0)

    def far_pair(gp, carry):
        ga = 2 * gp
        fold(sb_ref, FAR_GROUP * ga, FAR_GROUP, mfar_ref, None, sa_ref, FAR_GROUP * (ga + 1))
        fold(sa_ref, FAR_GROUP * (ga + 1), FAR_GROUP, mfar_ref, None,
             sb_ref, FAR_GROUP * jnp.minimum(ga + 2, last_group))
        return carry

    lax.fori_loop(0, (n_groups + 1) // 2, far_pair, 0)
    acc = acc_ref[...]
    ot = acc[0:AT_DIM, :] / acc[AT_DIM:AT_DIM + 1, :]
    o_ref[...] = jnp.concatenate(
        [ot[:, h * MOBA_BLOCK:(h + 1) * MOBA_BLOCK] for h in range(AT_HEADS)], axis=0
    ).T.astype(o_ref.dtype)


def _moba(rel_bias, q, k, vt, ksum, bias):
    s = q.shape[0]
    nblk = s // MOBA_BLOCK
    assert nblk >= N_BIAS_TABLES and FAR_GROUP <= N_BIAS_TABLES and nblk % FAR_GROUP == 0
    return pl.pallas_call(
        _moba_kernel,
        grid=(nblk,),
        in_specs=[
            pl.BlockSpec(memory_space=pltpu.SMEM),
            pl.BlockSpec((MOBA_BLOCK, AT_WIDTH), lambda i: (i, 0)),
            _const_spec((nblk, MOBA_BLOCK, AT_WIDTH)),
            _const_spec((nblk, AT_HEADS, V_ROWS, MOBA_BLOCK)),
            _const_spec((nblk * KSUM_ROWS, AT_WIDTH)),
            _const_spec((2 * N_BIAS_TABLES - 1, MOBA_BLOCK, AT_HEADS * MOBA_BLOCK)),
        ],
        out_specs=pl.BlockSpec((MOBA_BLOCK, AT_WIDTH), lambda i: (i, 0)),
        out_shape=jax.ShapeDtypeStruct((s, AT_WIDTH), BF16),
        scratch_shapes=[
            pltpu.VMEM((nblk, AT_WIDTH), F32),
            pltpu.VMEM((nblk, AT_HEADS * MOBA_BLOCK), F32),
            pltpu.VMEM((nblk, AT_HEADS * MOBA_BLOCK), F32),
            pltpu.VMEM((AT_WIDTH, AT_HEADS * MOBA_BLOCK), BF16),
            pltpu.VMEM((1, AT_HEADS * MOBA_BLOCK), F32),
            pltpu.VMEM((V_ROWS, AT_HEADS * MOBA_BLOCK), F32),
            pltpu.VMEM((N_BIAS_TABLES * MOBA_BLOCK, AT_HEADS * MOBA_BLOCK), F32),
            pltpu.VMEM((FAR_GROUP * MOBA_BLOCK, AT_HEADS * MOBA_BLOCK), F32),
        ],
        compiler_params=_params("arbitrary"),
        name="moba",
    )(rel_bias, q, k, vt, ksum, bias)


def _outproj_kernel(x_ref, ohg_ref, oat_ref, zcv_ref, halo_ref, cw_ref, w_ref, nw_ref, o_ref):
    i = pl.program_id(0)
    zcv = zcv_ref[...]
    bgate = zcv[:, 0:CV_WIDTH]
    u = zcv[:, CV_WIDTH:2 * CV_WIDTH] * zcv[:, 2 * CV_WIDTH:3 * CV_WIDTH]
    halo = halo_ref[...]
    uh = halo[:, CV_WIDTH:2 * CV_WIDTH] * halo[:, 2 * CV_WIDTH:3 * CV_WIDTH]
    uh = jnp.where(i > 0, uh, 0.0)
    row = lax.broadcasted_iota(jnp.int32, u.shape, 0)
    u1 = jnp.where(row == 0, uh[7:8, :], pltpu.roll(u, 1, axis=0))
    u2 = jnp.where(row == 0, uh[6:7, :], jnp.where(row == 1, uh[7:8, :], pltpu.roll(u, 2, axis=0)))
    cw = cw_ref[...]
    ocv = bgate * (cw[0:1, :] * u2 + cw[1:2, :] * u1 + cw[2:3, :] * u)
    h = jnp.dot(ohg_ref[...], w_ref[0:HG_WIDTH, :], preferred_element_type=F32)
    h = h + jnp.dot(oat_ref[...], w_ref[HG_WIDTH:HG_WIDTH + AT_WIDTH, :], preferred_element_type=F32)
    h = h + jnp.dot(ocv.astype(BF16), w_ref[HG_WIDTH + AT_WIDTH:D_MIX, :], preferred_element_type=F32)
    o_ref[...] = x_ref[...] + _rms(h, nw_ref[...])


def _outproj(x, ohg, oat, zcv, conv_w, w_out, nw, layer):
    s = x.shape[0]
    halo_blocks = ROW_TILE // 8
    return pl.pallas_call(
        _outproj_kernel,
        grid=(s // ROW_TILE,),
        in_specs=[
            pl.BlockSpec((ROW_TILE, D_MODEL), lambda i: (i, 0)),
            pl.BlockSpec((ROW_TILE, HG_WIDTH), lambda i: (i, 0)),
            pl.BlockSpec((ROW_TILE, AT_WIDTH), lambda i: (i, 0)),
            pl.BlockSpec((ROW_TILE, 3 * CV_WIDTH), lambda i: (i, 0)),
            pl.BlockSpec((8, 3 * CV_WIDTH), lambda i: (jnp.maximum(i * halo_blocks - 1, 0), 0)),
            _const_spec((CV_KERNEL, CV_WIDTH)),
            _layer_spec((D_MIX, D_MODEL), layer),
            _const_spec((1, D_MODEL)),
        ],
        out_specs=pl.BlockSpec((ROW_TILE, D_MODEL), lambda i: (i, 0)),
        out_shape=jax.ShapeDtypeStruct((s, D_MODEL), F32),
        compiler_params=_params("parallel"),
        name="outproj",
    )(x, ohg, oat, zcv, zcv, conv_w, w_out, nw)


def kernel(x, norm_w, ffn1_wg, ffn1_wu, ffn1_wd, mix_w_in, mix_w_out, hg_lb, hg_norm_w, conv_w,
           ffn2_wg, ffn2_wu, ffn2_wd, rel_bias):
    batch, seq, _ = x.shape
    depth = norm_w.shape[0]
    assert batch == 1 and seq % ROW_TILE == 0 and seq % MOBA_BLOCK == 0
    rel_bias = rel_bias.astype(F32)
    bias = _bias_tables(rel_bias)
    y = x.reshape(seq, D_MODEL)
    w1 = [w.astype(BF16) for w in (ffn1_wg, ffn1_wu, ffn1_wd)]
    w2 = [w.astype(BF16) for w in (ffn2_wg, ffn2_wu, ffn2_wd)]
    w_in = mix_w_in.astype(BF16)
    w_out = mix_w_out.astype(BF16)
    for l in range(depth):
        y = _ffn(y, norm_w[l, 0:2], *w1, l)
        zhg, q, k, vt, ksum, zcv = _inproj(y, norm_w[l, 2:3], w_in, l)
        ohg = _hgrn(zhg, hg_lb, hg_norm_w[l:l + 1], l)
        oat = _moba(rel_bias, q, k, vt, ksum, bias)
        y = _outproj(y, ohg, oat, zcv, conv_w[l], w_out, norm_w[l, 3:4], l)
        y = _ffn(y, norm_w[l, 4:6], *w2, l)
    return y.reshape(batch, seq, D_MODEL)
```

```python
import functools
import math

import numpy as np
import jax
import jax.numpy as jnp
from jax import lax
from jax.experimental import pallas as pl
from jax.experimental.pallas import tpu as pltpu

F32 = jnp.float32
BF16 = jnp.bfloat16

D_MODEL = 1024
D_FF = 2816
HG_HEADS = 4
HG_DIM = 128
HG_WIDTH = HG_HEADS * HG_DIM
AT_HEADS = 4
AT_DIM = 64
AT_WIDTH = AT_HEADS * AT_DIM
MOBA_BLOCK = 256
MOBA_TOPK = 3
REL_BUCKETS = 32
REL_MAX_DIST = 1024
CV_WIDTH = 256
CV_KERNEL = 3
D_MIX = HG_WIDTH + AT_WIDTH + CV_WIDTH
D_IN = 4 * HG_WIDTH + 3 * AT_WIDTH + 3 * CV_WIDTH
EPS = 1e-6

ROW_TILE = 512
FFN_TILE = 1024
FFN_CHAIN = 512
HG_TILE = 256
KSUM_ROWS = 8
V_ROWS = AT_DIM + 16
LOG2E = math.log2(math.e)
FAR_GROUP = 2
NEG = -1e30
VMEM_LIMIT = 56 * 1024 * 1024
FF_CHUNKS = ((0, 768), (768, 1536), (1536, 2304), (2304, 2816))

_NT = (((1,), (1,)), ((), ()))
_TN = (((0,), (0,)), ((), ()))


def _rms(x, w):
    ms = jnp.mean(x * x, axis=-1, keepdims=True)
    return x * lax.rsqrt(ms + EPS) * w


def _const_spec(shape):
    nd = len(shape)
    return pl.BlockSpec(shape, lambda *_: (0,) * nd, pipeline_mode=pl.Buffered(1))


def _layer_spec(shape, layer):
    nd = len(shape)
    return pl.BlockSpec((None,) + tuple(shape), lambda *_: (layer,) + (0,) * nd,
                        pipeline_mode=pl.Buffered(1))


def _params(*sem, flags=None):
    return pltpu.CompilerParams(dimension_semantics=sem, vmem_limit_bytes=VMEM_LIMIT, flags=flags)


def _ffn_rows(x, nw_ref, wg_ref, wu_ref, wd_ref):
    xn = _rms(x, nw_ref[0:1, :]).astype(BF16)
    h = None
    for c0, c1 in FF_CHUNKS:
        g = jnp.dot(xn, wg_ref[:, c0:c1], preferred_element_type=F32)
        u = jnp.dot(xn, wu_ref[:, c0:c1], preferred_element_type=F32)
        a = (g * jax.nn.sigmoid(g) * u).astype(BF16)
        part = jnp.dot(a, wd_ref[c0:c1, :], preferred_element_type=F32)
        h = part if h is None else h + part
    return x + 0.5 * _rms(h, nw_ref[1:2, :])


def _ffn_kernel(x_ref, nw_ref, wg_ref, wu_ref, wd_ref, o_ref):
    for r0 in range(0, FFN_TILE, FFN_CHAIN):
        rows = slice(r0, r0 + FFN_CHAIN)
        o_ref[rows, :] = _ffn_rows(x_ref[rows, :], nw_ref, wg_ref, wu_ref, wd_ref)


def _ffn(x, nw2, wg, wu, wd, layer):
    s = x.shape[0]
    return pl.pallas_call(
        _ffn_kernel,
        grid=(s // FFN_TILE,),
        in_specs=[
            pl.BlockSpec((FFN_TILE, D_MODEL), lambda i: (i, 0)),
            _const_spec((2, D_MODEL)),
            _layer_spec((D_MODEL, D_FF), layer),
            _layer_spec((D_MODEL, D_FF), layer),
            _layer_spec((D_FF, D_MODEL), layer),
        ],
        out_specs=pl.BlockSpec((FFN_TILE, D_MODEL), lambda i: (i, 0)),
        out_shape=jax.ShapeDtypeStruct((s, D_MODEL), F32),
        compiler_params=_params("parallel"),
        name="ffn",
    )(x, nw2, wg, wu, wd)


def _inproj_kernel(x_ref, nw_ref, w_ref, zhg_ref, q_ref, k_ref, vt_ref, ksum_ref, zcv_ref):
    xn = _rms(x_ref[...], nw_ref[...]).astype(BF16)
    c = 4 * HG_WIDTH
    zhg_ref[...] = jnp.dot(xn, w_ref[:, 0:c], preferred_element_type=F32)
    q_ref[...] = jnp.dot(xn, w_ref[:, c:c + AT_WIDTH], preferred_element_type=F32)
    k = jnp.dot(xn, w_ref[:, c + AT_WIDTH:c + 2 * AT_WIDTH], preferred_element_type=F32)
    v = jnp.dot(xn, w_ref[:, c + 2 * AT_WIDTH:c + 3 * AT_WIDTH], preferred_element_type=F32)
    c += 3 * AT_WIDTH
    zcv_ref[...] = jnp.dot(xn, w_ref[:, c:c + 3 * CV_WIDTH], preferred_element_type=F32)
    ones = jnp.ones((V_ROWS - AT_DIM, MOBA_BLOCK), F32)
    for b in range(ROW_TILE // MOBA_BLOCK):
        kb = k[b * MOBA_BLOCK:(b + 1) * MOBA_BLOCK, :]
        vbt = v[b * MOBA_BLOCK:(b + 1) * MOBA_BLOCK, :].T
        k_ref[b] = kb.astype(BF16)
        for h in range(AT_HEADS):
            vt_ref[b, h] = jnp.concatenate(
                [vbt[h * AT_DIM:(h + 1) * AT_DIM, :], ones], axis=0).astype(BF16)
        ksum_ref[b * KSUM_ROWS:(b + 1) * KSUM_ROWS, :] = jnp.sum(
            kb.reshape(MOBA_BLOCK // KSUM_ROWS, KSUM_ROWS, AT_WIDTH), axis=0)


def _inproj(x, nw, w_in, layer):
    s = x.shape[0]
    nblk = s // MOBA_BLOCK
    bpt = ROW_TILE // MOBA_BLOCK
    return pl.pallas_call(
        _inproj_kernel,
        grid=(s // ROW_TILE,),
        in_specs=[
            pl.BlockSpec((ROW_TILE, D_MODEL), lambda i: (i, 0)),
            _const_spec((1, D_MODEL)),
            _layer_spec((D_MODEL, D_IN), layer),
        ],
        out_specs=[
            pl.BlockSpec((ROW_TILE, 4 * HG_WIDTH), lambda i: (i, 0)),
            pl.BlockSpec((ROW_TILE, AT_WIDTH), lambda i: (i, 0)),
            pl.BlockSpec((bpt, MOBA_BLOCK, AT_WIDTH), lambda i: (i, 0, 0)),
            pl.BlockSpec((bpt, AT_HEADS, V_ROWS, MOBA_BLOCK), lambda i: (i, 0, 0, 0)),
            pl.BlockSpec((bpt * KSUM_ROWS, AT_WIDTH), lambda i: (i, 0)),
            pl.BlockSpec((ROW_TILE, 3 * CV_WIDTH), lambda i: (i, 0)),
        ],
        out_shape=[
            jax.ShapeDtypeStruct((s, 4 * HG_WIDTH), F32),
            jax.ShapeDtypeStruct((s, AT_WIDTH), F32),
            jax.ShapeDtypeStruct((nblk, MOBA_BLOCK, AT_WIDTH), BF16),
            jax.ShapeDtypeStruct((nblk, AT_HEADS, V_ROWS, MOBA_BLOCK), BF16),
            jax.ShapeDtypeStruct((nblk * KSUM_ROWS, AT_WIDTH), F32),
            jax.ShapeDtypeStruct((s, 3 * CV_WIDTH), F32),
        ],
        compiler_params=_params("parallel"),
        name="inproj",
    )(x, nw, w_in)


def _hg_levels():
    levels = []
    n = HG_TILE
    while n >= 2:
        levels.append(n)
        n //= 2
    return levels


def _hgrn_head(q, fp, v, gate, lb, nw, st_ref, row, masks):
    u = jnp.exp2(jnp.minimum(fp * -LOG2E, 126.0))
    r = 1.0 / (1.0 + u)
    logf = jnp.log2(lb + (1.0 - lb) * r)
    kk = (1.0 - lb) * (u * r)
    vb = v.astype(BF16)

    b = logf
    sh = 1
    while sh < 8:
        b = b + jnp.where(row >= sh, pltpu.roll(b, sh, axis=0), 0.0)
        sh *= 2
    while sh < HG_TILE:
        b = jnp.concatenate([b[:sh], b[sh:] + b[:-sh]], axis=0)
        sh *= 2

    diag, same = masks
    scores = jnp.where(
        diag, lax.dot_general(q.astype(BF16), kk.astype(BF16), _NT, preferred_element_type=F32), 0.0)
    for n in _hg_levels():
        half = n // 2
        if half >= 8:
            qp, kp = [], []
            zero = jnp.zeros((half, HG_DIM), F32)
            for lo in range(0, HG_TILE, n):
                mid, hi = lo + half, lo + n
                bm = b[mid - 1:mid, :]
                qp += [zero, q[mid:hi] * jnp.exp2(b[mid:hi] - bm)]
                kp += [kk[lo:mid] * jnp.exp2(bm - b[lo:mid]), zero]
            qs = jnp.concatenate(qp, axis=0)
            ks = jnp.concatenate(kp, axis=0)
        else:
            upper = (row & (n - 1)) >= half
            if n == 2:
                ex = jnp.exp2(jnp.where(upper, logf, 0.0))
            else:
                b3 = b.reshape(HG_TILE // 8, 8, HG_DIM)
                sub = lax.broadcasted_iota(jnp.int32, (HG_TILE // 8, 8, HG_DIM), 1)
                bm3 = None
                for lo in range(0, 8, n):
                    piece = jnp.broadcast_to(b3[:, lo + half - 1:lo + half, :], b3.shape)
                    bm3 = piece if bm3 is None else jnp.where(sub >= lo, piece, bm3)
                ex = jnp.exp2(-jnp.abs(b - bm3.reshape(HG_TILE, HG_DIM)))
            qs = jnp.where(upper, q * ex, 0.0)
            ks = jnp.where(upper, 0.0, kk * ex)
        lvl = lax.dot_general(qs.astype(BF16), ks.astype(BF16), _NT, preferred_element_type=F32)
        scores = scores + (lvl if n == HG_TILE else jnp.where(same[n], lvl, 0.0))

    st = st_ref[...]
    b_last = b[HG_TILE - 1:HG_TILE, :]
    o = jnp.dot(scores.astype(BF16), vb, preferred_element_type=F32)
    o = o + lax.dot_general((q * jnp.exp2(b)).astype(BF16), st.astype(BF16), _NT,
                            preferred_element_type=F32)
    kdec = (kk * jnp.exp2(b_last - b)).astype(BF16)
    st_ref[...] = st * jnp.exp2(b_last) + lax.dot_general(vb, kdec, _TN, preferred_element_type=F32)
    return _rms(o, nw) * (gate * jax.nn.sigmoid(gate))


def _hgrn_kernel(layer, q_ref, f_ref, i_ref, g_ref, lb_ref, nw_ref, o_ref, st_ref):
    @pl.when(pl.program_id(0) == 0)
    def _():
        st_ref[...] = jnp.zeros_like(st_ref)

    lbraw = lb_ref[...]
    e = jnp.exp(lbraw - jnp.max(lbraw, axis=0, keepdims=True))
    soft = e / jnp.sum(e, axis=0, keepdims=True)
    lb = jnp.sum(soft[0:layer + 1, :], axis=0, keepdims=True) - soft[0:1, :]

    row = lax.broadcasted_iota(jnp.int32, (HG_TILE, HG_DIM), 0)
    ti = lax.broadcasted_iota(jnp.int32, (HG_TILE, HG_TILE), 0)
    si = lax.broadcasted_iota(jnp.int32, (HG_TILE, HG_TILE), 1)
    masks = (ti == si, {n: (ti & -n) == (si & -n) for n in _hg_levels() if n < HG_TILE})
    for h in range(HG_HEADS):
        cols = slice(h * HG_DIM, (h + 1) * HG_DIM)
        o_ref[:, cols] = _hgrn_head(q_ref[:, cols], f_ref[:, cols], i_ref[:, cols], g_ref[:, cols],
                                    lb[:, cols], nw_ref[...], st_ref.at[h], row, masks
                                    ).astype(o_ref.dtype)


def _hgrn(zhg, hg_lb, hg_nw, layer):
    s = zhg.shape[0]
    depth = hg_lb.shape[0]

    def col(k):
        return pl.BlockSpec((HG_TILE, HG_WIDTH), lambda t: (t, k))

    return pl.pallas_call(
        functools.partial(_hgrn_kernel, layer),
        grid=(s // HG_TILE,),
        in_specs=[
            col(0), col(1), col(2), col(3),
            pl.BlockSpec((depth, HG_WIDTH), lambda t: (0, 0)),
            pl.BlockSpec((1, HG_DIM), lambda t: (0, 0)),
        ],
        out_specs=pl.BlockSpec((HG_TILE, HG_WIDTH), lambda t: (t, 0)),
        out_shape=jax.ShapeDtypeStruct((s, HG_WIDTH), BF16),
        scratch_shapes=[pltpu.VMEM((HG_HEADS, HG_DIM, HG_DIM), F32)],
        compiler_params=_params("arbitrary"),
        name="hgrn2",
    )(zhg, zhg, zhg, zhg, hg_lb, hg_nw)


N_BIAS_TABLES = 5


def _bucket_thresholds():
    max_exact = REL_BUCKETS // 2
    d = np.arange(1, 2 * REL_MAX_DIST, dtype=np.float64)
    large = max_exact + (np.log(d / max_exact) / math.log(REL_MAX_DIST / max_exact)
                         * (REL_BUCKETS - max_exact)).astype(np.int64)
    large = np.minimum(large, REL_BUCKETS - 1)
    bucket = np.where(d < max_exact, d.astype(np.int64), large)
    thr = [0] * REL_BUCKETS
    for bkt in range(1, REL_BUCKETS):
        thr[bkt] = int(d[np.argmax(bucket >= bkt)])
    assert (N_BIAS_TABLES - 1) * MOBA_BLOCK + 1 >= thr[REL_BUCKETS - 1]
    return thr


def _bias_kernel(rb_ref, o_ref):
    thr = _bucket_thresholds()
    key = lax.broadcasted_iota(jnp.int32, (MOBA_BLOCK, MOBA_BLOCK), 0)
    qry = lax.broadcasted_iota(jnp.int32, (MOBA_BLOCK, MOBA_BLOCK), 1)
    o_ref[N_BIAS_TABLES:] = jnp.zeros((N_BIAS_TABLES - 1,) + o_ref.shape[1:], F32)
    for u in range(N_BIAS_TABLES):
        t = N_BIAS_TABLES - 1 - u
        dist = qry - key + t * MOBA_BLOCK
        for h in range(AT_HEADS):
            val = jnp.full((MOBA_BLOCK, MOBA_BLOCK), rb_ref[REL_BUCKETS - 1, h], F32)
            for bkt in range(REL_BUCKETS - 2, -1, -1):
                val = jnp.where(dist < thr[bkt + 1], rb_ref[bkt, h], val)
            val = val * LOG2E
            if t == 0:
                val = jnp.where(dist < 0, NEG, val)
            o_ref[u, :, h * MOBA_BLOCK:(h + 1) * MOBA_BLOCK] = val


def _bias_tables(rel_bias):
    return pl.pallas_call(
        _bias_kernel,
        in_specs=[pl.BlockSpec(memory_space=pltpu.SMEM)],
        out_shape=jax.ShapeDtypeStruct((2 * N_BIAS_TABLES - 1, MOBA_BLOCK, AT_HEADS * MOBA_BLOCK), F32),
        name="moba_bias",
    )(rel_bias)


def _moba_kernel(rb_ref, q_ref, k_ref, vt_ref, ksum_ref, bias_ref, o_ref,
                 kmean_ref, mnear_ref, mfar_ref, qs_ref, m_ref, acc_ref, sa_ref, sb_ref):
    i = pl.program_id(0)
    nblk = k_ref.shape[0]

    @pl.when(i == 0)
    def _():
        ks = ksum_ref[...].reshape(nblk, KSUM_ROWS, AT_WIDTH)
        kmean_ref[...] = jnp.sum(ks, axis=1) * (1.0 / MOBA_BLOCK)

    qt = q_ref[...].T
    row_head = lax.broadcasted_iota(jnp.int32, (AT_WIDTH, MOBA_BLOCK), 0) // AT_DIM
    qmt = jnp.concatenate([jnp.where(row_head == h, qt, 0.0) for h in range(AT_HEADS)], axis=1)
    qs_ref[...] = (qmt * (AT_DIM ** -0.5 * LOG2E)).astype(BF16)

    ncol = AT_HEADS * MOBA_BLOCK
    jio = lax.broadcasted_iota(jnp.int32, (nblk, ncol), 0).astype(F32)
    fi = i.astype(F32)
    n_far = jnp.maximum(i - (N_BIAS_TABLES - 1), 0)
    gate = jnp.dot(kmean_ref[...], qmt, precision=lax.Precision.HIGHEST,
                   preferred_element_type=F32)
    gate = jnp.where(jio < fi, gate, -jnp.inf)
    sel = jio == fi
    for _ in range(MOBA_TOPK):
        mx = jnp.max(gate, axis=0, keepdims=True)
        cand = jnp.where(gate == mx, jio, float(nblk))
        idx = jnp.min(cand, axis=0, keepdims=True)
        pick = (jio == idx) & (mx > -jnp.inf)
        sel = sel | pick
        gate = jnp.where(pick, -jnp.inf, gate)
    col_head = lax.broadcasted_iota(jnp.int32, (1, ncol), 1) // MOBA_BLOCK
    far_bias = jnp.zeros((1, ncol), F32)
    for h in range(AT_HEADS):
        far_bias = jnp.where(col_head == h, rb_ref[REL_BUCKETS - 1, h] * LOG2E, far_bias)
    mnear_ref[...] = jnp.where(sel, 0.0, NEG)
    mfar_ref[...] = jnp.where(sel & (jio < n_far.astype(F32)), far_bias, NEG)
    m_ref[...] = jnp.full(m_ref.shape, NEG, F32)
    acc_ref[...] = jnp.zeros(acc_ref.shape, F32)

    def logits_into(s_ref, j0, nb):
        kg = k_ref[pl.ds(j0, nb)].reshape(nb * MOBA_BLOCK, AT_WIDTH)
        s_ref[0:nb * MOBA_BLOCK, :] = jnp.dot(kg, qs_ref[...], preferred_element_type=F32)

    def fold(s_ref, j0, nb, mask_ref, table):
        for h in range(AT_HEADS):
            cols = slice(h * MOBA_BLOCK, (h + 1) * MOBA_BLOCK)
            m = m_ref[:, cols]
            acc = acc_ref[:, cols]
            for g in range(nb):
                rows = slice(g * MOBA_BLOCK, (g + 1) * MOBA_BLOCK)
                mrow = mask_ref[pl.ds(j0 + g, 1), cols]
                s = s_ref[rows, cols]
                if table is not None:
                    s = s + mrow + bias_ref[table + g, :, cols]
                mb = jnp.max(s, axis=0, keepdims=True)
                p = jnp.exp2(s - mb).astype(BF16)
                if table is None:
                    mb = mb + mrow
                pv = jnp.dot(vt_ref[j0 + g, h], p, preferred_element_type=F32)
                m_new = jnp.maximum(m, mb)
                acc = jnp.exp2(m - m_new) * acc + jnp.exp2(mb - m_new) * pv
                m = m_new
            m_ref[:, cols] = m
            acc_ref[:, cols] = acc

    near0 = jnp.maximum(i - (N_BIAS_TABLES - 1), 0)
    n_groups = (n_far + FAR_GROUP - 1) // FAR_GROUP
    last_group = nblk // FAR_GROUP - 1
    logits_into(sa_ref, near0, N_BIAS_TABLES)
    logits_into(sb_ref, 0, FAR_GROUP)
    fold(sa_ref, near0, N_BIAS_TABLES, mnear_ref, near0 - (i - (N_BIAS_TABLES - 1)))

    def far_pair(gp, carry):
        ga = 2 * gp
        logits_into(sa_ref, FAR_GROUP * (ga + 1), FAR_GROUP)
        fold(sb_ref, FAR_GROUP * ga, FAR_GROUP, mfar_ref, None)
        logits_into(sb_ref, FAR_GROUP * jnp.minimum(ga + 2, last_group), FAR_GROUP)
        fold(sa_ref, FAR_GROUP * (ga + 1), FAR_GROUP, mfar_ref, None)
        return carry

    lax.fori_loop(0, (n_groups + 1) // 2, far_pair, 0)
    acc = acc_ref[...]
    ot = acc[0:AT_DIM, :] / acc[AT_DIM:AT_DIM + 1, :]
    o_ref[...] = jnp.concatenate(
        [ot[:, h * MOBA_BLOCK:(h + 1) * MOBA_BLOCK] for h in range(AT_HEADS)], axis=0
    ).T.astype(o_ref.dtype)


def _moba(rel_bias, q, k, vt, ksum, bias):
    s = q.shape[0]
    nblk = s // MOBA_BLOCK
    assert nblk >= N_BIAS_TABLES and FAR_GROUP <= N_BIAS_TABLES and nblk % FAR_GROUP == 0
    return pl.pallas_call(
        _moba_kernel,
        grid=(nblk,),
        in_specs=[
            pl.BlockSpec(memory_space=pltpu.SMEM),
            pl.BlockSpec((MOBA_BLOCK, AT_WIDTH), lambda i: (i, 0)),
            _const_spec((nblk, MOBA_BLOCK, AT_WIDTH)),
            _const_spec((nblk, AT_HEADS, V_ROWS, MOBA_BLOCK)),
            _const_spec((nblk * KSUM_ROWS, AT_WIDTH)),
            _const_spec((2 * N_BIAS_TABLES - 1, MOBA_BLOCK, AT_HEADS * MOBA_BLOCK)),
        ],
        out_specs=pl.BlockSpec((MOBA_BLOCK, AT_WIDTH), lambda i: (i, 0)),
        out_shape=jax.ShapeDtypeStruct((s, AT_WIDTH), BF16),
        scratch_shapes=[
            pltpu.VMEM((nblk, AT_WIDTH), F32),
            pltpu.VMEM((nblk, AT_HEADS * MOBA_BLOCK), F32),
            pltpu.VMEM((nblk, AT_HEADS * MOBA_BLOCK), F32),
            pltpu.VMEM((AT_WIDTH, AT_HEADS * MOBA_BLOCK), BF16),
            pltpu.VMEM((1, AT_HEADS * MOBA_BLOCK), F32),
            pltpu.VMEM((V_ROWS, AT_HEADS * MOBA_BLOCK), F32),
            pltpu.VMEM((N_BIAS_TABLES * MOBA_BLOCK, AT_HEADS * MOBA_BLOCK), F32),
            pltpu.VMEM((FAR_GROUP * MOBA_BLOCK, AT_HEADS * MOBA_BLOCK), F32),
        ],
        compiler_params=_params("arbitrary"),
        name="moba",
    )(rel_bias, q, k, vt, ksum, bias)


def _outproj_kernel(x_ref, ohg_ref, oat_ref, zcv_ref, halo_ref, cw_ref, w_ref, nw_ref, o_ref):
    i = pl.program_id(0)
    zcv = zcv_ref[...]
    bgate = zcv[:, 0:CV_WIDTH]
    u = zcv[:, CV_WIDTH:2 * CV_WIDTH] * zcv[:, 2 * CV_WIDTH:3 * CV_WIDTH]
    halo = halo_ref[...]
    uh = halo[:, CV_WIDTH:2 * CV_WIDTH] * halo[:, 2 * CV_WIDTH:3 * CV_WIDTH]
    uh = jnp.where(i > 0, uh, 0.0)
    row = lax.broadcasted_iota(jnp.int32, u.shape, 0)
    u1 = jnp.where(row == 0, uh[7:8, :], pltpu.roll(u, 1, axis=0))
    u2 = jnp.where(row == 0, uh[6:7, :], jnp.where(row == 1, uh[7:8, :], pltpu.roll(u, 2, axis=0)))
    cw = cw_ref[...]
    ocv = bgate * (cw[0:1, :] * u2 + cw[1:2, :] * u1 + cw[2:3, :] * u)
    h = jnp.dot(ohg_ref[...], w_ref[0:HG_WIDTH, :], preferred_element_type=F32)
    h = h + jnp.dot(oat_ref[...], w_ref[HG_WIDTH:HG_WIDTH + AT_WIDTH, :], preferred_element_type=F32)
    h = h + jnp.dot(ocv.astype(BF16), w_ref[HG_WIDTH + AT_WIDTH:D_MIX, :], preferred_element_type=F32)
    o_ref[...] = x_ref[...] + _rms(h, nw_ref[...])


def _outproj(x, ohg, oat, zcv, conv_w, w_out, nw, layer):
    s = x.shape[0]
    halo_blocks = ROW_TILE // 8
    return pl.pallas_call(
        _outproj_kernel,
        grid=(s // ROW_TILE,),
        in_specs=[
            pl.BlockSpec((ROW_TILE, D_MODEL), lambda i: (i, 0)),
            pl.BlockSpec((ROW_TILE, HG_WIDTH), lambda i: (i, 0)),
            pl.BlockSpec((ROW_TILE, AT_WIDTH), lambda i: (i, 0)),
            pl.BlockSpec((ROW_TILE, 3 * CV_WIDTH), lambda i: (i, 0)),
            pl.BlockSpec((8, 3 * CV_WIDTH), lambda i: (jnp.maximum(i * halo_blocks - 1, 0), 0)),
            _const_spec((CV_KERNEL, CV_WIDTH)),
            _layer_spec((D_MIX, D_MODEL), layer),
            _const_spec((1, D_MODEL)),
        ],
        out_specs=pl.BlockSpec((ROW_TILE, D_MODEL), lambda i: (i, 0)),
        out_shape=jax.ShapeDtypeStruct((s, D_MODEL), F32),
        compiler_params=_params("parallel"),
        name="outproj",
    )(x, ohg, oat, zcv, zcv, conv_w, w_out, nw)


def kernel(x, norm_w, ffn1_wg, ffn1_wu, ffn1_wd, mix_w_in, mix_w_out, hg_lb, hg_norm_w, conv_w,
           ffn2_wg, ffn2_wu, ffn2_wd, rel_bias):
    batch, seq, _ = x.shape
    depth = norm_w.shape[0]
    assert batch == 1 and seq % ROW_TILE == 0 and seq % MOBA_BLOCK == 0
    rel_bias = rel_bias.astype(F32)
    bias = _bias_tables(rel_bias)
    y = x.reshape(seq, D_MODEL)
    w1 = [w.astype(BF16) for w in (ffn1_wg, ffn1_wu, ffn1_wd)]
    w2 = [w.astype(BF16) for w in (ffn2_wg, ffn2_wu, ffn2_wd)]
    w_in = mix_w_in.astype(BF16)
    w_out = mix_w_out.astype(BF16)
    for l in range(depth):
        y = _ffn(y, norm_w[l, 0:2], *w1, l)
        zhg, q, k, vt, ksum, zcv = _inproj(y, norm_w[l, 2:3], w_in, l)
        ohg = _hgrn(zhg, hg_lb, hg_norm_w[l:l + 1], l)
        oat = _moba(rel_bias, q, k, vt, ksum, bias)
        y = _outproj(y, ohg, oat, zcv, conv_w[l], w_out, norm_w[l, 3:4], l)
        y = _ffn(y, norm_w[l, 4:6], *w2, l)
    return y.reshape(batch, seq, D_MODEL)
```

```python
import functools
import math

import numpy as np
import jax
import jax.numpy as jnp
from jax import lax
from jax.experimental import pallas as pl
from jax.experimental.pallas import tpu as pltpu

F32 = jnp.float32
BF16 = jnp.bfloat16

D_MODEL = 1024
D_FF = 2816
HG_HEADS = 4
HG_DIM = 128
HG_WIDTH = HG_HEADS * HG_DIM
AT_HEADS = 4
AT_DIM = 64
AT_WIDTH = AT_HEADS * AT_DIM
MOBA_BLOCK = 256
MOBA_TOPK = 3
REL_BUCKETS = 32
REL_MAX_DIST = 1024
CV_WIDTH = 256
CV_KERNEL = 3
D_MIX = HG_WIDTH + AT_WIDTH + CV_WIDTH
D_IN = 4 * HG_WIDTH + 3 * AT_WIDTH + 3 * CV_WIDTH
EPS = 1e-6

ROW_TILE = 512
FFN_TILE = 1024
FFN_CHAIN = 512
HG_TILE = 256
KSUM_ROWS = 8
V_ROWS = AT_DIM + 16
LOG2E = math.log2(math.e)
FAR_GROUP = 2
NEG = -1e30
VMEM_LIMIT = 56 * 1024 * 1024
FF_CHUNKS = ((0, 768), (768, 1536), (1536, 2304), (2304, 2816))

_NT = (((1,), (1,)), ((), ()))
_TN = (((0,), (0,)), ((), ()))


def _rms(x, w):
    ms = jnp.mean(x * x, axis=-1, keepdims=True)
    return x * lax.rsqrt(ms + EPS) * w


def _const_spec(shape):
    nd = len(shape)
    return pl.BlockSpec(shape, lambda *_: (0,) * nd, pipeline_mode=pl.Buffered(1))


def _layer_spec(shape, layer):
    nd = len(shape)
    return pl.BlockSpec((None,) + tuple(shape), lambda *_: (layer,) + (0,) * nd,
                        pipeline_mode=pl.Buffered(1))


def _params(*sem, flags=None):
    return pltpu.CompilerParams(dimension_semantics=sem, vmem_limit_bytes=VMEM_LIMIT, flags=flags)


def _ffn_rows(x, nw_pre, nw_post, wg_ref, wu_ref, wd_ref):
    xn = _rms(x, nw_pre).astype(BF16)
    h = None
    for c0, c1 in FF_CHUNKS:
        g = jnp.dot(xn, wg_ref[:, c0:c1], preferred_element_type=F32)
        u = jnp.dot(xn, wu_ref[:, c0:c1], preferred_element_type=F32)
        a = (g * jax.nn.sigmoid(g) * u).astype(BF16)
        part = jnp.dot(a, wd_ref[c0:c1, :], preferred_element_type=F32)
        h = part if h is None else h + part
    return x + 0.5 * _rms(h, nw_post)


def _ffn_kernel(x_ref, nw_ref, wg_ref, wu_ref, wd_ref, o_ref):
    for r0 in range(0, FFN_TILE, FFN_CHAIN):
        rows = slice(r0, r0 + FFN_CHAIN)
        o_ref[rows, :] = _ffn_rows(x_ref[rows, :], nw_ref[0:1, :], nw_ref[1:2, :],
                                   wg_ref, wu_ref, wd_ref)


def _ffn(x, nw2, wg, wu, wd, layer):
    s = x.shape[0]
    return pl.pallas_call(
        _ffn_kernel,
        grid=(s // FFN_TILE,),
        in_specs=[
            pl.BlockSpec((FFN_TILE, D_MODEL), lambda i: (i, 0)),
            _const_spec((2, D_MODEL)),
            _layer_spec((D_MODEL, D_FF), layer),
            _layer_spec((D_MODEL, D_FF), layer),
            _layer_spec((D_FF, D_MODEL), layer),
        ],
        out_specs=pl.BlockSpec((FFN_TILE, D_MODEL), lambda i: (i, 0)),
        out_shape=jax.ShapeDtypeStruct((s, D_MODEL), F32),
        compiler_params=_params("parallel"),
        name="ffn",
    )(x, nw2, wg, wu, wd)


def _inproj_kernel(x_ref, nw_ref, w_ref, zhg_ref, q_ref, k_ref, vt_ref, ksum_ref, zcv_ref):
    xn = _rms(x_ref[...], nw_ref[...]).astype(BF16)
    c = 4 * HG_WIDTH
    zhg_ref[...] = jnp.dot(xn, w_ref[:, 0:c], preferred_element_type=F32)
    q_ref[...] = jnp.dot(xn, w_ref[:, c:c + AT_WIDTH], preferred_element_type=F32)
    k = jnp.dot(xn, w_ref[:, c + AT_WIDTH:c + 2 * AT_WIDTH], preferred_element_type=F32)
    v = jnp.dot(xn, w_ref[:, c + 2 * AT_WIDTH:c + 3 * AT_WIDTH], preferred_element_type=F32)
    c += 3 * AT_WIDTH
    zcv_ref[...] = jnp.dot(xn, w_ref[:, c:c + 3 * CV_WIDTH], preferred_element_type=F32)
    ones = jnp.ones((V_ROWS - AT_DIM, MOBA_BLOCK), F32)
    for b in range(ROW_TILE // MOBA_BLOCK):
        kb = k[b * MOBA_BLOCK:(b + 1) * MOBA_BLOCK, :]
        vbt = v[b * MOBA_BLOCK:(b + 1) * MOBA_BLOCK, :].T
        k_ref[b] = kb.astype(BF16)
        for h in range(AT_HEADS):
            vt_ref[b, h] = jnp.concatenate(
                [vbt[h * AT_DIM:(h + 1) * AT_DIM, :], ones], axis=0).astype(BF16)
        ksum_ref[b * KSUM_ROWS:(b + 1) * KSUM_ROWS, :] = jnp.sum(
            kb.reshape(MOBA_BLOCK // KSUM_ROWS, KSUM_ROWS, AT_WIDTH), axis=0)


def _inproj(x, nw, w_in, layer):
    s = x.shape[0]
    nblk = s // MOBA_BLOCK
    bpt = ROW_TILE // MOBA_BLOCK
    return pl.pallas_call(
        _inproj_kernel,
        grid=(s // ROW_TILE,),
        in_specs=[
            pl.BlockSpec((ROW_TILE, D_MODEL), lambda i: (i, 0)),
            _const_spec((1, D_MODEL)),
            _layer_spec((D_MODEL, D_IN), layer),
        ],
        out_specs=[
            pl.BlockSpec((ROW_TILE, 4 * HG_WIDTH), lambda i: (i, 0)),
            pl.BlockSpec((ROW_TILE, AT_WIDTH), lambda i: (i, 0)),
            pl.BlockSpec((bpt, MOBA_BLOCK, AT_WIDTH), lambda i: (i, 0, 0)),
            pl.BlockSpec((bpt, AT_HEADS, V_ROWS, MOBA_BLOCK), lambda i: (i, 0, 0, 0)),
            pl.BlockSpec((bpt * KSUM_ROWS, AT_WIDTH), lambda i: (i, 0)),
            pl.BlockSpec((ROW_TILE, 3 * CV_WIDTH), lambda i: (i, 0)),
        ],
        out_shape=[
            jax.ShapeDtypeStruct((s, 4 * HG_WIDTH), F32),
            jax.ShapeDtypeStruct((s, AT_WIDTH), F32),
            jax.ShapeDtypeStruct((nblk, MOBA_BLOCK, AT_WIDTH), BF16),
            jax.ShapeDtypeStruct((nblk, AT_HEADS, V_ROWS, MOBA_BLOCK), BF16),
            jax.ShapeDtypeStruct((nblk * KSUM_ROWS, AT_WIDTH), F32),
            jax.ShapeDtypeStruct((s, 3 * CV_WIDTH), F32),
        ],
        compiler_params=_params("parallel"),
        name="inproj",
    )(x, nw, w_in)


def _hg_levels():
    levels = []
    n = HG_TILE
    while n >= 2:
        levels.append(n)
        n //= 2
    return levels


def _hgrn_head(q, fp, v, gate, lb, nw, st_ref, row, masks):
    u = jnp.exp2(jnp.minimum(fp * -LOG2E, 126.0))
    r = 1.0 / (1.0 + u)
    logf = jnp.log2(lb + (1.0 - lb) * r)
    kk = (1.0 - lb) * (u * r)
    vb = v.astype(BF16)

    b = logf
    sh = 1
    while sh < 8:
        rolled = pltpu.roll(b, sh, axis=0)
        head = jnp.where(row[:8] >= sh, rolled[:8], 0.0)
        b = b + jnp.concatenate([head, rolled[8:]], axis=0)
        sh *= 2
    while sh < HG_TILE:
        b = jnp.concatenate([b[:sh], b[sh:] + b[:-sh]], axis=0)
        sh *= 2

    diag, same = masks
    scores = jnp.where(
        diag, lax.dot_general(q.astype(BF16), kk.astype(BF16), _NT, preferred_element_type=F32), 0.0)
    for n in _hg_levels():
        half = n // 2
        if half >= 8:
            qp, kp = [], []
            zero = jnp.zeros((half, HG_DIM), F32)
            for lo in range(0, HG_TILE, n):
                mid, hi = lo + half, lo + n
                bm = b[mid - 1:mid, :]
                qp += [zero, q[mid:hi] * jnp.exp2(b[mid:hi] - bm)]
                kp += [kk[lo:mid] * jnp.exp2(bm - b[lo:mid]), zero]
            qs = jnp.concatenate(qp, axis=0)
            ks = jnp.concatenate(kp, axis=0)
        else:
            upper = (row & (n - 1)) >= half
            if n == 2:
                ex = jnp.exp2(jnp.where(upper, logf, 0.0))
            else:
                b3 = b.reshape(HG_TILE // 8, 8, HG_DIM)
                sub = lax.broadcasted_iota(jnp.int32, (HG_TILE // 8, 8, HG_DIM), 1)
                bm3 = None
                for lo in range(0, 8, n):
                    piece = jnp.broadcast_to(b3[:, lo + half - 1:lo + half, :], b3.shape)
                    bm3 = piece if bm3 is None else jnp.where(sub >= lo, piece, bm3)
                ex = jnp.exp2(-jnp.abs(b - bm3.reshape(HG_TILE, HG_DIM)))
            qs = jnp.where(upper, q * ex, 0.0)
            ks = jnp.where(upper, 0.0, kk * ex)
        lvl = lax.dot_general(qs.astype(BF16), ks.astype(BF16), _NT, preferred_element_type=F32)
        scores = scores + (lvl if n == HG_TILE else jnp.where(same[n], lvl, 0.0))

    st = st_ref[...]
    b_last = b[HG_TILE - 1:HG_TILE, :]
    o = jnp.dot(scores.astype(BF16), vb, preferred_element_type=F32)
    o = o + lax.dot_general((q * jnp.exp2(b)).astype(BF16), st.astype(BF16), _NT,
                            preferred_element_type=F32)
    kdec = (kk * jnp.exp2(b_last - b)).astype(BF16)
    st_ref[...] = st * jnp.exp2(b_last) + lax.dot_general(vb, kdec, _TN, preferred_element_type=F32)
    return _rms(o, nw) * (gate * jax.nn.sigmoid(gate))


def _hgrn_kernel(layer, q_ref, f_ref, i_ref, g_ref, lb_ref, nw_ref, o_ref, st_ref):
    @pl.when(pl.program_id(0) == 0)
    def _():
        st_ref[...] = jnp.zeros_like(st_ref)

    lbraw = lb_ref[...]
    e = jnp.exp(lbraw - jnp.max(lbraw, axis=0, keepdims=True))
    soft = e / jnp.sum(e, axis=0, keepdims=True)
    lb = jnp.sum(soft[0:layer + 1, :], axis=0, keepdims=True) - soft[0:1, :]

    row = lax.broadcasted_iota(jnp.int32, (HG_TILE, HG_DIM), 0)
    ti = lax.broadcasted_iota(jnp.int32, (HG_TILE, HG_TILE), 0)
    si = lax.broadcasted_iota(jnp.int32, (HG_TILE, HG_TILE), 1)
    masks = (ti == si, {n: (ti & -n) == (si & -n) for n in _hg_levels() if n < HG_TILE})
    for h in range(HG_HEADS):
        cols = slice(h * HG_DIM, (h + 1) * HG_DIM)
        o_ref[:, cols] = _hgrn_head(q_ref[:, cols], f_ref[:, cols], i_ref[:, cols], g_ref[:, cols],
                                    lb[:, cols], nw_ref[...], st_ref.at[h], row, masks
                                    ).astype(o_ref.dtype)


def _hgrn(zhg, hg_lb, hg_nw, layer):
    s = zhg.shape[0]
    depth = hg_lb.shape[0]

    def col(k):
        return pl.BlockSpec((HG_TILE, HG_WIDTH), lambda t: (t, k))

    return pl.pallas_call(
        functools.partial(_hgrn_kernel, layer),
        grid=(s // HG_TILE,),
        in_specs=[
            col(0), col(1), col(2), col(3),
            pl.BlockSpec((depth, HG_WIDTH), lambda t: (0, 0)),
            pl.BlockSpec((1, HG_DIM), lambda t: (0, 0)),
        ],
        out_specs=pl.BlockSpec((HG_TILE, HG_WIDTH), lambda t: (t, 0)),
        out_shape=jax.ShapeDtypeStruct((s, HG_WIDTH), BF16),
        scratch_shapes=[pltpu.VMEM((HG_HEADS, HG_DIM, HG_DIM), F32)],
        compiler_params=_params("arbitrary"),
        name="hgrn2",
    )(zhg, zhg, zhg, zhg, hg_lb, hg_nw)


N_BIAS_TABLES = 5


def _bucket_thresholds():
    max_exact = REL_BUCKETS // 2
    d = np.arange(1, 2 * REL_MAX_DIST, dtype=np.float64)
    large = max_exact + (np.log(d / max_exact) / math.log(REL_MAX_DIST / max_exact)
                         * (REL_BUCKETS - max_exact)).astype(np.int64)
    large = np.minimum(large, REL_BUCKETS - 1)
    bucket = np.where(d < max_exact, d.astype(np.int64), large)
    thr = [0] * REL_BUCKETS
    for bkt in range(1, REL_BUCKETS):
        thr[bkt] = int(d[np.argmax(bucket >= bkt)])
    assert (N_BIAS_TABLES - 1) * MOBA_BLOCK + 1 >= thr[REL_BUCKETS - 1]
    return thr


def _bias_kernel(rb_ref, o_ref):
    thr = _bucket_thresholds()
    key = lax.broadcasted_iota(jnp.int32, (MOBA_BLOCK, MOBA_BLOCK), 0)
    qry = lax.broadcasted_iota(jnp.int32, (MOBA_BLOCK, MOBA_BLOCK), 1)
    o_ref[N_BIAS_TABLES:] = jnp.zeros((N_BIAS_TABLES - 1,) + o_ref.shape[1:], F32)
    for u in range(N_BIAS_TABLES):
        t = N_BIAS_TABLES - 1 - u
        dist = qry - key + t * MOBA_BLOCK
        for h in range(AT_HEADS):
            val = jnp.full((MOBA_BLOCK, MOBA_BLOCK), rb_ref[REL_BUCKETS - 1, h], F32)
            for bkt in range(REL_BUCKETS - 2, -1, -1):
                val = jnp.where(dist < thr[bkt + 1], rb_ref[bkt, h], val)
            val = val * LOG2E
            if t == 0:
                val = jnp.where(dist < 0, NEG, val)
            o_ref[u, :, h * MOBA_BLOCK:(h + 1) * MOBA_BLOCK] = val


def _bias_tables(rel_bias):
    return pl.pallas_call(
        _bias_kernel,
        in_specs=[pl.BlockSpec(memory_space=pltpu.SMEM)],
        out_shape=jax.ShapeDtypeStruct((2 * N_BIAS_TABLES - 1, MOBA_BLOCK, AT_HEADS * MOBA_BLOCK), F32),
        name="moba_bias",
    )(rel_bias)


def _moba_kernel(rb_ref, q_ref, k_ref, vt_ref, ksum_ref, bias_ref, o_ref,
                 kmean_ref, mnear_ref, mfar_ref, qs_ref, m_ref, acc_ref, sa_ref, sb_ref):
    i = pl.program_id(0)
    nblk = k_ref.shape[0]

    @pl.when(i == 0)
    def _():
        ks = ksum_ref[...].reshape(nblk, KSUM_ROWS, AT_WIDTH)
        kmean_ref[...] = jnp.sum(ks, axis=1) * (1.0 / MOBA_BLOCK)

    qt = q_ref[...].T
    row_head = lax.broadcasted_iota(jnp.int32, (AT_WIDTH, MOBA_BLOCK), 0) // AT_DIM
    qmt = jnp.concatenate([jnp.where(row_head == h, qt, 0.0) for h in range(AT_HEADS)], axis=1)
    qs_ref[...] = (qmt * (AT_DIM ** -0.5 * LOG2E)).astype(BF16)

    ncol = AT_HEADS * MOBA_BLOCK
    jio = lax.broadcasted_iota(jnp.int32, (nblk, ncol), 0).astype(F32)
    fi = i.astype(F32)
    n_far = jnp.maximum(i - (N_BIAS_TABLES - 1), 0)
    gate = jnp.dot(kmean_ref[...], qmt, precision=lax.Precision.HIGHEST,
                   preferred_element_type=F32)
    gate = jnp.where(jio < fi, gate, -jnp.inf)
    sel = jio == fi
    for _ in range(MOBA_TOPK):
        mx = jnp.max(gate, axis=0, keepdims=True)
        cand = jnp.where(gate == mx, jio, float(nblk))
        idx = jnp.min(cand, axis=0, keepdims=True)
        pick = (jio == idx) & (mx > -jnp.inf)
        sel = sel | pick
        gate = jnp.where(pick, -jnp.inf, gate)
    col_head = lax.broadcasted_iota(jnp.int32, (1, ncol), 1) // MOBA_BLOCK
    far_bias = jnp.zeros((1, ncol), F32)
    for h in range(AT_HEADS):
        far_bias = jnp.where(col_head == h, rb_ref[REL_BUCKETS - 1, h] * LOG2E, far_bias)
    mnear_ref[...] = jnp.where(sel, 0.0, NEG)
    mfar_ref[...] = jnp.where(sel & (jio < n_far.astype(F32)), far_bias, NEG)
    m_ref[...] = jnp.full(m_ref.shape, NEG, F32)
    acc_ref[...] = jnp.zeros(acc_ref.shape, F32)

    def logits_into(s_ref, j0, nb):
        kg = k_ref[pl.ds(j0, nb)].reshape(nb * MOBA_BLOCK, AT_WIDTH)
        s_ref[0:nb * MOBA_BLOCK, :] = jnp.dot(kg, qs_ref[...], preferred_element_type=F32)

    def fold(s_ref, j0, nb, mask_ref, table):
        for h in range(AT_HEADS):
            cols = slice(h * MOBA_BLOCK, (h + 1) * MOBA_BLOCK)
            m = m_ref[:, cols]
            acc = acc_ref[:, cols]
            for g in range(nb):
                rows = slice(g * MOBA_BLOCK, (g + 1) * MOBA_BLOCK)
                mrow = mask_ref[pl.ds(j0 + g, 1), cols]
                s = s_ref[rows, cols]
                if table is not None:
                    s = s + mrow + bias_ref[table + g, :, cols]
                mb = jnp.max(s, axis=0, keepdims=True)
                p = jnp.exp2(s - mb).astype(BF16)
                if table is None:
                    mb = mb + mrow
                pv = jnp.dot(vt_ref[j0 + g, h], p, preferred_element_type=F32)
                m_new = jnp.maximum(m, mb)
                acc = jnp.exp2(m - m_new) * acc + jnp.exp2(mb - m_new) * pv
                m = m_new
            m_ref[:, cols] = m
            acc_ref[:, cols] = acc

    near0 = jnp.maximum(i - (N_BIAS_TABLES - 1), 0)
    n_groups = (n_far + FAR_GROUP - 1) // FAR_GROUP
    last_group = nblk // FAR_GROUP - 1
    logits_into(sa_ref, near0, N_BIAS_TABLES)
    logits_into(sb_ref, 0, FAR_GROUP)
    fold(sa_ref, near0, N_BIAS_TABLES, mnear_ref, near0 - (i - (N_BIAS_TABLES - 1)))

    def far_pair(gp, carry):
        ga = 2 * gp
        logits_into(sa_ref, FAR_GROUP * (ga + 1), FAR_GROUP)
        fold(sb_ref, FAR_GROUP * ga, FAR_GROUP, mfar_ref, None)
        logits_into(sb_ref, FAR_GROUP * jnp.minimum(ga + 2, last_group), FAR_GROUP)
        fold(sa_ref, FAR_GROUP * (ga + 1), FAR_GROUP, mfar_ref, None)
        return carry

    lax.fori_loop(0, (n_groups + 1) // 2, far_pair, 0)
    acc = acc_ref[...]
    ot = acc[0:AT_DIM, :] / acc[AT_DIM:AT_DIM + 1, :]
    o_ref[...] = jnp.concatenate(
        [ot[:, h * MOBA_BLOCK:(h + 1) * MOBA_BLOCK] for h in range(AT_HEADS)], axis=0
    ).T.astype(o_ref.dtype)


def _moba(rel_bias, q, k, vt, ksum, bias):
    s = q.shape[0]
    nblk = s // MOBA_BLOCK
    assert nblk >= N_BIAS_TABLES and FAR_GROUP <= N_BIAS_TABLES and nblk % FAR_GROUP == 0
    return pl.pallas_call(
        _moba_kernel,
        grid=(nblk,),
        in_specs=[
            pl.BlockSpec(memory_space=pltpu.SMEM),
            pl.BlockSpec((MOBA_BLOCK, AT_WIDTH), lambda i: (i, 0)),
            _const_spec((nblk, MOBA_BLOCK, AT_WIDTH)),
            _const_spec((nblk, AT_HEADS, V_ROWS, MOBA_BLOCK)),
            _const_spec((nblk * KSUM_ROWS, AT_WIDTH)),
            _const_spec((2 * N_BIAS_TABLES - 1, MOBA_BLOCK, AT_HEADS * MOBA_BLOCK)),
        ],
        out_specs=pl.BlockSpec((MOBA_BLOCK, AT_WIDTH), lambda i: (i, 0)),
        out_shape=jax.ShapeDtypeStruct((s, AT_WIDTH), BF16),
        scratch_shapes=[
            pltpu.VMEM((nblk, AT_WIDTH), F32),
            pltpu.VMEM((nblk, AT_HEADS * MOBA_BLOCK), F32),
            pltpu.VMEM((nblk, AT_HEADS * MOBA_BLOCK), F32),
            pltpu.VMEM((AT_WIDTH, AT_HEADS * MOBA_BLOCK), BF16),
            pltpu.VMEM((1, AT_HEADS * MOBA_BLOCK), F32),
            pltpu.VMEM((V_ROWS, AT_HEADS * MOBA_BLOCK), F32),
            pltpu.VMEM((N_BIAS_TABLES * MOBA_BLOCK, AT_HEADS * MOBA_BLOCK), F32),
            pltpu.VMEM((FAR_GROUP * MOBA_BLOCK, AT_HEADS * MOBA_BLOCK), F32),
        ],
        compiler_params=_params("arbitrary"),
        name="moba",
    )(rel_bias, q, k, vt, ksum, bias)


def _mix_rows(x, ohg, oat, zcv, uh, cw, w_ref, nw):
    bgate = zcv[:, 0:CV_WIDTH]
    u = zcv[:, CV_WIDTH:2 * CV_WIDTH] * zcv[:, 2 * CV_WIDTH:3 * CV_WIDTH]
    row = lax.broadcasted_iota(jnp.int32, u.shape, 0)
    u1 = jnp.where(row == 0, uh[7:8, :], pltpu.roll(u, 1, axis=0))
    u2 = jnp.where(row == 0, uh[6:7, :], jnp.where(row == 1, uh[7:8, :], pltpu.roll(u, 2, axis=0)))
    ocv = bgate * (cw[0:1, :] * u2 + cw[1:2, :] * u1 + cw[2:3, :] * u)
    h = jnp.dot(ohg, w_ref[0:HG_WIDTH, :], preferred_element_type=F32)
    h = h + jnp.dot(oat, w_ref[HG_WIDTH:HG_WIDTH + AT_WIDTH, :], preferred_element_type=F32)
    h = h + jnp.dot(ocv.astype(BF16), w_ref[HG_WIDTH + AT_WIDTH:D_MIX, :], preferred_element_type=F32)
    return x + _rms(h, nw)


def _mixffn_kernel(x_ref, ohg_ref, oat_ref, zcv_ref, halo_ref, cw_ref, wo_ref, nw_ref,
                   wg_ref, wu_ref, wd_ref, o_ref):
    halo = halo_ref[...]
    uh = halo[:, CV_WIDTH:2 * CV_WIDTH] * halo[:, 2 * CV_WIDTH:3 * CV_WIDTH]
    uh = jnp.where(pl.program_id(0) > 0, uh, 0.0)
    cw = cw_ref[...]
    for r0 in range(0, FFN_TILE, FFN_CHAIN):
        rows = slice(r0, r0 + FFN_CHAIN)
        if r0 > 0:
            prev = zcv_ref[r0 - 8:r0, :]
            uh = prev[:, CV_WIDTH:2 * CV_WIDTH] * prev[:, 2 * CV_WIDTH:3 * CV_WIDTH]
        y = _mix_rows(x_ref[rows, :], ohg_ref[rows, :], oat_ref[rows, :], zcv_ref[rows, :], uh, cw,
                      wo_ref, nw_ref[0:1, :])
        o_ref[rows, :] = _ffn_rows(y, nw_ref[1:2, :], nw_ref[2:3, :], wg_ref, wu_ref, wd_ref)


def _mixffn(x, ohg, oat, zcv, conv_w, w_out, nw3, wg, wu, wd, layer):
    s = x.shape[0]
    halo_blocks = FFN_TILE // 8
    return pl.pallas_call(
        _mixffn_kernel,
        grid=(s // FFN_TILE,),
        in_specs=[
            pl.BlockSpec((FFN_TILE, D_MODEL), lambda i: (i, 0)),
            pl.BlockSpec((FFN_TILE, HG_WIDTH), lambda i: (i, 0)),
            pl.BlockSpec((FFN_TILE, AT_WIDTH), lambda i: (i, 0)),
            pl.BlockSpec((FFN_TILE, 3 * CV_WIDTH), lambda i: (i, 0)),
            pl.BlockSpec((8, 3 * CV_WIDTH), lambda i: (jnp.maximum(i * halo_blocks - 1, 0), 0)),
            _const_spec((CV_KERNEL, CV_WIDTH)),
            _layer_spec((D_MIX, D_MODEL), layer),
            _const_spec((3, D_MODEL)),
            _layer_spec((D_MODEL, D_FF), layer),
            _layer_spec((D_MODEL, D_FF), layer),
            _layer_spec((D_FF, D_MODEL), layer),
        ],
        out_specs=pl.BlockSpec((FFN_TILE, D_MODEL), lambda i: (i, 0)),
        out_shape=jax.ShapeDtypeStruct((s, D_MODEL), F32),
        compiler_params=_params("parallel"),
        name="mixffn",
    )(x, ohg, oat, zcv, zcv, conv_w, w_out, nw3, wg, wu, wd)


def kernel(x, norm_w, ffn1_wg, ffn1_wu, ffn1_wd, mix_w_in, mix_w_out, hg_lb, hg_norm_w, conv_w,
           ffn2_wg, ffn2_wu, ffn2_wd, rel_bias):
    batch, seq, _ = x.shape
    depth = norm_w.shape[0]
    assert batch == 1 and seq % ROW_TILE == 0 and seq % MOBA_BLOCK == 0
    rel_bias = rel_bias.astype(F32)
    bias = _bias_tables(rel_bias)
    y = x.reshape(seq, D_MODEL)
    w1 = [w.astype(BF16) for w in (ffn1_wg, ffn1_wu, ffn1_wd)]
    w2 = [w.astype(BF16) for w in (ffn2_wg, ffn2_wu, ffn2_wd)]
    w_in = mix_w_in.astype(BF16)
    w_out = mix_w_out.astype(BF16)
    for l in range(depth):
        y = _ffn(y, norm_w[l, 0:2], *w1, l)
        zhg, q, k, vt, ksum, zcv = _inproj(y, norm_w[l, 2:3], w_in, l)
        ohg = _hgrn(zhg, hg_lb, hg_norm_w[l:l + 1], l)
        oat = _moba(rel_bias, q, k, vt, ksum, bias)
        y = _mixffn(y, ohg, oat, zcv, conv_w[l], w_out, norm_w[l, 3:6], *w2, l)
    return y.reshape(batch, seq, D_MODEL)
```

```python
import functools
import math

import numpy as np
import jax
import jax.numpy as jnp
from jax import lax
from jax.experimental import pallas as pl
from jax.experimental.pallas import tpu as pltpu

F32 = jnp.float32
BF16 = jnp.bfloat16

D_MODEL = 1024
D_FF = 2816
HG_HEADS = 4
HG_DIM = 128
HG_WIDTH = HG_HEADS * HG_DIM
AT_HEADS = 4
AT_DIM = 64
AT_WIDTH = AT_HEADS * AT_DIM
MOBA_BLOCK = 256
MOBA_TOPK = 3
REL_BUCKETS = 32
REL_MAX_DIST = 1024
CV_WIDTH = 256
CV_KERNEL = 3
D_MIX = HG_WIDTH + AT_WIDTH + CV_WIDTH
D_IN = 4 * HG_WIDTH + 3 * AT_WIDTH + 3 * CV_WIDTH
EPS = 1e-6

ROW_TILE = 512
FFN_TILE = 1024
FFN_CHAIN = 512
HG_TILE = 256
KSUM_ROWS = 8
V_ROWS = AT_DIM + 16
LOG2E = math.log2(math.e)
FAR_GROUP = 2
NEG = -1e30
VMEM_LIMIT = 56 * 1024 * 1024
FF_CHUNKS = ((0, 1536), (1536, 2816))

_NT = (((1,), (1,)), ((), ()))
_TN = (((0,), (0,)), ((), ()))


def _rms(x, w):
    ms = jnp.mean(x * x, axis=-1, keepdims=True)
    return x * lax.rsqrt(ms + EPS) * w


def _const_spec(shape):
    nd = len(shape)
    return pl.BlockSpec(shape, lambda *_: (0,) * nd, pipeline_mode=pl.Buffered(1))


def _layer_spec(shape, layer):
    nd = len(shape)
    return pl.BlockSpec((None,) + tuple(shape), lambda *_: (layer,) + (0,) * nd,
                        pipeline_mode=pl.Buffered(1))


def _params(*sem):
    return pltpu.CompilerParams(dimension_semantics=sem, vmem_limit_bytes=VMEM_LIMIT)


def _ffn_rows(x, nw_pre, nw_post, wg_ref, wu_ref, wd_ref):
    xn = _rms(x, nw_pre).astype(BF16)
    h = None
    for c0, c1 in FF_CHUNKS:
        g = jnp.dot(xn, wg_ref[:, c0:c1], preferred_element_type=F32)
        u = jnp.dot(xn, wu_ref[:, c0:c1], preferred_element_type=F32)
        a = (g * jax.nn.sigmoid(g) * u).astype(BF16)
        part = jnp.dot(a, wd_ref[c0:c1, :], preferred_element_type=F32)
        h = part if h is None else h + part
    return x + 0.5 * _rms(h, nw_post)


def _ffn_kernel(x_ref, nw_ref, wg_ref, wu_ref, wd_ref, o_ref):
    for r0 in range(0, FFN_TILE, FFN_CHAIN):
        rows = slice(r0, r0 + FFN_CHAIN)
        o_ref[rows, :] = _ffn_rows(x_ref[rows, :], nw_ref[0:1, :], nw_ref[1:2, :],
                                   wg_ref, wu_ref, wd_ref)


def _ffn(x, nw2, wg, wu, wd, layer):
    s = x.shape[0]
    return pl.pallas_call(
        _ffn_kernel,
        grid=(s // FFN_TILE,),
        in_specs=[
            pl.BlockSpec((FFN_TILE, D_MODEL), lambda i: (i, 0)),
            _const_spec((2, D_MODEL)),
            _layer_spec((D_MODEL, D_FF), layer),
            _layer_spec((D_MODEL, D_FF), layer),
            _layer_spec((D_FF, D_MODEL), layer),
        ],
        out_specs=pl.BlockSpec((FFN_TILE, D_MODEL), lambda i: (i, 0)),
        out_shape=jax.ShapeDtypeStruct((s, D_MODEL), F32),
        compiler_params=_params("parallel"),
        name="ffn",
    )(x, nw2, wg, wu, wd)


def _inproj_kernel(x_ref, nw_ref, w_ref, zhg_ref, q_ref, k_ref, vt_ref, ksum_ref, zcv_ref):
    xn = _rms(x_ref[...], nw_ref[...]).astype(BF16)
    c = 4 * HG_WIDTH
    zhg_ref[...] = jnp.dot(xn, w_ref[:, 0:c], preferred_element_type=F32)
    q_ref[...] = jnp.dot(xn, w_ref[:, c:c + AT_WIDTH], preferred_element_type=F32)
    k = jnp.dot(xn, w_ref[:, c + AT_WIDTH:c + 2 * AT_WIDTH], preferred_element_type=F32)
    v = jnp.dot(xn, w_ref[:, c + 2 * AT_WIDTH:c + 3 * AT_WIDTH], preferred_element_type=F32)
    c += 3 * AT_WIDTH
    zcv_ref[...] = jnp.dot(xn, w_ref[:, c:c + 3 * CV_WIDTH], preferred_element_type=F32)
    ones = jnp.ones((V_ROWS - AT_DIM, MOBA_BLOCK), F32)
    for b in range(ROW_TILE // MOBA_BLOCK):
        kb = k[b * MOBA_BLOCK:(b + 1) * MOBA_BLOCK, :]
        vbt = v[b * MOBA_BLOCK:(b + 1) * MOBA_BLOCK, :].T
        k_ref[b] = kb.astype(BF16)
        for h in range(AT_HEADS):
            vt_ref[b, h] = jnp.concatenate(
                [vbt[h * AT_DIM:(h + 1) * AT_DIM, :], ones], axis=0).astype(BF16)
        ksum_ref[b * KSUM_ROWS:(b + 1) * KSUM_ROWS, :] = jnp.sum(
            kb.reshape(MOBA_BLOCK // KSUM_ROWS, KSUM_ROWS, AT_WIDTH), axis=0)


def _inproj(x, nw, w_in, layer):
    s = x.shape[0]
    nblk = s // MOBA_BLOCK
    bpt = ROW_TILE // MOBA_BLOCK
    return pl.pallas_call(
        _inproj_kernel,
        grid=(s // ROW_TILE,),
        in_specs=[
            pl.BlockSpec((ROW_TILE, D_MODEL), lambda i: (i, 0)),
            _const_spec((1, D_MODEL)),
            _layer_spec((D_MODEL, D_IN), layer),
        ],
        out_specs=[
            pl.BlockSpec((ROW_TILE, 4 * HG_WIDTH), lambda i: (i, 0)),
            pl.BlockSpec((ROW_TILE, AT_WIDTH), lambda i: (i, 0)),
            pl.BlockSpec((bpt, MOBA_BLOCK, AT_WIDTH), lambda i: (i, 0, 0)),
            pl.BlockSpec((bpt, AT_HEADS, V_ROWS, MOBA_BLOCK), lambda i: (i, 0, 0, 0)),
            pl.BlockSpec((bpt * KSUM_ROWS, AT_WIDTH), lambda i: (i, 0)),
            pl.BlockSpec((ROW_TILE, 3 * CV_WIDTH), lambda i: (i, 0)),
        ],
        out_shape=[
            jax.ShapeDtypeStruct((s, 4 * HG_WIDTH), F32),
            jax.ShapeDtypeStruct((s, AT_WIDTH), F32),
            jax.ShapeDtypeStruct((nblk, MOBA_BLOCK, AT_WIDTH), BF16),
            jax.ShapeDtypeStruct((nblk, AT_HEADS, V_ROWS, MOBA_BLOCK), BF16),
            jax.ShapeDtypeStruct((nblk * KSUM_ROWS, AT_WIDTH), F32),
            jax.ShapeDtypeStruct((s, 3 * CV_WIDTH), F32),
        ],
        compiler_params=_params("parallel"),
        name="inproj",
    )(x, nw, w_in)


def _hg_levels():
    levels = []
    n = HG_TILE
    while n >= 2:
        levels.append(n)
        n //= 2
    return levels


def _hgrn_head(q, fp, v, gate, lb, nw, st_ref, row, masks):
    u = jnp.exp2(jnp.minimum(fp * -LOG2E, 126.0))
    r = 1.0 / (1.0 + u)
    logf = jnp.log2(lb + (1.0 - lb) * r)
    kk = (1.0 - lb) * (u * r)
    vb = v.astype(BF16)

    b = logf
    sh = 1
    while sh < 8:
        rolled = pltpu.roll(b, sh, axis=0)
        head = jnp.where(row[:8] >= sh, rolled[:8], 0.0)
        b = b + jnp.concatenate([head, rolled[8:]], axis=0)
        sh *= 2
    while sh < HG_TILE:
        b = jnp.concatenate([b[:sh], b[sh:] + b[:-sh]], axis=0)
        sh *= 2

    diag, same = masks
    scores = jnp.where(
        diag, lax.dot_general(q.astype(BF16), kk.astype(BF16), _NT, preferred_element_type=F32), 0.0)
    for n in _hg_levels():
        half = n // 2
        if half >= 8:
            qp, kp = [], []
            zero = jnp.zeros((half, HG_DIM), F32)
            for lo in range(0, HG_TILE, n):
                mid, hi = lo + half, lo + n
                bm = b[mid - 1:mid, :]
                qp += [zero, q[mid:hi] * jnp.exp2(b[mid:hi] - bm)]
                kp += [kk[lo:mid] * jnp.exp2(bm - b[lo:mid]), zero]
            qs = jnp.concatenate(qp, axis=0)
            ks = jnp.concatenate(kp, axis=0)
        else:
            upper = (row & (n - 1)) >= half
            if n == 2:
                ex = jnp.exp2(jnp.where(upper, logf, 0.0))
            else:
                b3 = b.reshape(HG_TILE // 8, 8, HG_DIM)
                sub = lax.broadcasted_iota(jnp.int32, (HG_TILE // 8, 8, HG_DIM), 1)
                bm3 = None
                for lo in range(0, 8, n):
                    piece = jnp.broadcast_to(b3[:, lo + half - 1:lo + half, :], b3.shape)
                    bm3 = piece if bm3 is None else jnp.where(sub >= lo, piece, bm3)
                ex = jnp.exp2(-jnp.abs(b - bm3.reshape(HG_TILE, HG_DIM)))
            qs = jnp.where(upper, q * ex, 0.0)
            ks = jnp.where(upper, 0.0, kk * ex)
        lvl = lax.dot_general(qs.astype(BF16), ks.astype(BF16), _NT, preferred_element_type=F32)
        scores = scores + (lvl if n == HG_TILE else jnp.where(same[n], lvl, 0.0))

    st = st_ref[...]
    b_last = b[HG_TILE - 1:HG_TILE, :]
    o = jnp.dot(scores.astype(BF16), vb, preferred_element_type=F32)
    o = o + lax.dot_general((q * jnp.exp2(b)).astype(BF16), st.astype(BF16), _NT,
                            preferred_element_type=F32)
    kdec = (kk * jnp.exp2(b_last - b)).astype(BF16)
    st_ref[...] = st * jnp.exp2(b_last) + lax.dot_general(vb, kdec, _TN, preferred_element_type=F32)
    return _rms(o, nw) * (gate * jax.nn.sigmoid(gate))


def _hgrn_kernel(layer, q_ref, f_ref, i_ref, g_ref, lb_ref, nw_ref, o_ref, st_ref):
    @pl.when(pl.program_id(0) == 0)
    def _():
        st_ref[...] = jnp.zeros_like(st_ref)

    lbraw = lb_ref[...]
    e = jnp.exp(lbraw - jnp.max(lbraw, axis=0, keepdims=True))
    soft = e / jnp.sum(e, axis=0, keepdims=True)
    lb = jnp.sum(soft[0:layer + 1, :], axis=0, keepdims=True) - soft[0:1, :]

    row = lax.broadcasted_iota(jnp.int32, (HG_TILE, HG_DIM), 0)
    ti = lax.broadcasted_iota(jnp.int32, (HG_TILE, HG_TILE), 0)
    si = lax.broadcasted_iota(jnp.int32, (HG_TILE, HG_TILE), 1)
    masks = (ti == si, {n: (ti & -n) == (si & -n) for n in _hg_levels() if n < HG_TILE})
    for h in range(HG_HEADS):
        cols = slice(h * HG_DIM, (h + 1) * HG_DIM)
        o_ref[:, cols] = _hgrn_head(q_ref[:, cols], f_ref[:, cols], i_ref[:, cols], g_ref[:, cols],
                                    lb[:, cols], nw_ref[...], st_ref.at[h], row, masks
                                    ).astype(o_ref.dtype)


def _hgrn(zhg, hg_lb, hg_nw, layer):
    s = zhg.shape[0]
    depth = hg_lb.shape[0]

    def col(k):
        return pl.BlockSpec((HG_TILE, HG_WIDTH), lambda t: (t, k))

    return pl.pallas_call(
        functools.partial(_hgrn_kernel, layer),
        grid=(s // HG_TILE,),
        in_specs=[
            col(0), col(1), col(2), col(3),
            pl.BlockSpec((depth, HG_WIDTH), lambda t: (0, 0)),
            pl.BlockSpec((1, HG_DIM), lambda t: (0, 0)),
        ],
        out_specs=pl.BlockSpec((HG_TILE, HG_WIDTH), lambda t: (t, 0)),
        out_shape=jax.ShapeDtypeStruct((s, HG_WIDTH), BF16),
        scratch_shapes=[pltpu.VMEM((HG_HEADS, HG_DIM, HG_DIM), F32)],
        compiler_params=_params("arbitrary"),
        name="hgrn2",
    )(zhg, zhg, zhg, zhg, hg_lb, hg_nw)


N_BIAS_TABLES = 5


def _bucket_thresholds():
    max_exact = REL_BUCKETS // 2
    d = np.arange(1, 2 * REL_MAX_DIST, dtype=np.float64)
    large = max_exact + (np.log(d / max_exact) / math.log(REL_MAX_DIST / max_exact)
                         * (REL_BUCKETS - max_exact)).astype(np.int64)
    large = np.minimum(large, REL_BUCKETS - 1)
    bucket = np.where(d < max_exact, d.astype(np.int64), large)
    thr = [0] * REL_BUCKETS
    for bkt in range(1, REL_BUCKETS):
        thr[bkt] = int(d[np.argmax(bucket >= bkt)])
    assert (N_BIAS_TABLES - 1) * MOBA_BLOCK + 1 >= thr[REL_BUCKETS - 1]
    return thr


def _bias_kernel(rb_ref, o_ref):
    thr = _bucket_thresholds()
    key = lax.broadcasted_iota(jnp.int32, (MOBA_BLOCK, MOBA_BLOCK), 0)
    qry = lax.broadcasted_iota(jnp.int32, (MOBA_BLOCK, MOBA_BLOCK), 1)
    o_ref[N_BIAS_TABLES:] = jnp.zeros((N_BIAS_TABLES - 1,) + o_ref.shape[1:], F32)
    for u in range(N_BIAS_TABLES):
        t = N_BIAS_TABLES - 1 - u
        dist = qry - key + t * MOBA_BLOCK
        for h in range(AT_HEADS):
            val = jnp.full((MOBA_BLOCK, MOBA_BLOCK), rb_ref[REL_BUCKETS - 1, h], F32)
            for bkt in range(REL_BUCKETS - 2, -1, -1):
                val = jnp.where(dist < thr[bkt + 1], rb_ref[bkt, h], val)
            val = val * LOG2E
            if t == 0:
                val = jnp.where(dist < 0, NEG, val)
            o_ref[u, :, h * MOBA_BLOCK:(h + 1) * MOBA_BLOCK] = val


def _bias_tables(rel_bias):
    return pl.pallas_call(
        _bias_kernel,
        in_specs=[pl.BlockSpec(memory_space=pltpu.SMEM)],
        out_shape=jax.ShapeDtypeStruct((2 * N_BIAS_TABLES - 1, MOBA_BLOCK, AT_HEADS * MOBA_BLOCK), F32),
        name="moba_bias",
    )(rel_bias)


def _moba_kernel(rb_ref, q_ref, k_ref, vt_ref, ksum_ref, bias_ref, o_ref,
                 kmean_ref, mnear_ref, mfar_ref, qs_ref, m_ref, acc_ref, sa_ref, sb_ref):
    i = pl.program_id(0)
    nblk = k_ref.shape[0]

    @pl.when(i == 0)
    def _():
        ks = ksum_ref[...].reshape(nblk, KSUM_ROWS, AT_WIDTH)
        kmean_ref[...] = jnp.sum(ks, axis=1) * (1.0 / MOBA_BLOCK)

    qt = q_ref[...].T
    zero = jnp.zeros((AT_DIM, MOBA_BLOCK), F32)
    qmt = jnp.concatenate(
        [jnp.concatenate([qt[r * AT_DIM:(r + 1) * AT_DIM, :] if r == h else zero
                          for r in range(AT_HEADS)], axis=0) for h in range(AT_HEADS)], axis=1)
    qs_ref[...] = (qmt * (AT_DIM ** -0.5 * LOG2E)).astype(BF16)

    ncol = AT_HEADS * MOBA_BLOCK
    jio = lax.broadcasted_iota(jnp.int32, (nblk, ncol), 0).astype(F32)
    fi = i.astype(F32)
    n_far = jnp.maximum(i - (N_BIAS_TABLES - 1), 0)
    gate = jnp.dot(kmean_ref[...], qmt, precision=lax.Precision.HIGHEST,
                   preferred_element_type=F32)
    gate = jnp.where(jio < fi, gate, -jnp.inf)
    sel = jio == fi
    for _ in range(MOBA_TOPK):
        mx = jnp.max(gate, axis=0, keepdims=True)
        cand = jnp.where(gate == mx, jio, float(nblk))
        idx = jnp.min(cand, axis=0, keepdims=True)
        pick = (jio == idx) & (mx > -jnp.inf)
        sel = sel | pick
        gate = jnp.where(pick, -jnp.inf, gate)
    col_head = lax.broadcasted_iota(jnp.int32, (1, ncol), 1) // MOBA_BLOCK
    far_bias = jnp.zeros((1, ncol), F32)
    for h in range(AT_HEADS):
        far_bias = jnp.where(col_head == h, rb_ref[REL_BUCKETS - 1, h] * LOG2E, far_bias)
    mnear_ref[...] = jnp.where(sel, 0.0, NEG)
    mfar_ref[...] = jnp.where(sel & (jio < n_far.astype(F32)), far_bias, NEG)
    m_ref[...] = jnp.full(m_ref.shape, NEG, F32)
    acc_ref[...] = jnp.zeros(acc_ref.shape, F32)

    def logits_into(s_ref, j0, nb):
        kg = k_ref[pl.ds(j0, nb)].reshape(nb * MOBA_BLOCK, AT_WIDTH)
        s_ref[0:nb * MOBA_BLOCK, :] = jnp.dot(kg, qs_ref[...], preferred_element_type=F32)

    def fold(s_ref, j0, nb, mask_ref, table):
        for h in range(AT_HEADS):
            cols = slice(h * MOBA_BLOCK, (h + 1) * MOBA_BLOCK)
            m = m_ref[:, cols]
            acc = acc_ref[:, cols]
            for g in range(nb):
                rows = slice(g * MOBA_BLOCK, (g + 1) * MOBA_BLOCK)
                mrow = mask_ref[pl.ds(j0 + g, 1), cols]
                s = s_ref[rows, cols]
                if table is not None:
                    s = s + mrow + bias_ref[table + g, :, cols]
                mb = jnp.max(s, axis=0, keepdims=True)
                p = jnp.exp2(s - mb).astype(BF16)
                if table is None:
                    mb = mb + mrow
                pv = jnp.dot(vt_ref[j0 + g, h], p, preferred_element_type=F32)
                m_new = jnp.maximum(m, mb)
                acc = jnp.exp2(m - m_new) * acc + jnp.exp2(mb - m_new) * pv
                m = m_new
            m_ref[:, cols] = m
            acc_ref[:, cols] = acc

    near0 = jnp.maximum(i - (N_BIAS_TABLES - 1), 0)
    n_groups = (n_far + FAR_GROUP - 1) // FAR_GROUP
    last_group = nblk // FAR_GROUP - 1
    logits_into(sa_ref, near0, N_BIAS_TABLES)
    logits_into(sb_ref, 0, FAR_GROUP)
    fold(sa_ref, near0, N_BIAS_TABLES, mnear_ref, near0 - (i - (N_BIAS_TABLES - 1)))

    def far_pair(gp, carry):
        ga = 2 * gp
        logits_into(sa_ref, FAR_GROUP * (ga + 1), FAR_GROUP)
        fold(sb_ref, FAR_GROUP * ga, FAR_GROUP, mfar_ref, None)
        logits_into(sb_ref, FAR_GROUP * jnp.minimum(ga + 2, last_group), FAR_GROUP)
        fold(sa_ref, FAR_GROUP * (ga + 1), FAR_GROUP, mfar_ref, None)
        return carry

    lax.fori_loop(0, (n_groups + 1) // 2, far_pair, 0)
    acc = acc_ref[...]
    ot = acc[0:AT_DIM, :] / acc[AT_DIM:AT_DIM + 1, :]
    o_ref[...] = jnp.concatenate(
        [ot[:, h * MOBA_BLOCK:(h + 1) * MOBA_BLOCK] for h in range(AT_HEADS)], axis=0
    ).T.astype(o_ref.dtype)


def _moba(rel_bias, q, k, vt, ksum, bias):
    s = q.shape[0]
    nblk = s // MOBA_BLOCK
    assert nblk >= N_BIAS_TABLES and FAR_GROUP <= N_BIAS_TABLES and nblk % FAR_GROUP == 0
    return pl.pallas_call(
        _moba_kernel,
        grid=(nblk,),
        in_specs=[
            pl.BlockSpec(memory_space=pltpu.SMEM),
            pl.BlockSpec((MOBA_BLOCK, AT_WIDTH), lambda i: (i, 0)),
            _const_spec((nblk, MOBA_BLOCK, AT_WIDTH)),
            _const_spec((nblk, AT_HEADS, V_ROWS, MOBA_BLOCK)),
            _const_spec((nblk * KSUM_ROWS, AT_WIDTH)),
            _const_spec((2 * N_BIAS_TABLES - 1, MOBA_BLOCK, AT_HEADS * MOBA_BLOCK)),
        ],
        out_specs=pl.BlockSpec((MOBA_BLOCK, AT_WIDTH), lambda i: (i, 0)),
        out_shape=jax.ShapeDtypeStruct((s, AT_WIDTH), BF16),
        scratch_shapes=[
            pltpu.VMEM((nblk, AT_WIDTH), F32),
            pltpu.VMEM((nblk, AT_HEADS * MOBA_BLOCK), F32),
            pltpu.VMEM((nblk, AT_HEADS * MOBA_BLOCK), F32),
            pltpu.VMEM((AT_WIDTH, AT_HEADS * MOBA_BLOCK), BF16),
            pltpu.VMEM((1, AT_HEADS * MOBA_BLOCK), F32),
            pltpu.VMEM((V_ROWS, AT_HEADS * MOBA_BLOCK), F32),
            pltpu.VMEM((N_BIAS_TABLES * MOBA_BLOCK, AT_HEADS * MOBA_BLOCK), F32),
            pltpu.VMEM((FAR_GROUP * MOBA_BLOCK, AT_HEADS * MOBA_BLOCK), F32),
        ],
        compiler_params=_params("arbitrary"),
        name="moba",
    )(rel_bias, q, k, vt, ksum, bias)


def _mix_rows(x, ohg, oat, zcv, uh, cw, w_ref, nw):
    bgate = zcv[:, 0:CV_WIDTH]
    u = zcv[:, CV_WIDTH:2 * CV_WIDTH] * zcv[:, 2 * CV_WIDTH:3 * CV_WIDTH]
    row = lax.broadcasted_iota(jnp.int32, u.shape, 0)
    u1 = jnp.where(row == 0, uh[7:8, :], pltpu.roll(u, 1, axis=0))
    u2 = jnp.where(row == 0, uh[6:7, :], jnp.where(row == 1, uh[7:8, :], pltpu.roll(u, 2, axis=0)))
    ocv = bgate * (cw[0:1, :] * u2 + cw[1:2, :] * u1 + cw[2:3, :] * u)
    h = jnp.dot(ohg, w_ref[0:HG_WIDTH, :], preferred_element_type=F32)
    h = h + jnp.dot(oat, w_ref[HG_WIDTH:HG_WIDTH + AT_WIDTH, :], preferred_element_type=F32)
    h = h + jnp.dot(ocv.astype(BF16), w_ref[HG_WIDTH + AT_WIDTH:D_MIX, :], preferred_element_type=F32)
    return x + _rms(h, nw)


def _mixffn_kernel(x_ref, ohg_ref, oat_ref, zcv_ref, halo_ref, cw_ref, wo_ref, nw_ref,
                   wg_ref, wu_ref, wd_ref, o_ref):
    halo = halo_ref[...]
    uh = halo[:, CV_WIDTH:2 * CV_WIDTH] * halo[:, 2 * CV_WIDTH:3 * CV_WIDTH]
    uh = jnp.where(pl.program_id(0) > 0, uh, 0.0)
    cw = cw_ref[...]
    for r0 in range(0, FFN_TILE, FFN_CHAIN):
        rows = slice(r0, r0 + FFN_CHAIN)
        if r0 > 0:
            prev = zcv_ref[r0 - 8:r0, :]
            uh = prev[:, CV_WIDTH:2 * CV_WIDTH] * prev[:, 2 * CV_WIDTH:3 * CV_WIDTH]
        y = _mix_rows(x_ref[rows, :], ohg_ref[rows, :], oat_ref[rows, :], zcv_ref[rows, :], uh, cw,
                      wo_ref, nw_ref[0:1, :])
        o_ref[rows, :] = _ffn_rows(y, nw_ref[1:2, :], nw_ref[2:3, :], wg_ref, wu_ref, wd_ref)


def _mixffn(x, ohg, oat, zcv, conv_w, w_out, nw3, wg, wu, wd, layer):
    s = x.shape[0]
    halo_blocks = FFN_TILE // 8
    return pl.pallas_call(
        _mixffn_kernel,
        grid=(s // FFN_TILE,),
        in_specs=[
            pl.BlockSpec((FFN_TILE, D_MODEL), lambda i: (i, 0)),
            pl.BlockSpec((FFN_TILE, HG_WIDTH), lambda i: (i, 0)),
            pl.BlockSpec((FFN_TILE, AT_WIDTH), lambda i: (i, 0)),
            pl.BlockSpec((FFN_TILE, 3 * CV_WIDTH), lambda i: (i, 0)),
            pl.BlockSpec((8, 3 * CV_WIDTH), lambda i: (jnp.maximum(i * halo_blocks - 1, 0), 0)),
            _const_spec((CV_KERNEL, CV_WIDTH)),
            _layer_spec((D_MIX, D_MODEL), layer),
            _const_spec((3, D_MODEL)),
            _layer_spec((D_MODEL, D_FF), layer),
            _layer_spec((D_MODEL, D_FF), layer),
            _layer_spec((D_FF, D_MODEL), layer),
        ],
        out_specs=pl.BlockSpec((FFN_TILE, D_MODEL), lambda i: (i, 0)),
        out_shape=jax.ShapeDtypeStruct((s, D_MODEL), F32),
        compiler_params=_params("parallel"),
        name="mixffn",
    )(x, ohg, oat, zcv, zcv, conv_w, w_out, nw3, wg, wu, wd)


def kernel(x, norm_w, ffn1_wg, ffn1_wu, ffn1_wd, mix_w_in, mix_w_out, hg_lb, hg_norm_w, conv_w,
           ffn2_wg, ffn2_wu, ffn2_wd, rel_bias):
    batch, seq, _ = x.shape
    depth = norm_w.shape[0]
    assert batch == 1 and seq % ROW_TILE == 0 and seq % MOBA_BLOCK == 0
    rel_bias = rel_bias.astype(F32)
    bias = _bias_tables(rel_bias)
    y = x.reshape(seq, D_MODEL)
    w1 = [w.astype(BF16) for w in (ffn1_wg, ffn1_wu, ffn1_wd)]
    w2 = [w.astype(BF16) for w in (ffn2_wg, ffn2_wu, ffn2_wd)]
    w_in = mix_w_in.astype(BF16)
    w_out = mix_w_out.astype(BF16)
    for l in range(depth):
        y = _ffn(y, norm_w[l, 0:2], *w1, l)
        zhg, q, k, vt, ksum, zcv = _inproj(y, norm_w[l, 2:3], w_in, l)
        ohg = _hgrn(zhg, hg_lb, hg_norm_w[l:l + 1], l)
        oat = _moba(rel_bias, q, k, vt, ksum, bias)
        y = _mixffn(y, ohg, oat, zcv, conv_w[l], w_out, norm_w[l, 3:6], *w2, l)
    return y.reshape(batch, seq, D_MODEL)
```

```python
import functools
import math

import numpy as np
import jax
import jax.numpy as jnp
from jax import lax
from jax.experimental import pallas as pl
from jax.experimental.pallas import tpu as pltpu

F32 = jnp.float32
BF16 = jnp.bfloat16

D_MODEL = 1024
D_FF = 2816
HG_HEADS = 4
HG_DIM = 128
HG_WIDTH = HG_HEADS * HG_DIM
AT_HEADS = 4
AT_DIM = 64
AT_WIDTH = AT_HEADS * AT_DIM
MOBA_BLOCK = 256
MOBA_TOPK = 3
REL_BUCKETS = 32
REL_MAX_DIST = 1024
CV_WIDTH = 256
CV_KERNEL = 3
D_MIX = HG_WIDTH + AT_WIDTH + CV_WIDTH
D_IN = 4 * HG_WIDTH + 3 * AT_WIDTH + 3 * CV_WIDTH
EPS = 1e-6

ROW_TILE = 512
FFN_TILE = 1024
FFN_CHAIN = 512
HG_TILE = 256
KSUM_ROWS = 8
V_ROWS = AT_DIM + 16
LOG2E = math.log2(math.e)
FAR_GROUP = 2
NEG = -1e30
VMEM_LIMIT = 56 * 1024 * 1024
FF_CHUNKS = ((0, 1536), (1536, 2816))

_NT = (((1,), (1,)), ((), ()))
_TN = (((0,), (0,)), ((), ()))


def _rms(x, w):
    ms = jnp.mean(x * x, axis=-1, keepdims=True)
    return x * lax.rsqrt(ms + EPS) * w


def _const_spec(shape):
    nd = len(shape)
    return pl.BlockSpec(shape, lambda *_: (0,) * nd, pipeline_mode=pl.Buffered(1))


def _layer_spec(shape, layer):
    nd = len(shape)
    return pl.BlockSpec((None,) + tuple(shape), lambda *_: (layer,) + (0,) * nd,
                        pipeline_mode=pl.Buffered(1))


def _params(*sem):
    return pltpu.CompilerParams(dimension_semantics=sem, vmem_limit_bytes=VMEM_LIMIT)


def _ffn_rows(x, nw_pre, nw_post, wg_ref, wu_ref, wd_ref):
    xn = _rms(x, nw_pre).astype(BF16)
    h = None
    for c0, c1 in FF_CHUNKS:
        g = jnp.dot(xn, wg_ref[:, c0:c1], preferred_element_type=F32)
        u = jnp.dot(xn, wu_ref[:, c0:c1], preferred_element_type=F32)
        a = (g * jax.nn.sigmoid(g) * u).astype(BF16)
        part = jnp.dot(a, wd_ref[c0:c1, :], preferred_element_type=F32)
        h = part if h is None else h + part
    return x + 0.5 * _rms(h, nw_post)


def _ffn_kernel(x_ref, nw_ref, wg_ref, wu_ref, wd_ref, o_ref):
    for r0 in range(0, FFN_TILE, FFN_CHAIN):
        rows = slice(r0, r0 + FFN_CHAIN)
        o_ref[rows, :] = _ffn_rows(x_ref[rows, :], nw_ref[0:1, :], nw_ref[1:2, :],
                                   wg_ref, wu_ref, wd_ref)


def _ffn(x, nw2, wg, wu, wd, layer):
    s = x.shape[0]
    return pl.pallas_call(
        _ffn_kernel,
        grid=(s // FFN_TILE,),
        in_specs=[
            pl.BlockSpec((FFN_TILE, D_MODEL), lambda i: (i, 0)),
            _const_spec((2, D_MODEL)),
            _layer_spec((D_MODEL, D_FF), layer),
            _layer_spec((D_MODEL, D_FF), layer),
            _layer_spec((D_FF, D_MODEL), layer),
        ],
        out_specs=pl.BlockSpec((FFN_TILE, D_MODEL), lambda i: (i, 0)),
        out_shape=jax.ShapeDtypeStruct((s, D_MODEL), F32),
        compiler_params=_params("parallel"),
        name="ffn",
    )(x, nw2, wg, wu, wd)


def _inproj_kernel(x_ref, nw_ref, w_ref, zhg_ref, q_ref, k_ref, vt_ref, ksum_ref, zcv_ref):
    xn = _rms(x_ref[...], nw_ref[...]).astype(BF16)
    c = 4 * HG_WIDTH
    zhg_ref[...] = jnp.dot(xn, w_ref[:, 0:c], preferred_element_type=F32)
    q_ref[...] = jnp.dot(xn, w_ref[:, c:c + AT_WIDTH], preferred_element_type=F32)
    k = jnp.dot(xn, w_ref[:, c + AT_WIDTH:c + 2 * AT_WIDTH], preferred_element_type=F32)
    v = jnp.dot(xn, w_ref[:, c + 2 * AT_WIDTH:c + 3 * AT_WIDTH], preferred_element_type=F32)
    c += 3 * AT_WIDTH
    zcv_ref[...] = jnp.dot(xn, w_ref[:, c:c + 3 * CV_WIDTH], preferred_element_type=F32)
    ones = jnp.ones((V_ROWS - AT_DIM, MOBA_BLOCK), F32)
    for b in range(ROW_TILE // MOBA_BLOCK):
        kb = k[b * MOBA_BLOCK:(b + 1) * MOBA_BLOCK, :]
        vbt = v[b * MOBA_BLOCK:(b + 1) * MOBA_BLOCK, :].T
        k_ref[b] = kb.astype(BF16)
        for h in range(AT_HEADS):
            vt_ref[b, h] = jnp.concatenate(
                [vbt[h * AT_DIM:(h + 1) * AT_DIM, :], ones], axis=0).astype(BF16)
        ksum_ref[b * KSUM_ROWS:(b + 1) * KSUM_ROWS, :] = jnp.sum(
            kb.reshape(MOBA_BLOCK // KSUM_ROWS, KSUM_ROWS, AT_WIDTH), axis=0)


def _inproj(x, nw, w_in, layer):
    s = x.shape[0]
    nblk = s // MOBA_BLOCK
    bpt = ROW_TILE // MOBA_BLOCK
    return pl.pallas_call(
        _inproj_kernel,
        grid=(s // ROW_TILE,),
        in_specs=[
            pl.BlockSpec((ROW_TILE, D_MODEL), lambda i: (i, 0)),
            _const_spec((1, D_MODEL)),
            _layer_spec((D_MODEL, D_IN), layer),
        ],
        out_specs=[
            pl.BlockSpec((ROW_TILE, 4 * HG_WIDTH), lambda i: (i, 0)),
            pl.BlockSpec((ROW_TILE, AT_WIDTH), lambda i: (i, 0)),
            pl.BlockSpec((bpt, MOBA_BLOCK, AT_WIDTH), lambda i: (i, 0, 0)),
            pl.BlockSpec((bpt, AT_HEADS, V_ROWS, MOBA_BLOCK), lambda i: (i, 0, 0, 0)),
            pl.BlockSpec((bpt * KSUM_ROWS, AT_WIDTH), lambda i: (i, 0)),
            pl.BlockSpec((ROW_TILE, 3 * CV_WIDTH), lambda i: (i, 0)),
        ],
        out_shape=[
            jax.ShapeDtypeStruct((s, 4 * HG_WIDTH), F32),
            jax.ShapeDtypeStruct((s, AT_WIDTH), F32),
            jax.ShapeDtypeStruct((nblk, MOBA_BLOCK, AT_WIDTH), BF16),
            jax.ShapeDtypeStruct((nblk, AT_HEADS, V_ROWS, MOBA_BLOCK), BF16),
            jax.ShapeDtypeStruct((nblk * KSUM_ROWS, AT_WIDTH), F32),
            jax.ShapeDtypeStruct((s, 3 * CV_WIDTH), F32),
        ],
        compiler_params=_params("parallel"),
        name="inproj",
    )(x, nw, w_in)


def _hg_levels():
    levels = []
    n = HG_TILE
    while n >= 2:
        levels.append(n)
        n //= 2
    return levels


def _hgrn_head(q, fp, v, gate, lb, nw, st_ref, row, masks):
    u = jnp.exp2(jnp.minimum(fp * -LOG2E, 126.0))
    r = 1.0 / (1.0 + u)
    logf = jnp.log2(lb + (1.0 - lb) * r)
    kk = (1.0 - lb) * (u * r)
    vb = v.astype(BF16)

    b = logf
    sh = 1
    while sh < 8:
        rolled = pltpu.roll(b, sh, axis=0)
        head = jnp.where(row[:8] >= sh, rolled[:8], 0.0)
        b = b + jnp.concatenate([head, rolled[8:]], axis=0)
        sh *= 2
    while sh < HG_TILE:
        b = jnp.concatenate([b[:sh], b[sh:] + b[:-sh]], axis=0)
        sh *= 2

    diag, same = masks
    scores = jnp.where(
        diag, lax.dot_general(q.astype(BF16), kk.astype(BF16), _NT, preferred_element_type=F32), 0.0)
    for n in _hg_levels():
        half = n // 2
        if half >= 8:
            qp, kp = [], []
            zero = jnp.zeros((half, HG_DIM), F32)
            for lo in range(0, HG_TILE, n):
                mid, hi = lo + half, lo + n
                bm = b[mid - 1:mid, :]
                qp += [zero, q[mid:hi] * jnp.exp2(b[mid:hi] - bm)]
                kp += [kk[lo:mid] * jnp.exp2(bm - b[lo:mid]), zero]
            qs = jnp.concatenate(qp, axis=0)
            ks = jnp.concatenate(kp, axis=0)
        else:
            upper = (row & (n - 1)) >= half
            if n == 2:
                ex = jnp.exp2(jnp.where(upper, logf, 0.0))
            else:
                b3 = b.reshape(HG_TILE // 8, 8, HG_DIM)
                sub = lax.broadcasted_iota(jnp.int32, (HG_TILE // 8, 8, HG_DIM), 1)
                bm3 = None
                for lo in range(0, 8, n):
                    piece = jnp.broadcast_to(b3[:, lo + half - 1:lo + half, :], b3.shape)
                    bm3 = piece if bm3 is None else jnp.where(sub >= lo, piece, bm3)
                ex = jnp.exp2(-jnp.abs(b - bm3.reshape(HG_TILE, HG_DIM)))
            qs = jnp.where(upper, q * ex, 0.0)
            ks = jnp.where(upper, 0.0, kk * ex)
        lvl = lax.dot_general(qs.astype(BF16), ks.astype(BF16), _NT, preferred_element_type=F32)
        scores = scores + (lvl if n == HG_TILE else jnp.where(same[n], lvl, 0.0))

    st = st_ref[...]
    b_last = b[HG_TILE - 1:HG_TILE, :]
    o = jnp.dot(scores.astype(BF16), vb, preferred_element_type=F32)
    o = o + lax.dot_general((q * jnp.exp2(b)).astype(BF16), st.astype(BF16), _NT,
                            preferred_element_type=F32)
    kdec = (kk * jnp.exp2(b_last - b)).astype(BF16)
    st_ref[...] = st * jnp.exp2(b_last) + lax.dot_general(vb, kdec, _TN, preferred_element_type=F32)
    return _rms(o, nw) * (gate * jax.nn.sigmoid(gate))


def _hgrn_tiles(layer, q_ref, f_ref, i_ref, g_ref, lb_ref, nw_ref, o_ref, st_ref):
    lbraw = lb_ref[...]
    e = jnp.exp(lbraw - jnp.max(lbraw, axis=0, keepdims=True))
    soft = e / jnp.sum(e, axis=0, keepdims=True)
    lb = jnp.sum(soft[0:layer + 1, :], axis=0, keepdims=True) - soft[0:1, :]

    row = lax.broadcasted_iota(jnp.int32, (HG_TILE, HG_DIM), 0)
    ti = lax.broadcasted_iota(jnp.int32, (HG_TILE, HG_TILE), 0)
    si = lax.broadcasted_iota(jnp.int32, (HG_TILE, HG_TILE), 1)
    masks = (ti == si, {n: (ti & -n) == (si & -n) for n in _hg_levels() if n < HG_TILE})
    for r0 in range(0, o_ref.shape[0], HG_TILE):
        rows = slice(r0, r0 + HG_TILE)
        for h in range(HG_HEADS):
            cols = slice(h * HG_DIM, (h + 1) * HG_DIM)
            o_ref[rows, cols] = _hgrn_head(
                q_ref[rows, cols], f_ref[rows, cols], i_ref[rows, cols], g_ref[rows, cols],
                lb[:, cols], nw_ref[...], st_ref.at[h], row, masks).astype(o_ref.dtype)


def _hgrn_kernel(layer, q_ref, f_ref, i_ref, g_ref, lb_ref, nw_ref, o_ref, st_ref):
    @pl.when(pl.program_id(0) == 0)
    def _():
        st_ref[...] = jnp.zeros_like(st_ref)

    _hgrn_tiles(layer, q_ref, f_ref, i_ref, g_ref, lb_ref, nw_ref, o_ref, st_ref)


def _hgrn(zhg, hg_lb, hg_nw, layer):
    s = zhg.shape[0]
    depth = hg_lb.shape[0]

    def col(k):
        return pl.BlockSpec((HG_TILE, HG_WIDTH), lambda t: (t, k))

    return pl.pallas_call(
        functools.partial(_hgrn_kernel, layer),
        grid=(s // HG_TILE,),
        in_specs=[
            col(0), col(1), col(2), col(3),
            pl.BlockSpec((depth, HG_WIDTH), lambda t: (0, 0)),
            pl.BlockSpec((1, HG_DIM), lambda t: (0, 0)),
        ],
        out_specs=pl.BlockSpec((HG_TILE, HG_WIDTH), lambda t: (t, 0)),
        out_shape=jax.ShapeDtypeStruct((s, HG_WIDTH), BF16),
        scratch_shapes=[pltpu.VMEM((HG_HEADS, HG_DIM, HG_DIM), F32)],
        compiler_params=_params("arbitrary"),
        name="hgrn2",
    )(zhg, zhg, zhg, zhg, hg_lb, hg_nw)


N_BIAS_TABLES = 5


def _bucket_thresholds():
    max_exact = REL_BUCKETS // 2
    d = np.arange(1, 2 * REL_MAX_DIST, dtype=np.float64)
    large = max_exact + (np.log(d / max_exact) / math.log(REL_MAX_DIST / max_exact)
                         * (REL_BUCKETS - max_exact)).astype(np.int64)
    large = np.minimum(large, REL_BUCKETS - 1)
    bucket = np.where(d < max_exact, d.astype(np.int64), large)
    thr = [0] * REL_BUCKETS
    for bkt in range(1, REL_BUCKETS):
        thr[bkt] = int(d[np.argmax(bucket >= bkt)])
    assert (N_BIAS_TABLES - 1) * MOBA_BLOCK + 1 >= thr[REL_BUCKETS - 1]
    return thr


def _bias_kernel(rb_ref, o_ref):
    thr = _bucket_thresholds()
    key = lax.broadcasted_iota(jnp.int32, (MOBA_BLOCK, MOBA_BLOCK), 0)
    qry = lax.broadcasted_iota(jnp.int32, (MOBA_BLOCK, MOBA_BLOCK), 1)
    o_ref[N_BIAS_TABLES:] = jnp.zeros((N_BIAS_TABLES - 1,) + o_ref.shape[1:], F32)
    for u in range(N_BIAS_TABLES):
        t = N_BIAS_TABLES - 1 - u
        dist = qry - key + t * MOBA_BLOCK
        for h in range(AT_HEADS):
            val = jnp.full((MOBA_BLOCK, MOBA_BLOCK), rb_ref[REL_BUCKETS - 1, h], F32)
            for bkt in range(REL_BUCKETS - 2, -1, -1):
                val = jnp.where(dist < thr[bkt + 1], rb_ref[bkt, h], val)
            val = val * LOG2E
            if t == 0:
                val = jnp.where(dist < 0, NEG, val)
            o_ref[u, :, h * MOBA_BLOCK:(h + 1) * MOBA_BLOCK] = val


def _bias_tables(rel_bias):
    return pl.pallas_call(
        _bias_kernel,
        in_specs=[pl.BlockSpec(memory_space=pltpu.SMEM)],
        out_shape=jax.ShapeDtypeStruct((2 * N_BIAS_TABLES - 1, MOBA_BLOCK, AT_HEADS * MOBA_BLOCK), F32),
        name="moba_bias",
    )(rel_bias)


def _moba_kernel(rb_ref, q_ref, k_ref, vt_ref, ksum_ref, bias_ref, o_ref,
                 kmean_ref, mnear_ref, mfar_ref, qs_ref, m_ref, acc_ref, sa_ref, sb_ref):
    i = pl.program_id(0)
    nblk = k_ref.shape[0]

    @pl.when(i == 0)
    def _():
        ks = ksum_ref[...].reshape(nblk, KSUM_ROWS, AT_WIDTH)
        kmean_ref[...] = jnp.sum(ks, axis=1) * (1.0 / MOBA_BLOCK)

    qt = q_ref[...].T
    zero = jnp.zeros((AT_DIM, MOBA_BLOCK), F32)
    qmt = jnp.concatenate(
        [jnp.concatenate([qt[r * AT_DIM:(r + 1) * AT_DIM, :] if r == h else zero
                          for r in range(AT_HEADS)], axis=0) for h in range(AT_HEADS)], axis=1)
    qs_ref[...] = (qmt * (AT_DIM ** -0.5 * LOG2E)).astype(BF16)

    ncol = AT_HEADS * MOBA_BLOCK
    jio = lax.broadcasted_iota(jnp.int32, (nblk, ncol), 0).astype(F32)
    fi = i.astype(F32)
    n_far = jnp.maximum(i - (N_BIAS_TABLES - 1), 0)
    gate = jnp.dot(kmean_ref[...], qmt, precision=lax.Precision.HIGHEST,
                   preferred_element_type=F32)
    gate = jnp.where(jio < fi, gate, -jnp.inf)
    sel = jio == fi
    for _ in range(MOBA_TOPK):
        mx = jnp.max(gate, axis=0, keepdims=True)
        cand = jnp.where(gate == mx, jio, float(nblk))
        idx = jnp.min(cand, axis=0, keepdims=True)
        pick = (jio == idx) & (mx > -jnp.inf)
        sel = sel | pick
        gate = jnp.where(pick, -jnp.inf, gate)
    col_head = lax.broadcasted_iota(jnp.int32, (1, ncol), 1) // MOBA_BLOCK
    far_bias = jnp.zeros((1, ncol), F32)
    for h in range(AT_HEADS):
        far_bias = jnp.where(col_head == h, rb_ref[REL_BUCKETS - 1, h] * LOG2E, far_bias)
    mnear_ref[...] = jnp.where(sel, 0.0, NEG)
    mfar_ref[...] = jnp.where(sel & (jio < n_far.astype(F32)), far_bias, NEG)
    m_ref[...] = jnp.full(m_ref.shape, NEG, F32)
    acc_ref[...] = jnp.zeros(acc_ref.shape, F32)

    def logits_into(s_ref, j0, nb):
        kg = k_ref[pl.ds(j0, nb)].reshape(nb * MOBA_BLOCK, AT_WIDTH)
        s_ref[0:nb * MOBA_BLOCK, :] = jnp.dot(kg, qs_ref[...], preferred_element_type=F32)

    def fold(s_ref, j0, nb, mask_ref, table):
        for h in range(AT_HEADS):
            cols = slice(h * MOBA_BLOCK, (h + 1) * MOBA_BLOCK)
            m = m_ref[:, cols]
            acc = acc_ref[:, cols]
            for g in range(nb):
                rows = slice(g * MOBA_BLOCK, (g + 1) * MOBA_BLOCK)
                mrow = mask_ref[pl.ds(j0 + g, 1), cols]
                s = s_ref[rows, cols]
                if table is not None:
                    s = s + mrow + bias_ref[table + g, :, cols]
                mb = jnp.max(s, axis=0, keepdims=True)
                p = jnp.exp2(s - mb).astype(BF16)
                if table is None:
                    mb = mb + mrow
                pv = jnp.dot(vt_ref[j0 + g, h], p, preferred_element_type=F32)
                m_new = jnp.maximum(m, mb)
                acc = jnp.exp2(m - m_new) * acc + jnp.exp2(mb - m_new) * pv
                m = m_new
            m_ref[:, cols] = m
            acc_ref[:, cols] = acc

    near0 = jnp.maximum(i - (N_BIAS_TABLES - 1), 0)
    n_groups = (n_far + FAR_GROUP - 1) // FAR_GROUP
    last_group = nblk // FAR_GROUP - 1
    logits_into(sa_ref, near0, N_BIAS_TABLES)
    logits_into(sb_ref, 0, FAR_GROUP)
    fold(sa_ref, near0, N_BIAS_TABLES, mnear_ref, near0 - (i - (N_BIAS_TABLES - 1)))

    def far_pair(gp, carry):
        ga = 2 * gp
        logits_into(sa_ref, FAR_GROUP * (ga + 1), FAR_GROUP)
        fold(sb_ref, FAR_GROUP * ga, FAR_GROUP, mfar_ref, None)
        logits_into(sb_ref, FAR_GROUP * jnp.minimum(ga + 2, last_group), FAR_GROUP)
        fold(sa_ref, FAR_GROUP * (ga + 1), FAR_GROUP, mfar_ref, None)
        return carry

    lax.fori_loop(0, (n_groups + 1) // 2, far_pair, 0)
    acc = acc_ref[...]
    ot = acc[0:AT_DIM, :] / acc[AT_DIM:AT_DIM + 1, :]
    o_ref[...] = jnp.concatenate(
        [ot[:, h * MOBA_BLOCK:(h + 1) * MOBA_BLOCK] for h in range(AT_HEADS)], axis=0
    ).T.astype(o_ref.dtype)


def _moba(rel_bias, q, k, vt, ksum, bias):
    s = q.shape[0]
    nblk = s // MOBA_BLOCK
    assert nblk >= N_BIAS_TABLES and FAR_GROUP <= N_BIAS_TABLES and nblk % FAR_GROUP == 0
    return pl.pallas_call(
        _moba_kernel,
        grid=(nblk,),
        in_specs=[
            pl.BlockSpec(memory_space=pltpu.SMEM),
            pl.BlockSpec((MOBA_BLOCK, AT_WIDTH), lambda i: (i, 0)),
            _const_spec((nblk, MOBA_BLOCK, AT_WIDTH)),
            _const_spec((nblk, AT_HEADS, V_ROWS, MOBA_BLOCK)),
            _const_spec((nblk * KSUM_ROWS, AT_WIDTH)),
            _const_spec((2 * N_BIAS_TABLES - 1, MOBA_BLOCK, AT_HEADS * MOBA_BLOCK)),
        ],
        out_specs=pl.BlockSpec((MOBA_BLOCK, AT_WIDTH), lambda i: (i, 0)),
        out_shape=jax.ShapeDtypeStruct((s, AT_WIDTH), BF16),
        scratch_shapes=[
            pltpu.VMEM((nblk, AT_WIDTH), F32),
            pltpu.VMEM((nblk, AT_HEADS * MOBA_BLOCK), F32),
            pltpu.VMEM((nblk, AT_HEADS * MOBA_BLOCK), F32),
            pltpu.VMEM((AT_WIDTH, AT_HEADS * MOBA_BLOCK), BF16),
            pltpu.VMEM((1, AT_HEADS * MOBA_BLOCK), F32),
            pltpu.VMEM((V_ROWS, AT_HEADS * MOBA_BLOCK), F32),
            pltpu.VMEM((N_BIAS_TABLES * MOBA_BLOCK, AT_HEADS * MOBA_BLOCK), F32),
            pltpu.VMEM((FAR_GROUP * MOBA_BLOCK, AT_HEADS * MOBA_BLOCK), F32),
        ],
        compiler_params=_params("arbitrary"),
        name="moba",
    )(rel_bias, q, k, vt, ksum, bias)


def _mix_rows(x, ohg, oat, zcv, uh, cw, w_ref, nw):
    bgate = zcv[:, 0:CV_WIDTH]
    u = zcv[:, CV_WIDTH:2 * CV_WIDTH] * zcv[:, 2 * CV_WIDTH:3 * CV_WIDTH]
    row = lax.broadcasted_iota(jnp.int32, u.shape, 0)
    u1 = jnp.where(row == 0, uh[7:8, :], pltpu.roll(u, 1, axis=0))
    u2 = jnp.where(row == 0, uh[6:7, :], jnp.where(row == 1, uh[7:8, :], pltpu.roll(u, 2, axis=0)))
    ocv = bgate * (cw[0:1, :] * u2 + cw[1:2, :] * u1 + cw[2:3, :] * u)
    h = jnp.dot(ohg, w_ref[0:HG_WIDTH, :], preferred_element_type=F32)
    h = h + jnp.dot(oat, w_ref[HG_WIDTH:HG_WIDTH + AT_WIDTH, :], preferred_element_type=F32)
    h = h + jnp.dot(ocv.astype(BF16), w_ref[HG_WIDTH + AT_WIDTH:D_MIX, :], preferred_element_type=F32)
    return x + _rms(h, nw)


def _mixffn_kernel(x_ref, ohg_ref, oat_ref, zcv_ref, halo_ref, cw_ref, wo_ref, nw_ref,
                   wg_ref, wu_ref, wd_ref, o_ref):
    halo = halo_ref[...]
    uh = halo[:, CV_WIDTH:2 * CV_WIDTH] * halo[:, 2 * CV_WIDTH:3 * CV_WIDTH]
    uh = jnp.where(pl.program_id(0) > 0, uh, 0.0)
    cw = cw_ref[...]
    for r0 in range(0, FFN_TILE, FFN_CHAIN):
        rows = slice(r0, r0 + FFN_CHAIN)
        if r0 > 0:
            prev = zcv_ref[r0 - 8:r0, :]
            uh = prev[:, CV_WIDTH:2 * CV_WIDTH] * prev[:, 2 * CV_WIDTH:3 * CV_WIDTH]
        y = _mix_rows(x_ref[rows, :], ohg_ref[rows, :], oat_ref[rows, :], zcv_ref[rows, :], uh, cw,
                      wo_ref, nw_ref[0:1, :])
        o_ref[rows, :] = _ffn_rows(y, nw_ref[1:2, :], nw_ref[2:3, :], wg_ref, wu_ref, wd_ref)


def _mixffn(x, ohg, oat, zcv, conv_w, w_out, nw3, wg, wu, wd, layer):
    s = x.shape[0]
    halo_blocks = FFN_TILE // 8
    return pl.pallas_call(
        _mixffn_kernel,
        grid=(s // FFN_TILE,),
        in_specs=[
            pl.BlockSpec((FFN_TILE, D_MODEL), lambda i: (i, 0)),
            pl.BlockSpec((FFN_TILE, HG_WIDTH), lambda i: (i, 0)),
            pl.BlockSpec((FFN_TILE, AT_WIDTH), lambda i: (i, 0)),
            pl.BlockSpec((FFN_TILE, 3 * CV_WIDTH), lambda i: (i, 0)),
            pl.BlockSpec((8, 3 * CV_WIDTH), lambda i: (jnp.maximum(i * halo_blocks - 1, 0), 0)),
            _const_spec((CV_KERNEL, CV_WIDTH)),
            _layer_spec((D_MIX, D_MODEL), layer),
            _const_spec((3, D_MODEL)),
            _layer_spec((D_MODEL, D_FF), layer),
            _layer_spec((D_MODEL, D_FF), layer),
            _layer_spec((D_FF, D_MODEL), layer),
        ],
        out_specs=pl.BlockSpec((FFN_TILE, D_MODEL), lambda i: (i, 0)),
        out_shape=jax.ShapeDtypeStruct((s, D_MODEL), F32),
        compiler_params=_params("parallel"),
        name="mixffn",
    )(x, ohg, oat, zcv, zcv, conv_w, w_out, nw3, wg, wu, wd)


def _hgmixffn_kernel(layer, x_ref, oat_ref, zcv_ref, halo_ref, cw_ref, wo_ref, nw_ref,
                     wg_ref, wu_ref, wd_ref, q_ref, f_ref, i_ref, g_ref, lb_ref, hnw_ref,
                     o_ref, ohg_ref, st_ref):
    t = pl.program_id(0)

    @pl.when(t == 0)
    def _():
        st_ref[...] = jnp.zeros_like(st_ref)
        ohg_ref[...] = jnp.zeros_like(ohg_ref)

    halo = halo_ref[...]
    uh = halo[:, CV_WIDTH:2 * CV_WIDTH] * halo[:, 2 * CV_WIDTH:3 * CV_WIDTH]
    uh = jnp.where(t > 1, uh, 0.0)
    y = _mix_rows(x_ref[...], ohg_ref[...], oat_ref[...], zcv_ref[...], uh, cw_ref[...],
                  wo_ref, nw_ref[0:1, :])
    _hgrn_tiles(layer, q_ref, f_ref, i_ref, g_ref, lb_ref, hnw_ref, ohg_ref, st_ref)
    o_ref[...] = _ffn_rows(y, nw_ref[1:2, :], nw_ref[2:3, :], wg_ref, wu_ref, wd_ref)


def _hgmixffn(x, zhg, oat, zcv, conv_w, w_out, nw3, wg, wu, wd, hg_lb, hg_nw, layer):
    s = x.shape[0]
    n = s // ROW_TILE
    depth = hg_lb.shape[0]
    halo_blocks = ROW_TILE // 8

    def prev(width):
        return pl.BlockSpec((ROW_TILE, width), lambda t: (jnp.maximum(t - 1, 0), 0))

    def cur(k):
        return pl.BlockSpec((ROW_TILE, HG_WIDTH), lambda t: (jnp.minimum(t, n - 1), k))

    return pl.pallas_call(
        functools.partial(_hgmixffn_kernel, layer),
        grid=(n + 1,),
        in_specs=[
            prev(D_MODEL),
            prev(AT_WIDTH),
            prev(3 * CV_WIDTH),
            pl.BlockSpec((8, 3 * CV_WIDTH),
                         lambda t: (jnp.maximum(jnp.maximum(t - 1, 0) * halo_blocks - 1, 0), 0)),
            _const_spec((CV_KERNEL, CV_WIDTH)),
            _layer_spec((D_MIX, D_MODEL), layer),
            _const_spec((3, D_MODEL)),
            _layer_spec((D_MODEL, D_FF), layer),
            _layer_spec((D_MODEL, D_FF), layer),
            _layer_spec((D_FF, D_MODEL), layer),
            cur(0), cur(1), cur(2), cur(3),
            _const_spec((depth, HG_WIDTH)),
            _const_spec((1, HG_DIM)),
        ],
        out_specs=prev(D_MODEL),
        out_shape=jax.ShapeDtypeStruct((s, D_MODEL), F32),
        scratch_shapes=[pltpu.VMEM((ROW_TILE, HG_WIDTH), BF16),
                        pltpu.VMEM((HG_HEADS, HG_DIM, HG_DIM), F32)],
        compiler_params=_params("arbitrary"),
        name="hgmixffn",
    )(x, oat, zcv, zcv, conv_w, w_out, nw3, wg, wu, wd, zhg, zhg, zhg, zhg, hg_lb, hg_nw)


def kernel(x, norm_w, ffn1_wg, ffn1_wu, ffn1_wd, mix_w_in, mix_w_out, hg_lb, hg_norm_w, conv_w,
           ffn2_wg, ffn2_wu, ffn2_wd, rel_bias):
    batch, seq, _ = x.shape
    depth = norm_w.shape[0]
    assert batch == 1 and seq % ROW_TILE == 0 and seq % MOBA_BLOCK == 0
    rel_bias = rel_bias.astype(F32)
    bias = _bias_tables(rel_bias)
    y = x.reshape(seq, D_MODEL)
    w1 = [w.astype(BF16) for w in (ffn1_wg, ffn1_wu, ffn1_wd)]
    w2 = [w.astype(BF16) for w in (ffn2_wg, ffn2_wu, ffn2_wd)]
    w_in = mix_w_in.astype(BF16)
    w_out = mix_w_out.astype(BF16)
    for l in range(depth):
        y = _ffn(y, norm_w[l, 0:2], *w1, l)
        zhg, q, k, vt, ksum, zcv = _inproj(y, norm_w[l, 2:3], w_in, l)
        oat = _moba(rel_bias, q, k, vt, ksum, bias)
        y = _hgmixffn(y, zhg, oat, zcv, conv_w[l], w_out, norm_w[l, 3:6], *w2,
                      hg_lb, hg_norm_w[l:l + 1], l)
    return y.reshape(batch, seq, D_MODEL)
```

```python
import functools
import math

import numpy as np
import jax
import jax.numpy as jnp
from jax import lax
from jax.experimental import pallas as pl
from jax.experimental.pallas import tpu as pltpu

F32 = jnp.float32
BF16 = jnp.bfloat16

D_MODEL = 1024
D_FF = 2816
HG_HEADS = 4
HG_DIM = 128
HG_WIDTH = HG_HEADS * HG_DIM
AT_HEADS = 4
AT_DIM = 64
AT_WIDTH = AT_HEADS * AT_DIM
MOBA_BLOCK = 256
MOBA_TOPK = 3
REL_BUCKETS = 32
REL_MAX_DIST = 1024
CV_WIDTH = 256
CV_KERNEL = 3
D_MIX = HG_WIDTH + AT_WIDTH + CV_WIDTH
D_IN = 4 * HG_WIDTH + 3 * AT_WIDTH + 3 * CV_WIDTH
EPS = 1e-6

ROW_TILE = 512
FFN_TILE = 1024
FFN_CHAIN = 512
HG_TILE = 256
KSUM_ROWS = 8
V_ROWS = AT_DIM + 16
LOG2E = math.log2(math.e)
FAR_GROUP = 2
NEG = -1e30
VMEM_LIMIT = 56 * 1024 * 1024
FF_CHUNKS = ((0, 1536), (1536, 2816))

_NT = (((1,), (1,)), ((), ()))
_TN = (((0,), (0,)), ((), ()))


def _rms(x, w):
    ms = jnp.mean(x * x, axis=-1, keepdims=True)
    return x * lax.rsqrt(ms + EPS) * w


def _const_spec(shape):
    nd = len(shape)
    return pl.BlockSpec(shape, lambda *_: (0,) * nd, pipeline_mode=pl.Buffered(1))


def _layer_spec(shape, layer):
    nd = len(shape)
    return pl.BlockSpec((None,) + tuple(shape), lambda *_: (layer,) + (0,) * nd,
                        pipeline_mode=pl.Buffered(1))


def _params(*sem):
    return pltpu.CompilerParams(dimension_semantics=sem, vmem_limit_bytes=VMEM_LIMIT)


def _ffn_rows(x, nw_pre, nw_post, wg_ref, wu_ref, wd_ref):
    xn = _rms(x, nw_pre).astype(BF16)
    h = None
    for c0, c1 in FF_CHUNKS:
        g = jnp.dot(xn, wg_ref[:, c0:c1], preferred_element_type=F32)
        u = jnp.dot(xn, wu_ref[:, c0:c1], preferred_element_type=F32)
        a = (g * jax.nn.sigmoid(g) * u).astype(BF16)
        part = jnp.dot(a, wd_ref[c0:c1, :], preferred_element_type=F32)
        h = part if h is None else h + part
    return x + 0.5 * _rms(h, nw_post)


def _ffn_kernel(x_ref, nw_ref, wg_ref, wu_ref, wd_ref, o_ref):
    for r0 in range(0, FFN_TILE, FFN_CHAIN):
        rows = slice(r0, r0 + FFN_CHAIN)
        o_ref[rows, :] = _ffn_rows(x_ref[rows, :], nw_ref[0:1, :], nw_ref[1:2, :],
                                   wg_ref, wu_ref, wd_ref)


def _ffn(x, nw2, wg, wu, wd, layer):
    s = x.shape[0]
    return pl.pallas_call(
        _ffn_kernel,
        grid=(s // FFN_TILE,),
        in_specs=[
            pl.BlockSpec((FFN_TILE, D_MODEL), lambda i: (i, 0)),
            _const_spec((2, D_MODEL)),
            _layer_spec((D_MODEL, D_FF), layer),
            _layer_spec((D_MODEL, D_FF), layer),
            _layer_spec((D_FF, D_MODEL), layer),
        ],
        out_specs=pl.BlockSpec((FFN_TILE, D_MODEL), lambda i: (i, 0)),
        out_shape=jax.ShapeDtypeStruct((s, D_MODEL), F32),
        compiler_params=_params("parallel"),
        name="ffn",
    )(x, nw2, wg, wu, wd)


def _inproj_kernel(x_ref, nw_ref, w_ref, zhg_ref, q_ref, k_ref, vt_ref, ksum_ref, zcv_ref):
    xn = _rms(x_ref[...], nw_ref[...]).astype(BF16)
    c = 4 * HG_WIDTH
    zhg_ref[...] = jnp.dot(xn, w_ref[:, 0:c], preferred_element_type=F32)
    q_ref[...] = jnp.dot(xn, w_ref[:, c:c + AT_WIDTH], preferred_element_type=F32)
    k = jnp.dot(xn, w_ref[:, c + AT_WIDTH:c + 2 * AT_WIDTH], preferred_element_type=F32)
    v = jnp.dot(xn, w_ref[:, c + 2 * AT_WIDTH:c + 3 * AT_WIDTH], preferred_element_type=F32)
    c += 3 * AT_WIDTH
    zcv_ref[...] = jnp.dot(xn, w_ref[:, c:c + 3 * CV_WIDTH], preferred_element_type=F32)
    ones = jnp.ones((V_ROWS - AT_DIM, MOBA_BLOCK), F32)
    for b in range(ROW_TILE // MOBA_BLOCK):
        kb = k[b * MOBA_BLOCK:(b + 1) * MOBA_BLOCK, :]
        vbt = v[b * MOBA_BLOCK:(b + 1) * MOBA_BLOCK, :].T
        k_ref[b] = kb.astype(BF16)
        for h in range(AT_HEADS):
            vt_ref[b, h] = jnp.concatenate(
                [vbt[h * AT_DIM:(h + 1) * AT_DIM, :], ones], axis=0).astype(BF16)
        ksum_ref[b * KSUM_ROWS:(b + 1) * KSUM_ROWS, :] = jnp.sum(
            kb.reshape(MOBA_BLOCK // KSUM_ROWS, KSUM_ROWS, AT_WIDTH), axis=0)


def _inproj(x, nw, w_in, layer):
    s = x.shape[0]
    nblk = s // MOBA_BLOCK
    bpt = ROW_TILE // MOBA_BLOCK
    return pl.pallas_call(
        _inproj_kernel,
        grid=(s // ROW_TILE,),
        in_specs=[
            pl.BlockSpec((ROW_TILE, D_MODEL), lambda i: (i, 0)),
            _const_spec((1, D_MODEL)),
            _layer_spec((D_MODEL, D_IN), layer),
        ],
        out_specs=[
            pl.BlockSpec((ROW_TILE, 4 * HG_WIDTH), lambda i: (i, 0)),
            pl.BlockSpec((ROW_TILE, AT_WIDTH), lambda i: (i, 0)),
            pl.BlockSpec((bpt, MOBA_BLOCK, AT_WIDTH), lambda i: (i, 0, 0)),
            pl.BlockSpec((bpt, AT_HEADS, V_ROWS, MOBA_BLOCK), lambda i: (i, 0, 0, 0)),
            pl.BlockSpec((bpt * KSUM_ROWS, AT_WIDTH), lambda i: (i, 0)),
            pl.BlockSpec((ROW_TILE, 3 * CV_WIDTH), lambda i: (i, 0)),
        ],
        out_shape=[
            jax.ShapeDtypeStruct((s, 4 * HG_WIDTH), F32),
            jax.ShapeDtypeStruct((s, AT_WIDTH), F32),
            jax.ShapeDtypeStruct((nblk, MOBA_BLOCK, AT_WIDTH), BF16),
            jax.ShapeDtypeStruct((nblk, AT_HEADS, V_ROWS, MOBA_BLOCK), BF16),
            jax.ShapeDtypeStruct((nblk * KSUM_ROWS, AT_WIDTH), F32),
            jax.ShapeDtypeStruct((s, 3 * CV_WIDTH), F32),
        ],
        compiler_params=_params("parallel"),
        name="inproj",
    )(x, nw, w_in)


def _hg_levels():
    levels = []
    n = HG_TILE
    while n >= 2:
        levels.append(n)
        n //= 2
    return levels


def _hgrn_head(q, fp, v, gate, lb, nw, st_ref, row, masks):
    u = jnp.exp2(jnp.minimum(fp * -LOG2E, 126.0))
    r = 1.0 / (1.0 + u)
    logf = jnp.log2(lb + (1.0 - lb) * r)
    kk = (1.0 - lb) * (u * r)
    vb = v.astype(BF16)

    b = logf
    sh = 1
    while sh < 8:
        rolled = pltpu.roll(b, sh, axis=0)
        head = jnp.where(row[:8] >= sh, rolled[:8], 0.0)
        b = b + jnp.concatenate([head, rolled[8:]], axis=0)
        sh *= 2
    while sh < HG_TILE:
        b = jnp.concatenate([b[:sh], b[sh:] + b[:-sh]], axis=0)
        sh *= 2

    diag, same = masks
    scores = jnp.where(
        diag, lax.dot_general(q.astype(BF16), kk.astype(BF16), _NT, preferred_element_type=F32), 0.0)
    for n in _hg_levels():
        half = n // 2
        if half >= 8:
            qp, kp, same_up = [], [], []
            zero = jnp.zeros((half, HG_DIM), F32)
            for lo in range(0, HG_TILE, n):
                mid, hi = lo + half, lo + n
                bm = b[mid - 1:mid, :]
                qp.append(q[mid:hi] * jnp.exp2(b[mid:hi] - bm))
                kp += [kk[lo:mid] * jnp.exp2(bm - b[lo:mid]), zero]
                same_up.append(same[n][mid:hi] if n < HG_TILE else None)
            lvl = lax.dot_general(jnp.concatenate(qp, axis=0).astype(BF16),
                                  jnp.concatenate(kp, axis=0).astype(BF16), _NT,
                                  preferred_element_type=F32)
            pieces = []
            for blk, lo in enumerate(range(0, HG_TILE, n)):
                mid, hi = lo + half, lo + n
                part = lvl[blk * half:(blk + 1) * half]
                if n < HG_TILE:
                    part = jnp.where(same_up[blk], part, 0.0)
                pieces += [scores[lo:mid], scores[mid:hi] + part]
            scores = jnp.concatenate(pieces, axis=0)
            continue
        else:
            upper = (row & (n - 1)) >= half
            if n == 2:
                ex = jnp.exp2(jnp.where(upper, logf, 0.0))
            else:
                b3 = b.reshape(HG_TILE // 8, 8, HG_DIM)
                sub = lax.broadcasted_iota(jnp.int32, (HG_TILE // 8, 8, HG_DIM), 1)
                bm3 = None
                for lo in range(0, 8, n):
                    piece = jnp.broadcast_to(b3[:, lo + half - 1:lo + half, :], b3.shape)
                    bm3 = piece if bm3 is None else jnp.where(sub >= lo, piece, bm3)
                ex = jnp.exp2(-jnp.abs(b - bm3.reshape(HG_TILE, HG_DIM)))
            qs = jnp.where(upper, q * ex, 0.0)
            ks = jnp.where(upper, 0.0, kk * ex)
        lvl = lax.dot_general(qs.astype(BF16), ks.astype(BF16), _NT, preferred_element_type=F32)
        scores = scores + (lvl if n == HG_TILE else jnp.where(same[n], lvl, 0.0))

    st = st_ref[...]
    b_last = b[HG_TILE - 1:HG_TILE, :]
    o = jnp.dot(scores.astype(BF16), vb, preferred_element_type=F32)
    o = o + lax.dot_general((q * jnp.exp2(b)).astype(BF16), st.astype(BF16), _NT,
                            preferred_element_type=F32)
    kdec = (kk * jnp.exp2(b_last - b)).astype(BF16)
    st_ref[...] = st * jnp.exp2(b_last) + lax.dot_general(vb, kdec, _TN, preferred_element_type=F32)
    return _rms(o, nw) * (gate * jax.nn.sigmoid(gate))


def _hgrn_tiles(layer, q_ref, f_ref, i_ref, g_ref, lb_ref, nw_ref, o_ref, st_ref):
    lbraw = lb_ref[...]
    e = jnp.exp(lbraw - jnp.max(lbraw, axis=0, keepdims=True))
    soft = e / jnp.sum(e, axis=0, keepdims=True)
    lb = jnp.sum(soft[0:layer + 1, :], axis=0, keepdims=True) - soft[0:1, :]

    row = lax.broadcasted_iota(jnp.int32, (HG_TILE, HG_DIM), 0)
    ti = lax.broadcasted_iota(jnp.int32, (HG_TILE, HG_TILE), 0)
    si = lax.broadcasted_iota(jnp.int32, (HG_TILE, HG_TILE), 1)
    masks = (ti == si, {n: (ti & -n) == (si & -n) for n in _hg_levels() if n < HG_TILE})
    for r0 in range(0, o_ref.shape[0], HG_TILE):
        rows = slice(r0, r0 + HG_TILE)
        for h in range(HG_HEADS):
            cols = slice(h * HG_DIM, (h + 1) * HG_DIM)
            o_ref[rows, cols] = _hgrn_head(
                q_ref[rows, cols], f_ref[rows, cols], i_ref[rows, cols], g_ref[rows, cols],
                lb[:, cols], nw_ref[...], st_ref.at[h], row, masks).astype(o_ref.dtype)


N_BIAS_TABLES = 5


def _bucket_thresholds():
    max_exact = REL_BUCKETS // 2
    d = np.arange(1, 2 * REL_MAX_DIST, dtype=np.float64)
    large = max_exact + (np.log(d / max_exact) / math.log(REL_MAX_DIST / max_exact)
                         * (REL_BUCKETS - max_exact)).astype(np.int64)
    large = np.minimum(large, REL_BUCKETS - 1)
    bucket = np.where(d < max_exact, d.astype(np.int64), large)
    thr = [0] * REL_BUCKETS
    for bkt in range(1, REL_BUCKETS):
        thr[bkt] = int(d[np.argmax(bucket >= bkt)])
    assert (N_BIAS_TABLES - 1) * MOBA_BLOCK + 1 >= thr[REL_BUCKETS - 1]
    return thr


def _bias_kernel(rb_ref, o_ref):
    thr = _bucket_thresholds()
    key = lax.broadcasted_iota(jnp.int32, (MOBA_BLOCK, MOBA_BLOCK), 0)
    qry = lax.broadcasted_iota(jnp.int32, (MOBA_BLOCK, MOBA_BLOCK), 1)
    o_ref[N_BIAS_TABLES:] = jnp.zeros((N_BIAS_TABLES - 1,) + o_ref.shape[1:], F32)
    for u in range(N_BIAS_TABLES):
        t = N_BIAS_TABLES - 1 - u
        dist = qry - key + t * MOBA_BLOCK
        for h in range(AT_HEADS):
            val = jnp.full((MOBA_BLOCK, MOBA_BLOCK), rb_ref[REL_BUCKETS - 1, h], F32)
            for bkt in range(REL_BUCKETS - 2, -1, -1):
                val = jnp.where(dist < thr[bkt + 1], rb_ref[bkt, h], val)
            val = val * LOG2E
            if t == 0:
                val = jnp.where(dist < 0, NEG, val)
            o_ref[u, :, h * MOBA_BLOCK:(h + 1) * MOBA_BLOCK] = val


def _bias_tables(rel_bias):
    return pl.pallas_call(
        _bias_kernel,
        in_specs=[pl.BlockSpec(memory_space=pltpu.SMEM)],
        out_shape=jax.ShapeDtypeStruct((2 * N_BIAS_TABLES - 1, MOBA_BLOCK, AT_HEADS * MOBA_BLOCK), F32),
        name="moba_bias",
    )(rel_bias)


def _moba_kernel(rb_ref, q_ref, k_ref, vt_ref, ksum_ref, bias_ref, o_ref,
                 kmean_ref, mnear_ref, mfar_ref, qs_ref, m_ref, acc_ref, sa_ref, sb_ref):
    i = pl.program_id(0)
    nblk = k_ref.shape[0]

    @pl.when(i == 0)
    def _():
        ks = ksum_ref[...].reshape(nblk, KSUM_ROWS, AT_WIDTH)
        kmean_ref[...] = jnp.sum(ks, axis=1) * (1.0 / MOBA_BLOCK)

    qt = q_ref[...].T
    zero = jnp.zeros((AT_DIM, MOBA_BLOCK), F32)
    qmt = jnp.concatenate(
        [jnp.concatenate([qt[r * AT_DIM:(r + 1) * AT_DIM, :] if r == h else zero
                          for r in range(AT_HEADS)], axis=0) for h in range(AT_HEADS)], axis=1)
    qs_ref[...] = (qmt * (AT_DIM ** -0.5 * LOG2E)).astype(BF16)

    ncol = AT_HEADS * MOBA_BLOCK
    jio = lax.broadcasted_iota(jnp.int32, (nblk, ncol), 0).astype(F32)
    fi = i.astype(F32)
    n_far = jnp.maximum(i - (N_BIAS_TABLES - 1), 0)
    gate = jnp.dot(kmean_ref[...], qmt, precision=lax.Precision.HIGHEST,
                   preferred_element_type=F32)
    gate = jnp.where(jio < fi, gate, -jnp.inf)
    sel = jio == fi
    for _ in range(MOBA_TOPK):
        mx = jnp.max(gate, axis=0, keepdims=True)
        cand = jnp.where(gate == mx, jio, float(nblk))
        idx = jnp.min(cand, axis=0, keepdims=True)
        pick = (jio == idx) & (mx > -jnp.inf)
        sel = sel | pick
        gate = jnp.where(pick, -jnp.inf, gate)
    col_head = lax.broadcasted_iota(jnp.int32, (1, ncol), 1) // MOBA_BLOCK
    far_bias = jnp.zeros((1, ncol), F32)
    for h in range(AT_HEADS):
        far_bias = jnp.where(col_head == h, rb_ref[REL_BUCKETS - 1, h] * LOG2E, far_bias)
    mnear_ref[...] = jnp.where(sel, 0.0, NEG)
    mfar_ref[...] = jnp.where(sel & (jio < n_far.astype(F32)), far_bias, NEG)
    m_ref[...] = jnp.full(m_ref.shape, NEG, F32)
    acc_ref[...] = jnp.zeros(acc_ref.shape, F32)

    def logits_into(s_ref, j0, nb):
        kg = k_ref[pl.ds(j0, nb)].reshape(nb * MOBA_BLOCK, AT_WIDTH)
        s_ref[0:nb * MOBA_BLOCK, :] = jnp.dot(kg, qs_ref[...], preferred_element_type=F32)

    def fold(s_ref, j0, nb, mask_ref, table):
        for h in range(AT_HEADS):
            cols = slice(h * MOBA_BLOCK, (h + 1) * MOBA_BLOCK)
            m = m_ref[:, cols]
            acc = acc_ref[:, cols]
            for g in range(nb):
                rows = slice(g * MOBA_BLOCK, (g + 1) * MOBA_BLOCK)
                mrow = mask_ref[pl.ds(j0 + g, 1), cols]
                s = s_ref[rows, cols]
                if table is not None:
                    s = s + mrow + bias_ref[table + g, :, cols]
                mb = jnp.max(s, axis=0, keepdims=True)
                p = jnp.exp2(s - mb).astype(BF16)
                if table is None:
                    mb = mb + mrow
                pv = jnp.dot(vt_ref[j0 + g, h], p, preferred_element_type=F32)
                m_new = jnp.maximum(m, mb)
                acc = jnp.exp2(m - m_new) * acc + jnp.exp2(mb - m_new) * pv
                m = m_new
            m_ref[:, cols] = m
            acc_ref[:, cols] = acc

    near0 = jnp.maximum(i - (N_BIAS_TABLES - 1), 0)
    n_groups = (n_far + FAR_GROUP - 1) // FAR_GROUP
    last_group = nblk // FAR_GROUP - 1
    logits_into(sa_ref, near0, N_BIAS_TABLES)
    logits_into(sb_ref, 0, FAR_GROUP)
    fold(sa_ref, near0, N_BIAS_TABLES, mnear_ref, near0 - (i - (N_BIAS_TABLES - 1)))

    def far_pair(gp, carry):
        ga = 2 * gp
        logits_into(sa_ref, FAR_GROUP * (ga + 1), FAR_GROUP)
        fold(sb_ref, FAR_GROUP * ga, FAR_GROUP, mfar_ref, None)
        logits_into(sb_ref, FAR_GROUP * jnp.minimum(ga + 2, last_group), FAR_GROUP)
        fold(sa_ref, FAR_GROUP * (ga + 1), FAR_GROUP, mfar_ref, None)
        return carry

    lax.fori_loop(0, (n_groups + 1) // 2, far_pair, 0)
    acc = acc_ref[...]
    ot = acc[0:AT_DIM, :] / acc[AT_DIM:AT_DIM + 1, :]
    o_ref[...] = jnp.concatenate(
        [ot[:, h * MOBA_BLOCK:(h + 1) * MOBA_BLOCK] for h in range(AT_HEADS)], axis=0
    ).T.astype(o_ref.dtype)


def _moba(rel_bias, q, k, vt, ksum, bias):
    s = q.shape[0]
    nblk = s // MOBA_BLOCK
    assert nblk >= N_BIAS_TABLES and FAR_GROUP <= N_BIAS_TABLES and nblk % FAR_GROUP == 0
    return pl.pallas_call(
        _moba_kernel,
        grid=(nblk,),
        in_specs=[
            pl.BlockSpec(memory_space=pltpu.SMEM),
            pl.BlockSpec((MOBA_BLOCK, AT_WIDTH), lambda i: (i, 0)),
            _const_spec((nblk, MOBA_BLOCK, AT_WIDTH)),
            _const_spec((nblk, AT_HEADS, V_ROWS, MOBA_BLOCK)),
            _const_spec((nblk * KSUM_ROWS, AT_WIDTH)),
            _const_spec((2 * N_BIAS_TABLES - 1, MOBA_BLOCK, AT_HEADS * MOBA_BLOCK)),
        ],
        out_specs=pl.BlockSpec((MOBA_BLOCK, AT_WIDTH), lambda i: (i, 0)),
        out_shape=jax.ShapeDtypeStruct((s, AT_WIDTH), BF16),
        scratch_shapes=[
            pltpu.VMEM((nblk, AT_WIDTH), F32),
            pltpu.VMEM((nblk, AT_HEADS * MOBA_BLOCK), F32),
            pltpu.VMEM((nblk, AT_HEADS * MOBA_BLOCK), F32),
            pltpu.VMEM((AT_WIDTH, AT_HEADS * MOBA_BLOCK), BF16),
            pltpu.VMEM((1, AT_HEADS * MOBA_BLOCK), F32),
            pltpu.VMEM((V_ROWS, AT_HEADS * MOBA_BLOCK), F32),
            pltpu.VMEM((N_BIAS_TABLES * MOBA_BLOCK, AT_HEADS * MOBA_BLOCK), F32),
            pltpu.VMEM((FAR_GROUP * MOBA_BLOCK, AT_HEADS * MOBA_BLOCK), F32),
        ],
        compiler_params=_params("arbitrary"),
        name="moba",
    )(rel_bias, q, k, vt, ksum, bias)


def _mix_rows(x, ohg, oat, zcv, uh, cw, w_ref, nw):
    bgate = zcv[:, 0:CV_WIDTH]
    u = zcv[:, CV_WIDTH:2 * CV_WIDTH] * zcv[:, 2 * CV_WIDTH:3 * CV_WIDTH]
    row = lax.broadcasted_iota(jnp.int32, u.shape, 0)
    u1 = jnp.where(row == 0, uh[7:8, :], pltpu.roll(u, 1, axis=0))
    u2 = jnp.where(row == 0, uh[6:7, :], jnp.where(row == 1, uh[7:8, :], pltpu.roll(u, 2, axis=0)))
    ocv = bgate * (cw[0:1, :] * u2 + cw[1:2, :] * u1 + cw[2:3, :] * u)
    h = jnp.dot(ohg, w_ref[0:HG_WIDTH, :], preferred_element_type=F32)
    h = h + jnp.dot(oat, w_ref[HG_WIDTH:HG_WIDTH + AT_WIDTH, :], preferred_element_type=F32)
    h = h + jnp.dot(ocv.astype(BF16), w_ref[HG_WIDTH + AT_WIDTH:D_MIX, :], preferred_element_type=F32)
    return x + _rms(h, nw)


def _hgmixffn_kernel(layer, x_ref, oat_ref, zcv_ref, halo_ref, cw_ref, wo_ref, nw_ref,
                     wg_ref, wu_ref, wd_ref, q_ref, f_ref, i_ref, g_ref, lb_ref, hnw_ref,
                     o_ref, ohg_ref, st_ref):
    t = pl.program_id(0)

    @pl.when(t == 0)
    def _():
        st_ref[...] = jnp.zeros_like(st_ref)
        ohg_ref[...] = jnp.zeros_like(ohg_ref)

    halo = halo_ref[...]
    uh = halo[:, CV_WIDTH:2 * CV_WIDTH] * halo[:, 2 * CV_WIDTH:3 * CV_WIDTH]
    uh = jnp.where(t > 1, uh, 0.0)
    y = _mix_rows(x_ref[...], ohg_ref[...], oat_ref[...], zcv_ref[...], uh, cw_ref[...],
                  wo_ref, nw_ref[0:1, :])
    _hgrn_tiles(layer, q_ref, f_ref, i_ref, g_ref, lb_ref, hnw_ref, ohg_ref, st_ref)
    o_ref[...] = _ffn_rows(y, nw_ref[1:2, :], nw_ref[2:3, :], wg_ref, wu_ref, wd_ref)


def _hgmixffn(x, zhg, oat, zcv, conv_w, w_out, nw3, wg, wu, wd, hg_lb, hg_nw, layer):
    s = x.shape[0]
    n = s // ROW_TILE
    depth = hg_lb.shape[0]
    halo_blocks = ROW_TILE // 8

    def prev(width):
        return pl.BlockSpec((ROW_TILE, width), lambda t: (jnp.maximum(t - 1, 0), 0))

    def cur(k):
        return pl.BlockSpec((ROW_TILE, HG_WIDTH), lambda t: (jnp.minimum(t, n - 1), k))

    return pl.pallas_call(
        functools.partial(_hgmixffn_kernel, layer),
        grid=(n + 1,),
        in_specs=[
            prev(D_MODEL),
            prev(AT_WIDTH),
            prev(3 * CV_WIDTH),
            pl.BlockSpec((8, 3 * CV_WIDTH),
                         lambda t: (jnp.maximum(jnp.maximum(t - 1, 0) * halo_blocks - 1, 0), 0)),
            _const_spec((CV_KERNEL, CV_WIDTH)),
            _layer_spec((D_MIX, D_MODEL), layer),
            _const_spec((3, D_MODEL)),
            _layer_spec((D_MODEL, D_FF), layer),
            _layer_spec((D_MODEL, D_FF), layer),
            _layer_spec((D_FF, D_MODEL), layer),
            cur(0), cur(1), cur(2), cur(3),
            _const_spec((depth, HG_WIDTH)),
            _const_spec((1, HG_DIM)),
        ],
        out_specs=prev(D_MODEL),
        out_shape=jax.ShapeDtypeStruct((s, D_MODEL), F32),
        scratch_shapes=[pltpu.VMEM((ROW_TILE, HG_WIDTH), BF16),
                        pltpu.VMEM((HG_HEADS, HG_DIM, HG_DIM), F32)],
        compiler_params=_params("arbitrary"),
        name="hgmixffn",
    )(x, oat, zcv, zcv, conv_w, w_out, nw3, wg, wu, wd, zhg, zhg, zhg, zhg, hg_lb, hg_nw)


def kernel(x, norm_w, ffn1_wg, ffn1_wu, ffn1_wd, mix_w_in, mix_w_out, hg_lb, hg_norm_w, conv_w,
           ffn2_wg, ffn2_wu, ffn2_wd, rel_bias):
    batch, seq, _ = x.shape
    depth = norm_w.shape[0]
    assert batch == 1 and seq % ROW_TILE == 0 and seq % MOBA_BLOCK == 0
    rel_bias = rel_bias.astype(F32)
    bias = _bias_tables(rel_bias)
    y = x.reshape(seq, D_MODEL)
    w1 = [w.astype(BF16) for w in (ffn1_wg, ffn1_wu, ffn1_wd)]
    w2 = [w.astype(BF16) for w in (ffn2_wg, ffn2_wu, ffn2_wd)]
    w_in = mix_w_in.astype(BF16)
    w_out = mix_w_out.astype(BF16)
    for l in range(depth):
        y = _ffn(y, norm_w[l, 0:2], *w1, l)
        zhg, q, k, vt, ksum, zcv = _inproj(y, norm_w[l, 2:3], w_in, l)
        oat = _moba(rel_bias, q, k, vt, ksum, bias)
        y = _hgmixffn(y, zhg, oat, zcv, conv_w[l], w_out, norm_w[l, 3:6], *w2,
                      hg_lb, hg_norm_w[l:l + 1], l)
    return y.reshape(batch, seq, D_MODEL)
```

```python
import functools
import math

import numpy as np
import jax
import jax.numpy as jnp
from jax import lax
from jax.experimental import pallas as pl
from jax.experimental.pallas import tpu as pltpu

F32 = jnp.float32
BF16 = jnp.bfloat16

D_MODEL = 1024
D_FF = 2816
HG_HEADS = 4
HG_DIM = 128
HG_WIDTH = HG_HEADS * HG_DIM
AT_HEADS = 4
AT_DIM = 64
AT_WIDTH = AT_HEADS * AT_DIM
MOBA_BLOCK = 256
MOBA_TOPK = 3
REL_BUCKETS = 32
REL_MAX_DIST = 1024
CV_WIDTH = 256
CV_KERNEL = 3
D_MIX = HG_WIDTH + AT_WIDTH + CV_WIDTH
D_IN = 4 * HG_WIDTH + 3 * AT_WIDTH + 3 * CV_WIDTH
EPS = 1e-6

ROW_TILE = 512
FFN_TILE = 1024
FFN_CHAIN = 512
HG_TILE = 256
KSUM_ROWS = 8
V_ROWS = AT_DIM + 16
LOG2E = math.log2(math.e)
FAR_GROUP = 2
NEG = -1e30
VMEM_LIMIT = 56 * 1024 * 1024
FF_CHUNKS = ((0, 1536), (1536, 2816))

_NT = (((1,), (1,)), ((), ()))
_TN = (((0,), (0,)), ((), ()))


def _rms(x, w):
    ms = jnp.mean(x * x, axis=-1, keepdims=True)
    return x * lax.rsqrt(ms + EPS) * w


def _const_spec(shape):
    nd = len(shape)
    return pl.BlockSpec(shape, lambda *_: (0,) * nd, pipeline_mode=pl.Buffered(1))


def _cast_specs(w, layer, steps):
    _, rows, cols = w.shape
    span = 1
    while rows % (steps // span) or (rows // (steps // span)) % 16:
        span *= 2
        assert span <= steps
    blk = rows // (steps // span)
    return (pl.BlockSpec((None, blk, cols), lambda i: (layer, i // span, 0)),
            pl.BlockSpec((blk, cols), lambda i: (i // span, 0)),
            jax.ShapeDtypeStruct((rows, cols), BF16))


def _params(*sem):
    return pltpu.CompilerParams(dimension_semantics=sem, vmem_limit_bytes=VMEM_LIMIT)


def _ffn_rows(x, nw_pre, nw_post, wg_ref, wu_ref, wd_ref):
    xn = _rms(x, nw_pre).astype(BF16)
    h = None
    for c0, c1 in FF_CHUNKS:
        g = jnp.dot(xn, wg_ref[:, c0:c1], preferred_element_type=F32)
        u = jnp.dot(xn, wu_ref[:, c0:c1], preferred_element_type=F32)
        a = (g * jax.nn.sigmoid(g) * u).astype(BF16)
        part = jnp.dot(a, wd_ref[c0:c1, :], preferred_element_type=F32)
        h = part if h is None else h + part
    return x + 0.5 * _rms(h, nw_post)


def _ffn_kernel(x_ref, nw_ref, wg_ref, wu_ref, wd_ref, o_ref):
    for r0 in range(0, FFN_TILE, FFN_CHAIN):
        rows = slice(r0, r0 + FFN_CHAIN)
        o_ref[rows, :] = _ffn_rows(x_ref[rows, :], nw_ref[0:1, :], nw_ref[1:2, :],
                                   wg_ref, wu_ref, wd_ref)


def _ffn(x, nw2, wg, wu, wd):
    s = x.shape[0]
    return pl.pallas_call(
        _ffn_kernel,
        grid=(s // FFN_TILE,),
        in_specs=[
            pl.BlockSpec((FFN_TILE, D_MODEL), lambda i: (i, 0)),
            _const_spec((2, D_MODEL)),
            _const_spec((D_MODEL, D_FF)),
            _const_spec((D_MODEL, D_FF)),
            _const_spec((D_FF, D_MODEL)),
        ],
        out_specs=pl.BlockSpec((FFN_TILE, D_MODEL), lambda i: (i, 0)),
        out_shape=jax.ShapeDtypeStruct((s, D_MODEL), F32),
        compiler_params=_params("parallel"),
        name="ffn",
    )(x, nw2, wg, wu, wd)


def _inproj_kernel(x_ref, nw_ref, w_ref, zhg_ref, q_ref, k_ref, vt_ref, ksum_ref, zcv_ref):
    xn = _rms(x_ref[...], nw_ref[...]).astype(BF16)
    c = 4 * HG_WIDTH
    zhg_ref[...] = jnp.dot(xn, w_ref[:, 0:c], preferred_element_type=F32)
    q_ref[...] = jnp.dot(xn, w_ref[:, c:c + AT_WIDTH], preferred_element_type=F32)
    k = jnp.dot(xn, w_ref[:, c + AT_WIDTH:c + 2 * AT_WIDTH], preferred_element_type=F32)
    v = jnp.dot(xn, w_ref[:, c + 2 * AT_WIDTH:c + 3 * AT_WIDTH], preferred_element_type=F32)
    c += 3 * AT_WIDTH
    zcv_ref[...] = jnp.dot(xn, w_ref[:, c:c + 3 * CV_WIDTH], preferred_element_type=F32)
    ones = jnp.ones((V_ROWS - AT_DIM, MOBA_BLOCK), F32)
    for b in range(ROW_TILE // MOBA_BLOCK):
        kb = k[b * MOBA_BLOCK:(b + 1) * MOBA_BLOCK, :]
        vbt = v[b * MOBA_BLOCK:(b + 1) * MOBA_BLOCK, :].T
        k_ref[b] = kb.astype(BF16)
        for h in range(AT_HEADS):
            vt_ref[b, h] = jnp.concatenate(
                [vbt[h * AT_DIM:(h + 1) * AT_DIM, :], ones], axis=0).astype(BF16)
        ksum_ref[b * KSUM_ROWS:(b + 1) * KSUM_ROWS, :] = jnp.sum(
            kb.reshape(MOBA_BLOCK // KSUM_ROWS, KSUM_ROWS, AT_WIDTH), axis=0)


def _inproj(x, nw, w_in):
    s = x.shape[0]
    nblk = s // MOBA_BLOCK
    bpt = ROW_TILE // MOBA_BLOCK
    return pl.pallas_call(
        _inproj_kernel,
        grid=(s // ROW_TILE,),
        in_specs=[
            pl.BlockSpec((ROW_TILE, D_MODEL), lambda i: (i, 0)),
            _const_spec((1, D_MODEL)),
            _const_spec((D_MODEL, D_IN)),
        ],
        out_specs=[
            pl.BlockSpec((ROW_TILE, 4 * HG_WIDTH), lambda i: (i, 0)),
            pl.BlockSpec((ROW_TILE, AT_WIDTH), lambda i: (i, 0)),
            pl.BlockSpec((bpt, MOBA_BLOCK, AT_WIDTH), lambda i: (i, 0, 0)),
            pl.BlockSpec((bpt, AT_HEADS, V_ROWS, MOBA_BLOCK), lambda i: (i, 0, 0, 0)),
            pl.BlockSpec((bpt * KSUM_ROWS, AT_WIDTH), lambda i: (i, 0)),
            pl.BlockSpec((ROW_TILE, 3 * CV_WIDTH), lambda i: (i, 0)),
        ],
        out_shape=[
            jax.ShapeDtypeStruct((s, 4 * HG_WIDTH), F32),
            jax.ShapeDtypeStruct((s, AT_WIDTH), F32),
            jax.ShapeDtypeStruct((nblk, MOBA_BLOCK, AT_WIDTH), BF16),
            jax.ShapeDtypeStruct((nblk, AT_HEADS, V_ROWS, MOBA_BLOCK), BF16),
            jax.ShapeDtypeStruct((nblk * KSUM_ROWS, AT_WIDTH), F32),
            jax.ShapeDtypeStruct((s, 3 * CV_WIDTH), F32),
        ],
        compiler_params=_params("parallel"),
        name="inproj",
    )(x, nw, w_in)


def _hg_levels():
    levels = []
    n = HG_TILE
    while n >= 2:
        levels.append(n)
        n //= 2
    return levels


def _hgrn_head(q, fp, v, gate, lb, nw, st_ref, row, masks):
    u = jnp.exp2(jnp.minimum(fp * -LOG2E, 126.0))
    r = 1.0 / (1.0 + u)
    logf = jnp.log2(lb + (1.0 - lb) * r)
    kk = (1.0 - lb) * (u * r)
    vb = v.astype(BF16)

    b = logf
    sh = 1
    while sh < 8:
        rolled = pltpu.roll(b, sh, axis=0)
        head = jnp.where(row[:8] >= sh, rolled[:8], 0.0)
        b = b + jnp.concatenate([head, rolled[8:]], axis=0)
        sh *= 2
    while sh < HG_TILE:
        b = jnp.concatenate([b[:sh], b[sh:] + b[:-sh]], axis=0)
        sh *= 2

    diag, same = masks
    scores = jnp.where(
        diag, lax.dot_general(q.astype(BF16), kk.astype(BF16), _NT, preferred_element_type=F32), 0.0)
    for n in _hg_levels():
        half = n // 2
        if half >= 8:
            qp, kp, same_up = [], [], []
            zero = jnp.zeros((half, HG_DIM), F32)
            for lo in range(0, HG_TILE, n):
                mid, hi = lo + half, lo + n
                bm = b[mid - 1:mid, :]
                qp.append(q[mid:hi] * jnp.exp2(b[mid:hi] - bm))
                kp += [kk[lo:mid] * jnp.exp2(bm - b[lo:mid]), zero]
                same_up.append(same[n][mid:hi] if n < HG_TILE else None)
            lvl = lax.dot_general(jnp.concatenate(qp, axis=0).astype(BF16),
                                  jnp.concatenate(kp, axis=0).astype(BF16), _NT,
                                  preferred_element_type=F32)
            pieces = []
            for blk, lo in enumerate(range(0, HG_TILE, n)):
                mid, hi = lo + half, lo + n
                part = lvl[blk * half:(blk + 1) * half]
                if n < HG_TILE:
                    part = jnp.where(same_up[blk], part, 0.0)
                pieces += [scores[lo:mid], scores[mid:hi] + part]
            scores = jnp.concatenate(pieces, axis=0)
            continue
        else:
            upper = (row & (n - 1)) >= half
            if n == 2:
                ex = jnp.exp2(jnp.where(upper, logf, 0.0))
            else:
                b3 = b.reshape(HG_TILE // 8, 8, HG_DIM)
                sub = lax.broadcasted_iota(jnp.int32, (HG_TILE // 8, 8, HG_DIM), 1)
                bm3 = None
                for lo in range(0, 8, n):
                    piece = jnp.broadcast_to(b3[:, lo + half - 1:lo + half, :], b3.shape)
                    bm3 = piece if bm3 is None else jnp.where(sub >= lo, piece, bm3)
                ex = jnp.exp2(-jnp.abs(b - bm3.reshape(HG_TILE, HG_DIM)))
            qs = jnp.where(upper, q * ex, 0.0)
            ks = jnp.where(upper, 0.0, kk * ex)
        lvl = lax.dot_general(qs.astype(BF16), ks.astype(BF16), _NT, preferred_element_type=F32)
        scores = scores + (lvl if n == HG_TILE else jnp.where(same[n], lvl, 0.0))

    st = st_ref[...]
    b_last = b[HG_TILE - 1:HG_TILE, :]
    o = jnp.dot(scores.astype(BF16), vb, preferred_element_type=F32)
    o = o + lax.dot_general((q * jnp.exp2(b)).astype(BF16), st.astype(BF16), _NT,
                            preferred_element_type=F32)
    kdec = (kk * jnp.exp2(b_last - b)).astype(BF16)
    st_ref[...] = st * jnp.exp2(b_last) + lax.dot_general(vb, kdec, _TN, preferred_element_type=F32)
    return _rms(o, nw) * (gate * jax.nn.sigmoid(gate))


def _hgrn_tiles(layer, q_ref, f_ref, i_ref, g_ref, lb_ref, nw_ref, o_ref, st_ref):
    lbraw = lb_ref[...]
    e = jnp.exp(lbraw - jnp.max(lbraw, axis=0, keepdims=True))
    soft = e / jnp.sum(e, axis=0, keepdims=True)
    lb = jnp.sum(soft[0:layer + 1, :], axis=0, keepdims=True) - soft[0:1, :]

    row = lax.broadcasted_iota(jnp.int32, (HG_TILE, HG_DIM), 0)
    ti = lax.broadcasted_iota(jnp.int32, (HG_TILE, HG_TILE), 0)
    si = lax.broadcasted_iota(jnp.int32, (HG_TILE, HG_TILE), 1)
    masks = (ti == si, {n: (ti & -n) == (si & -n) for n in _hg_levels() if n < HG_TILE})
    for r0 in range(0, o_ref.shape[0], HG_TILE):
        rows = slice(r0, r0 + HG_TILE)
        for h in range(HG_HEADS):
            cols = slice(h * HG_DIM, (h + 1) * HG_DIM)
            o_ref[rows, cols] = _hgrn_head(
                q_ref[rows, cols], f_ref[rows, cols], i_ref[rows, cols], g_ref[rows, cols],
                lb[:, cols], nw_ref[...], st_ref.at[h], row, masks).astype(o_ref.dtype)


N_BIAS_TABLES = 5


def _bucket_thresholds():
    max_exact = REL_BUCKETS // 2
    d = np.arange(1, 2 * REL_MAX_DIST, dtype=np.float64)
    large = max_exact + (np.log(d / max_exact) / math.log(REL_MAX_DIST / max_exact)
                         * (REL_BUCKETS - max_exact)).astype(np.int64)
    large = np.minimum(large, REL_BUCKETS - 1)
    bucket = np.where(d < max_exact, d.astype(np.int64), large)
    thr = [0] * REL_BUCKETS
    for bkt in range(1, REL_BUCKETS):
        thr[bkt] = int(d[np.argmax(bucket >= bkt)])
    assert (N_BIAS_TABLES - 1) * MOBA_BLOCK + 1 >= thr[REL_BUCKETS - 1]
    return thr


def _bias_kernel(rb_ref, o_ref):
    thr = _bucket_thresholds()
    key = lax.broadcasted_iota(jnp.int32, (MOBA_BLOCK, MOBA_BLOCK), 0)
    qry = lax.broadcasted_iota(jnp.int32, (MOBA_BLOCK, MOBA_BLOCK), 1)
    o_ref[N_BIAS_TABLES:] = jnp.zeros((N_BIAS_TABLES - 1,) + o_ref.shape[1:], F32)
    for u in range(N_BIAS_TABLES):
        t = N_BIAS_TABLES - 1 - u
        dist = qry - key + t * MOBA_BLOCK
        for h in range(AT_HEADS):
            val = jnp.full((MOBA_BLOCK, MOBA_BLOCK), rb_ref[REL_BUCKETS - 1, h], F32)
            for bkt in range(REL_BUCKETS - 2, -1, -1):
                val = jnp.where(dist < thr[bkt + 1], rb_ref[bkt, h], val)
            val = val * LOG2E
            if t == 0:
                val = jnp.where(dist < 0, NEG, val)
            o_ref[u, :, h * MOBA_BLOCK:(h + 1) * MOBA_BLOCK] = val


def _bias_tables(rel_bias):
    return pl.pallas_call(
        _bias_kernel,
        in_specs=[pl.BlockSpec(memory_space=pltpu.SMEM)],
        out_shape=jax.ShapeDtypeStruct((2 * N_BIAS_TABLES - 1, MOBA_BLOCK, AT_HEADS * MOBA_BLOCK), F32),
        name="moba_bias",
    )(rel_bias)


def _moba_kernel(n_cast, rb_ref, q_ref, k_ref, vt_ref, ksum_ref, bias_ref, *refs):
    cast_in, o_ref, cast_out = refs[:n_cast], refs[n_cast], refs[n_cast + 1:2 * n_cast + 1]
    kmean_ref, mnear_ref, mfar_ref, qs_ref, m_ref, acc_ref, sa_ref, sb_ref = refs[2 * n_cast + 1:]
    for src, dst in zip(cast_in, cast_out):
        dst[...] = src[...].astype(BF16)

    i = pl.program_id(0)
    nblk = k_ref.shape[0]

    @pl.when(i == 0)
    def _():
        ks = ksum_ref[...].reshape(nblk, KSUM_ROWS, AT_WIDTH)
        kmean_ref[...] = jnp.sum(ks, axis=1) * (1.0 / MOBA_BLOCK)

    qt = q_ref[...].T
    zero = jnp.zeros((AT_DIM, MOBA_BLOCK), F32)
    qmt = jnp.concatenate(
        [jnp.concatenate([qt[r * AT_DIM:(r + 1) * AT_DIM, :] if r == h else zero
                          for r in range(AT_HEADS)], axis=0) for h in range(AT_HEADS)], axis=1)
    qs_ref[...] = (qmt * (AT_DIM ** -0.5 * LOG2E)).astype(BF16)

    ncol = AT_HEADS * MOBA_BLOCK
    jio = lax.broadcasted_iota(jnp.int32, (nblk, ncol), 0).astype(F32)
    fi = i.astype(F32)
    n_far = jnp.maximum(i - (N_BIAS_TABLES - 1), 0)
    gate = jnp.dot(kmean_ref[...], qmt, precision=lax.Precision.HIGHEST,
                   preferred_element_type=F32)
    gate = jnp.where(jio < fi, gate, -jnp.inf)
    sel = jio == fi
    for _ in range(MOBA_TOPK):
        mx = jnp.max(gate, axis=0, keepdims=True)
        cand = jnp.where(gate == mx, jio, float(nblk))
        idx = jnp.min(cand, axis=0, keepdims=True)
        pick = (jio == idx) & (mx > -jnp.inf)
        sel = sel | pick
        gate = jnp.where(pick, -jnp.inf, gate)
    col_head = lax.broadcasted_iota(jnp.int32, (1, ncol), 1) // MOBA_BLOCK
    far_bias = jnp.zeros((1, ncol), F32)
    for h in range(AT_HEADS):
        far_bias = jnp.where(col_head == h, rb_ref[REL_BUCKETS - 1, h] * LOG2E, far_bias)
    mnear_ref[...] = jnp.where(sel, 0.0, NEG)
    mfar_ref[...] = jnp.where(sel & (jio < n_far.astype(F32)), far_bias, NEG)
    m_ref[...] = jnp.full(m_ref.shape, NEG, F32)
    acc_ref[...] = jnp.zeros(acc_ref.shape, F32)

    def logits_into(s_ref, j0, nb):
        kg = k_ref[pl.ds(j0, nb)].reshape(nb * MOBA_BLOCK, AT_WIDTH)
        s_ref[0:nb * MOBA_BLOCK, :] = jnp.dot(kg, qs_ref[...], preferred_element_type=F32)

    def fold(s_ref, j0, nb, mask_ref, table):
        for h in range(AT_HEADS):
            cols = slice(h * MOBA_BLOCK, (h + 1) * MOBA_BLOCK)
            m = m_ref[:, cols]
            acc = acc_ref[:, cols]
            for g in range(nb):
                rows = slice(g * MOBA_BLOCK, (g + 1) * MOBA_BLOCK)
                mrow = mask_ref[pl.ds(j0 + g, 1), cols]
                s = s_ref[rows, cols]
                if table is not None:
                    s = s + mrow + bias_ref[table + g, :, cols]
                mb = jnp.max(s, axis=0, keepdims=True)
                p = jnp.exp2(s - mb).astype(BF16)
                if table is None:
                    mb = mb + mrow
                pv = jnp.dot(vt_ref[j0 + g, h], p, preferred_element_type=F32)
                m_new = jnp.maximum(m, mb)
                acc = jnp.exp2(m - m_new) * acc + jnp.exp2(mb - m_new) * pv
                m = m_new
            m_ref[:, cols] = m
            acc_ref[:, cols] = acc

    near0 = jnp.maximum(i - (N_BIAS_TABLES - 1), 0)
    n_groups = (n_far + FAR_GROUP - 1) // FAR_GROUP
    last_group = nblk // FAR_GROUP - 1
    logits_into(sa_ref, near0, N_BIAS_TABLES)
    logits_into(sb_ref, 0, FAR_GROUP)
    fold(sa_ref, near0, N_BIAS_TABLES, mnear_ref, near0 - (i - (N_BIAS_TABLES - 1)))

    def far_pair(gp, carry):
        ga = 2 * gp
        logits_into(sa_ref, FAR_GROUP * (ga + 1), FAR_GROUP)
        fold(sb_ref, FAR_GROUP * ga, FAR_GROUP, mfar_ref, None)
        logits_into(sb_ref, FAR_GROUP * jnp.minimum(ga + 2, last_group), FAR_GROUP)
        fold(sa_ref, FAR_GROUP * (ga + 1), FAR_GROUP, mfar_ref, None)
        return carry

    lax.fori_loop(0, (n_groups + 1) // 2, far_pair, 0)
    acc = acc_ref[...]
    ot = acc[0:AT_DIM, :] / acc[AT_DIM:AT_DIM + 1, :]
    o_ref[...] = jnp.concatenate(
        [ot[:, h * MOBA_BLOCK:(h + 1) * MOBA_BLOCK] for h in range(AT_HEADS)], axis=0
    ).T.astype(o_ref.dtype)


def _moba(rel_bias, q, k, vt, ksum, bias, casts=()):
    s = q.shape[0]
    nblk = s // MOBA_BLOCK
    assert nblk >= N_BIAS_TABLES and FAR_GROUP <= N_BIAS_TABLES and nblk % FAR_GROUP == 0
    cast_specs = [_cast_specs(w, layer, nblk) for w, layer in casts]
    outs = pl.pallas_call(
        functools.partial(_moba_kernel, len(casts)),
        grid=(nblk,),
        in_specs=[
            pl.BlockSpec(memory_space=pltpu.SMEM),
            pl.BlockSpec((MOBA_BLOCK, AT_WIDTH), lambda i: (i, 0)),
            _const_spec((nblk, MOBA_BLOCK, AT_WIDTH)),
            _const_spec((nblk, AT_HEADS, V_ROWS, MOBA_BLOCK)),
            _const_spec((nblk * KSUM_ROWS, AT_WIDTH)),
            _const_spec((2 * N_BIAS_TABLES - 1, MOBA_BLOCK, AT_HEADS * MOBA_BLOCK)),
        ] + [c[0] for c in cast_specs],
        out_specs=[pl.BlockSpec((MOBA_BLOCK, AT_WIDTH), lambda i: (i, 0))] + [c[1] for c in cast_specs],
        out_shape=[jax.ShapeDtypeStruct((s, AT_WIDTH), BF16)] + [c[2] for c in cast_specs],
        scratch_shapes=[
            pltpu.VMEM((nblk, AT_WIDTH), F32),
            pltpu.VMEM((nblk, AT_HEADS * MOBA_BLOCK), F32),
            pltpu.VMEM((nblk, AT_HEADS * MOBA_BLOCK), F32),
            pltpu.VMEM((AT_WIDTH, AT_HEADS * MOBA_BLOCK), BF16),
            pltpu.VMEM((1, AT_HEADS * MOBA_BLOCK), F32),
            pltpu.VMEM((V_ROWS, AT_HEADS * MOBA_BLOCK), F32),
            pltpu.VMEM((N_BIAS_TABLES * MOBA_BLOCK, AT_HEADS * MOBA_BLOCK), F32),
            pltpu.VMEM((FAR_GROUP * MOBA_BLOCK, AT_HEADS * MOBA_BLOCK), F32),
        ],
        compiler_params=_params("arbitrary"),
        name="moba",
    )(rel_bias, q, k, vt, ksum, bias, *[w for w, _ in casts])
    return outs[0], outs[1:]


def _mix_rows(x, ohg, oat, zcv, uh, cw, w_ref, nw):
    bgate = zcv[:, 0:CV_WIDTH]
    u = zcv[:, CV_WIDTH:2 * CV_WIDTH] * zcv[:, 2 * CV_WIDTH:3 * CV_WIDTH]
    row = lax.broadcasted_iota(jnp.int32, u.shape, 0)
    u1 = jnp.where(row == 0, uh[7:8, :], pltpu.roll(u, 1, axis=0))
    u2 = jnp.where(row == 0, uh[6:7, :], jnp.where(row == 1, uh[7:8, :], pltpu.roll(u, 2, axis=0)))
    ocv = bgate * (cw[0:1, :] * u2 + cw[1:2, :] * u1 + cw[2:3, :] * u)
    h = jnp.dot(ohg, w_ref[0:HG_WIDTH, :], preferred_element_type=F32)
    h = h + jnp.dot(oat, w_ref[HG_WIDTH:HG_WIDTH + AT_WIDTH, :], preferred_element_type=F32)
    h = h + jnp.dot(ocv.astype(BF16), w_ref[HG_WIDTH + AT_WIDTH:D_MIX, :], preferred_element_type=F32)
    return x + _rms(h, nw)


def _hgmixffn_kernel(layer, x_ref, oat_ref, zcv_ref, halo_ref, cw_ref, wo_ref, nw_ref,
                     wg_ref, wu_ref, wd_ref, q_ref, f_ref, i_ref, g_ref, lb_ref, hnw_ref,
                     o_ref, ohg_ref, st_ref):
    t = pl.program_id(0)

    @pl.when(t == 0)
    def _():
        st_ref[...] = jnp.zeros_like(st_ref)
        ohg_ref[...] = jnp.zeros_like(ohg_ref)

    halo = halo_ref[...]
    uh = halo[:, CV_WIDTH:2 * CV_WIDTH] * halo[:, 2 * CV_WIDTH:3 * CV_WIDTH]
    uh = jnp.where(t > 1, uh, 0.0)
    y = _mix_rows(x_ref[...], ohg_ref[...], oat_ref[...], zcv_ref[...], uh, cw_ref[...],
                  wo_ref, nw_ref[0:1, :])
    _hgrn_tiles(layer, q_ref, f_ref, i_ref, g_ref, lb_ref, hnw_ref, ohg_ref, st_ref)
    o_ref[...] = _ffn_rows(y, nw_ref[1:2, :], nw_ref[2:3, :], wg_ref, wu_ref, wd_ref)


def _hgmixffn(x, zhg, oat, zcv, conv_w, w_out, nw3, wg, wu, wd, hg_lb, hg_nw, layer):
    s = x.shape[0]
    n = s // ROW_TILE
    depth = hg_lb.shape[0]
    halo_blocks = ROW_TILE // 8

    def prev(width):
        return pl.BlockSpec((ROW_TILE, width), lambda t: (jnp.maximum(t - 1, 0), 0))

    def cur(k):
        return pl.BlockSpec((ROW_TILE, HG_WIDTH), lambda t: (jnp.minimum(t, n - 1), k))

    return pl.pallas_call(
        functools.partial(_hgmixffn_kernel, layer),
        grid=(n + 1,),
        in_specs=[
            prev(D_MODEL),
            prev(AT_WIDTH),
            prev(3 * CV_WIDTH),
            pl.BlockSpec((8, 3 * CV_WIDTH),
                         lambda t: (jnp.maximum(jnp.maximum(t - 1, 0) * halo_blocks - 1, 0), 0)),
            _const_spec((CV_KERNEL, CV_WIDTH)),
            _const_spec((D_MIX, D_MODEL)),
            _const_spec((3, D_MODEL)),
            _const_spec((D_MODEL, D_FF)),
            _const_spec((D_MODEL, D_FF)),
            _const_spec((D_FF, D_MODEL)),
            cur(0), cur(1), cur(2), cur(3),
            _const_spec((depth, HG_WIDTH)),
            _const_spec((1, HG_DIM)),
        ],
        out_specs=prev(D_MODEL),
        out_shape=jax.ShapeDtypeStruct((s, D_MODEL), F32),
        scratch_shapes=[pltpu.VMEM((ROW_TILE, HG_WIDTH), BF16),
                        pltpu.VMEM((HG_HEADS, HG_DIM, HG_DIM), F32)],
        compiler_params=_params("arbitrary"),
        name="hgmixffn",
    )(x, oat, zcv, zcv, conv_w, w_out, nw3, wg, wu, wd, zhg, zhg, zhg, zhg, hg_lb, hg_nw)


def kernel(x, norm_w, ffn1_wg, ffn1_wu, ffn1_wd, mix_w_in, mix_w_out, hg_lb, hg_norm_w, conv_w,
           ffn2_wg, ffn2_wu, ffn2_wd, rel_bias):
    batch, seq, _ = x.shape
    depth = norm_w.shape[0]
    assert batch == 1 and seq % ROW_TILE == 0 and seq % MOBA_BLOCK == 0
    rel_bias = rel_bias.astype(F32)
    bias = _bias_tables(rel_bias)
    y = x.reshape(seq, D_MODEL)
    early = (ffn1_wg, ffn1_wu, ffn1_wd, mix_w_in)
    late = (mix_w_out, ffn2_wg, ffn2_wu, ffn2_wd)
    wb = {(id(w), 0): w[0].astype(BF16) for w in early}
    casts = [(w, 0) for w in late] + [(w, l) for l in range(1, depth) for w in early + late]
    for l in range(depth):
        def wl(w, l=l):
            return wb[(id(w), l)]

        y = _ffn(y, norm_w[l, 0:2], wl(ffn1_wg), wl(ffn1_wu), wl(ffn1_wd))
        zhg, q, k, vt, ksum, zcv = _inproj(y, norm_w[l, 2:3], wl(mix_w_in))
        oat, cast_out = _moba(rel_bias, q, k, vt, ksum, bias, casts if l == 0 else ())
        if l == 0:
            wb.update({(id(w), cl): o for (w, cl), o in zip(casts, cast_out)})
        y = _hgmixffn(y, zhg, oat, zcv, conv_w[l], wl(mix_w_out), norm_w[l, 3:6],
                      wl(ffn2_wg), wl(ffn2_wu), wl(ffn2_wd), hg_lb, hg_norm_w[l:l + 1], l)
    return y.reshape(batch, seq, D_MODEL)
```

```python
import functools
import math

import numpy as np
import jax
import jax.numpy as jnp
from jax import lax
from jax.experimental import pallas as pl
from jax.experimental.pallas import tpu as pltpu

F32 = jnp.float32
BF16 = jnp.bfloat16

D_MODEL = 1024
D_FF = 2816
HG_HEADS = 4
HG_DIM = 128
HG_WIDTH = HG_HEADS * HG_DIM
AT_HEADS = 4
AT_DIM = 64
AT_WIDTH = AT_HEADS * AT_DIM
MOBA_BLOCK = 256
MOBA_TOPK = 3
REL_BUCKETS = 32
REL_MAX_DIST = 1024
CV_WIDTH = 256
CV_KERNEL = 3
D_MIX = HG_WIDTH + AT_WIDTH + CV_WIDTH
D_IN = 4 * HG_WIDTH + 3 * AT_WIDTH + 3 * CV_WIDTH
EPS = 1e-6

ROW_TILE = 512
FFN_TILE = 1024
FFN_CHAIN = 512
HG_TILE = 256
KSUM_ROWS = 8
V_ROWS = AT_DIM + 16
LOG2E = math.log2(math.e)
FAR_GROUP = 2
NEG = -1e30
VMEM_LIMIT = 56 * 1024 * 1024
FF_CHUNKS = ((0, 1536), (1536, 2816))
HGFF_CHUNKS = tuple((c, c + 256) for c in range(0, 2816, 256))

_NT = (((1,), (1,)), ((), ()))
_TN = (((0,), (0,)), ((), ()))


def _rms(x, w):
    ms = jnp.mean(x * x, axis=-1, keepdims=True)
    return x * lax.rsqrt(ms + EPS) * w


def _const_spec(shape):
    nd = len(shape)
    return pl.BlockSpec(shape, lambda *_: (0,) * nd, pipeline_mode=pl.Buffered(1))


def _cast_specs(w, layer, steps):
    _, rows, cols = w.shape
    span = 1
    while rows % (steps // span) or (rows // (steps // span)) % 16:
        span *= 2
        assert span <= steps
    blk = rows // (steps // span)
    return (pl.BlockSpec((None, blk, cols), lambda i: (layer, i // span, 0)),
            pl.BlockSpec((blk, cols), lambda i: (i // span, 0)),
            jax.ShapeDtypeStruct((rows, cols), BF16))


def _params(*sem):
    return pltpu.CompilerParams(dimension_semantics=sem, vmem_limit_bytes=VMEM_LIMIT)


def _ffn_rows(x, nw_pre, nw_post, wg_ref, wu_ref, wd_ref):
    xn = _rms(x, nw_pre).astype(BF16)
    h = None
    for c0, c1 in FF_CHUNKS:
        g = jnp.dot(xn, wg_ref[:, c0:c1], preferred_element_type=F32)
        u = jnp.dot(xn, wu_ref[:, c0:c1], preferred_element_type=F32)
        a = (g * jax.nn.sigmoid(g) * u).astype(BF16)
        part = jnp.dot(a, wd_ref[c0:c1, :], preferred_element_type=F32)
        h = part if h is None else h + part
    return x + 0.5 * _rms(h, nw_post)


def _ffn_kernel(x_ref, nw_ref, wg_ref, wu_ref, wd_ref, o_ref):
    for r0 in range(0, FFN_TILE, FFN_CHAIN):
        rows = slice(r0, r0 + FFN_CHAIN)
        o_ref[rows, :] = _ffn_rows(x_ref[rows, :], nw_ref[0:1, :], nw_ref[1:2, :],
                                   wg_ref, wu_ref, wd_ref)


def _ffn(x, nw2, wg, wu, wd):
    s = x.shape[0]
    return pl.pallas_call(
        _ffn_kernel,
        grid=(s // FFN_TILE,),
        in_specs=[
            pl.BlockSpec((FFN_TILE, D_MODEL), lambda i: (i, 0)),
            _const_spec((2, D_MODEL)),
            _const_spec((D_MODEL, D_FF)),
            _const_spec((D_MODEL, D_FF)),
            _const_spec((D_FF, D_MODEL)),
        ],
        out_specs=pl.BlockSpec((FFN_TILE, D_MODEL), lambda i: (i, 0)),
        out_shape=jax.ShapeDtypeStruct((s, D_MODEL), F32),
        compiler_params=_params("parallel"),
        name="ffn",
    )(x, nw2, wg, wu, wd)


def _inproj_kernel(x_ref, nw_ref, w_ref, zhg_ref, q_ref, k_ref, vt_ref, ksum_ref, zcv_ref):
    xn = _rms(x_ref[...], nw_ref[...]).astype(BF16)
    c = 4 * HG_WIDTH
    zhg_ref[...] = jnp.dot(xn, w_ref[:, 0:c], preferred_element_type=F32)
    q_ref[...] = jnp.dot(xn, w_ref[:, c:c + AT_WIDTH], preferred_element_type=F32)
    k = jnp.dot(xn, w_ref[:, c + AT_WIDTH:c + 2 * AT_WIDTH], preferred_element_type=F32)
    v = jnp.dot(xn, w_ref[:, c + 2 * AT_WIDTH:c + 3 * AT_WIDTH], preferred_element_type=F32)
    c += 3 * AT_WIDTH
    zcv_ref[...] = jnp.dot(xn, w_ref[:, c:c + 3 * CV_WIDTH], preferred_element_type=F32)
    ones = jnp.ones((V_ROWS - AT_DIM, MOBA_BLOCK), F32)
    for b in range(ROW_TILE // MOBA_BLOCK):
        kb = k[b * MOBA_BLOCK:(b + 1) * MOBA_BLOCK, :]
        vbt = v[b * MOBA_BLOCK:(b + 1) * MOBA_BLOCK, :].T
        k_ref[b] = kb.astype(BF16)
        for h in range(AT_HEADS):
            vt_ref[b, h] = jnp.concatenate(
                [vbt[h * AT_DIM:(h + 1) * AT_DIM, :], ones], axis=0).astype(BF16)
        ksum_ref[b * KSUM_ROWS:(b + 1) * KSUM_ROWS, :] = jnp.sum(
            kb.reshape(MOBA_BLOCK // KSUM_ROWS, KSUM_ROWS, AT_WIDTH), axis=0)


def _inproj(x, nw, w_in):
    s = x.shape[0]
    nblk = s // MOBA_BLOCK
    bpt = ROW_TILE // MOBA_BLOCK
    return pl.pallas_call(
        _inproj_kernel,
        grid=(s // ROW_TILE,),
        in_specs=[
            pl.BlockSpec((ROW_TILE, D_MODEL), lambda i: (i, 0)),
            _const_spec((1, D_MODEL)),
            _const_spec((D_MODEL, D_IN)),
        ],
        out_specs=[
            pl.BlockSpec((ROW_TILE, 4 * HG_WIDTH), lambda i: (i, 0)),
            pl.BlockSpec((ROW_TILE, AT_WIDTH), lambda i: (i, 0)),
            pl.BlockSpec((bpt, MOBA_BLOCK, AT_WIDTH), lambda i: (i, 0, 0)),
            pl.BlockSpec((bpt, AT_HEADS, V_ROWS, MOBA_BLOCK), lambda i: (i, 0, 0, 0)),
            pl.BlockSpec((bpt * KSUM_ROWS, AT_WIDTH), lambda i: (i, 0)),
            pl.BlockSpec((ROW_TILE, 3 * CV_WIDTH), lambda i: (i, 0)),
        ],
        out_shape=[
            jax.ShapeDtypeStruct((s, 4 * HG_WIDTH), F32),
            jax.ShapeDtypeStruct((s, AT_WIDTH), F32),
            jax.ShapeDtypeStruct((nblk, MOBA_BLOCK, AT_WIDTH), BF16),
            jax.ShapeDtypeStruct((nblk, AT_HEADS, V_ROWS, MOBA_BLOCK), BF16),
            jax.ShapeDtypeStruct((nblk * KSUM_ROWS, AT_WIDTH), F32),
            jax.ShapeDtypeStruct((s, 3 * CV_WIDTH), F32),
        ],
        compiler_params=_params("parallel"),
        name="inproj",
    )(x, nw, w_in)


def _hg_levels():
    levels = []
    n = HG_TILE
    while n >= 2:
        levels.append(n)
        n //= 2
    return levels


def _hgrn_head(q, fp, v, gate, lb, nw, st_ref, row, masks):
    u = jnp.exp2(jnp.minimum(fp * -LOG2E, 126.0))
    r = 1.0 / (1.0 + u)
    logf = jnp.log2(lb + (1.0 - lb) * r)
    kk = (1.0 - lb) * (u * r)
    vb = v.astype(BF16)

    b = logf
    sh = 1
    while sh < 8:
        rolled = pltpu.roll(b, sh, axis=0)
        head = jnp.where(row[:8] >= sh, rolled[:8], 0.0)
        b = b + jnp.concatenate([head, rolled[8:]], axis=0)
        sh *= 2
    while sh < HG_TILE:
        b = jnp.concatenate([b[:sh], b[sh:] + b[:-sh]], axis=0)
        sh *= 2

    diag, same = masks
    scores = jnp.where(
        diag, lax.dot_general(q.astype(BF16), kk.astype(BF16), _NT, preferred_element_type=F32), 0.0)
    for n in _hg_levels():
        half = n // 2
        if half >= 8:
            qp, kp, same_up = [], [], []
            zero = jnp.zeros((half, HG_DIM), F32)
            for lo in range(0, HG_TILE, n):
                mid, hi = lo + half, lo + n
                bm = b[mid - 1:mid, :]
                qp.append(q[mid:hi] * jnp.exp2(b[mid:hi] - bm))
                kp += [kk[lo:mid] * jnp.exp2(bm - b[lo:mid]), zero]
                same_up.append(same[n][mid:hi] if n < HG_TILE else None)
            lvl = lax.dot_general(jnp.concatenate(qp, axis=0).astype(BF16),
                                  jnp.concatenate(kp, axis=0).astype(BF16), _NT,
                                  preferred_element_type=F32)
            pieces = []
            for blk, lo in enumerate(range(0, HG_TILE, n)):
                mid, hi = lo + half, lo + n
                part = lvl[blk * half:(blk + 1) * half]
                if n < HG_TILE:
                    part = jnp.where(same_up[blk], part, 0.0)
                pieces += [scores[lo:mid], scores[mid:hi] + part]
            scores = jnp.concatenate(pieces, axis=0)
            continue
        else:
            upper = (row & (n - 1)) >= half
            if n == 2:
                ex = jnp.exp2(jnp.where(upper, logf, 0.0))
            else:
                b3 = b.reshape(HG_TILE // 8, 8, HG_DIM)
                sub = lax.broadcasted_iota(jnp.int32, (HG_TILE // 8, 8, HG_DIM), 1)
                bm3 = None
                for lo in range(0, 8, n):
                    piece = jnp.broadcast_to(b3[:, lo + half - 1:lo + half, :], b3.shape)
                    bm3 = piece if bm3 is None else jnp.where(sub >= lo, piece, bm3)
                ex = jnp.exp2(-jnp.abs(b - bm3.reshape(HG_TILE, HG_DIM)))
            qs = jnp.where(upper, q * ex, 0.0)
            ks = jnp.where(upper, 0.0, kk * ex)
        lvl = lax.dot_general(qs.astype(BF16), ks.astype(BF16), _NT, preferred_element_type=F32)
        scores = scores + (lvl if n == HG_TILE else jnp.where(same[n], lvl, 0.0))

    st = st_ref[...]
    b_last = b[HG_TILE - 1:HG_TILE, :]
    o = jnp.dot(scores.astype(BF16), vb, preferred_element_type=F32)
    o = o + lax.dot_general((q * jnp.exp2(b)).astype(BF16), st.astype(BF16), _NT,
                            preferred_element_type=F32)
    kdec = (kk * jnp.exp2(b_last - b)).astype(BF16)
    st_ref[...] = st * jnp.exp2(b_last) + lax.dot_general(vb, kdec, _TN, preferred_element_type=F32)
    return _rms(o, nw) * (gate * jax.nn.sigmoid(gate))


def _hgrn_units(layer, q_ref, f_ref, i_ref, g_ref, lb_ref, nw_ref, o_ref, st_ref):
    lbraw = lb_ref[...]
    e = jnp.exp(lbraw - jnp.max(lbraw, axis=0, keepdims=True))
    soft = e / jnp.sum(e, axis=0, keepdims=True)
    lb = jnp.sum(soft[0:layer + 1, :], axis=0, keepdims=True) - soft[0:1, :]

    row = lax.broadcasted_iota(jnp.int32, (HG_TILE, HG_DIM), 0)
    ti = lax.broadcasted_iota(jnp.int32, (HG_TILE, HG_TILE), 0)
    si = lax.broadcasted_iota(jnp.int32, (HG_TILE, HG_TILE), 1)
    masks = (ti == si, {n: (ti & -n) == (si & -n) for n in _hg_levels() if n < HG_TILE})
    def unit(r0, h):
        rows = slice(r0, r0 + HG_TILE)
        cols = slice(h * HG_DIM, (h + 1) * HG_DIM)
        o_ref[rows, cols] = _hgrn_head(
            q_ref[rows, cols], f_ref[rows, cols], i_ref[rows, cols], g_ref[rows, cols],
            lb[:, cols], nw_ref[...], st_ref.at[h], row, masks).astype(o_ref.dtype)

    return [functools.partial(unit, r0, h)
            for r0 in range(0, o_ref.shape[0], HG_TILE) for h in range(HG_HEADS)]


N_BIAS_TABLES = 5


def _bucket_thresholds():
    max_exact = REL_BUCKETS // 2
    d = np.arange(1, 2 * REL_MAX_DIST, dtype=np.float64)
    large = max_exact + (np.log(d / max_exact) / math.log(REL_MAX_DIST / max_exact)
                         * (REL_BUCKETS - max_exact)).astype(np.int64)
    large = np.minimum(large, REL_BUCKETS - 1)
    bucket = np.where(d < max_exact, d.astype(np.int64), large)
    thr = [0] * REL_BUCKETS
    for bkt in range(1, REL_BUCKETS):
        thr[bkt] = int(d[np.argmax(bucket >= bkt)])
    assert (N_BIAS_TABLES - 1) * MOBA_BLOCK + 1 >= thr[REL_BUCKETS - 1]
    return thr


def _bias_kernel(rb_ref, o_ref):
    thr = _bucket_thresholds()
    key = lax.broadcasted_iota(jnp.int32, (MOBA_BLOCK, MOBA_BLOCK), 0)
    qry = lax.broadcasted_iota(jnp.int32, (MOBA_BLOCK, MOBA_BLOCK), 1)
    o_ref[N_BIAS_TABLES:] = jnp.zeros((N_BIAS_TABLES - 1,) + o_ref.shape[1:], F32)
    for u in range(N_BIAS_TABLES):
        t = N_BIAS_TABLES - 1 - u
        dist = qry - key + t * MOBA_BLOCK
        for h in range(AT_HEADS):
            val = jnp.full((MOBA_BLOCK, MOBA_BLOCK), rb_ref[REL_BUCKETS - 1, h], F32)
            for bkt in range(REL_BUCKETS - 2, -1, -1):
                val = jnp.where(dist < thr[bkt + 1], rb_ref[bkt, h], val)
            val = val * LOG2E
            if t == 0:
                val = jnp.where(dist < 0, NEG, val)
            o_ref[u, :, h * MOBA_BLOCK:(h + 1) * MOBA_BLOCK] = val


def _bias_tables(rel_bias):
    return pl.pallas_call(
        _bias_kernel,
        in_specs=[pl.BlockSpec(memory_space=pltpu.SMEM)],
        out_shape=jax.ShapeDtypeStruct((2 * N_BIAS_TABLES - 1, MOBA_BLOCK, AT_HEADS * MOBA_BLOCK), F32),
        name="moba_bias",
    )(rel_bias)


def _moba_kernel(n_cast, rb_ref, q_ref, k_ref, vt_ref, ksum_ref, bias_ref, *refs):
    cast_in, o_ref, cast_out = refs[:n_cast], refs[n_cast], refs[n_cast + 1:2 * n_cast + 1]
    kmean_ref, mnear_ref, mfar_ref, qs_ref, m_ref, acc_ref, sa_ref, sb_ref = refs[2 * n_cast + 1:]
    for src, dst in zip(cast_in, cast_out):
        dst[...] = src[...].astype(BF16)

    i = pl.program_id(0)
    nblk = k_ref.shape[0]

    @pl.when(i == 0)
    def _():
        ks = ksum_ref[...].reshape(nblk, KSUM_ROWS, AT_WIDTH)
        kmean_ref[...] = jnp.sum(ks, axis=1) * (1.0 / MOBA_BLOCK)

    qt = q_ref[...].T
    zero = jnp.zeros((AT_DIM, MOBA_BLOCK), F32)
    qmt = jnp.concatenate(
        [jnp.concatenate([qt[r * AT_DIM:(r + 1) * AT_DIM, :] if r == h else zero
                          for r in range(AT_HEADS)], axis=0) for h in range(AT_HEADS)], axis=1)
    qs_ref[...] = (qmt * (AT_DIM ** -0.5 * LOG2E)).astype(BF16)

    ncol = AT_HEADS * MOBA_BLOCK
    jio = lax.broadcasted_iota(jnp.int32, (nblk, ncol), 0).astype(F32)
    fi = i.astype(F32)
    n_far = jnp.maximum(i - (N_BIAS_TABLES - 1), 0)
    gate = jnp.dot(kmean_ref[...], qmt, precision=lax.Precision.HIGHEST,
                   preferred_element_type=F32)
    gate = jnp.where(jio < fi, gate, -jnp.inf)
    sel = jio == fi
    for _ in range(MOBA_TOPK):
        mx = jnp.max(gate, axis=0, keepdims=True)
        cand = jnp.where(gate == mx, jio, float(nblk))
        idx = jnp.min(cand, axis=0, keepdims=True)
        pick = (jio == idx) & (mx > -jnp.inf)
        sel = sel | pick
        gate = jnp.where(pick, -jnp.inf, gate)
    col_head = lax.broadcasted_iota(jnp.int32, (1, ncol), 1) // MOBA_BLOCK
    far_bias = jnp.zeros((1, ncol), F32)
    for h in range(AT_HEADS):
        far_bias = jnp.where(col_head == h, rb_ref[REL_BUCKETS - 1, h] * LOG2E, far_bias)
    mnear_ref[...] = jnp.where(sel, 0.0, NEG)
    mfar_ref[...] = jnp.where(sel & (jio < n_far.astype(F32)), far_bias, NEG)
    m_ref[...] = jnp.full(m_ref.shape, NEG, F32)
    acc_ref[...] = jnp.zeros(acc_ref.shape, F32)

    def logits_into(s_ref, j0, nb):
        kg = k_ref[pl.ds(j0, nb)].reshape(nb * MOBA_BLOCK, AT_WIDTH)
        s_ref[0:nb * MOBA_BLOCK, :] = jnp.dot(kg, qs_ref[...], preferred_element_type=F32)

    def fold(s_ref, j0, nb, mask_ref, table):
        for h in range(AT_HEADS):
            cols = slice(h * MOBA_BLOCK, (h + 1) * MOBA_BLOCK)
            m = m_ref[:, cols]
            acc = acc_ref[:, cols]
            for g in range(nb):
                rows = slice(g * MOBA_BLOCK, (g + 1) * MOBA_BLOCK)
                mrow = mask_ref[pl.ds(j0 + g, 1), cols]
                s = s_ref[rows, cols]
                if table is not None:
                    s = s + mrow + bias_ref[table + g, :, cols]
                mb = jnp.max(s, axis=0, keepdims=True)
                p = jnp.exp2(s - mb).astype(BF16)
                if table is None:
                    mb = mb + mrow
                pv = jnp.dot(vt_ref[j0 + g, h], p, preferred_element_type=F32)
                m_new = jnp.maximum(m, mb)
                acc = jnp.exp2(m - m_new) * acc + jnp.exp2(mb - m_new) * pv
                m = m_new
            m_ref[:, cols] = m
            acc_ref[:, cols] = acc

    near0 = jnp.maximum(i - (N_BIAS_TABLES - 1), 0)
    n_groups = (n_far + FAR_GROUP - 1) // FAR_GROUP
    last_group = nblk // FAR_GROUP - 1
    logits_into(sa_ref, near0, N_BIAS_TABLES)
    logits_into(sb_ref, 0, FAR_GROUP)
    fold(sa_ref, near0, N_BIAS_TABLES, mnear_ref, near0 - (i - (N_BIAS_TABLES - 1)))

    def far_pair(gp, carry):
        ga = 2 * gp
        logits_into(sa_ref, FAR_GROUP * (ga + 1), FAR_GROUP)
        fold(sb_ref, FAR_GROUP * ga, FAR_GROUP, mfar_ref, None)
        logits_into(sb_ref, FAR_GROUP * jnp.minimum(ga + 2, last_group), FAR_GROUP)
        fold(sa_ref, FAR_GROUP * (ga + 1), FAR_GROUP, mfar_ref, None)
        return carry

    lax.fori_loop(0, (n_groups + 1) // 2, far_pair, 0)
    acc = acc_ref[...]
    ot = acc[0:AT_DIM, :] / acc[AT_DIM:AT_DIM + 1, :]
    o_ref[...] = jnp.concatenate(
        [ot[:, h * MOBA_BLOCK:(h + 1) * MOBA_BLOCK] for h in range(AT_HEADS)], axis=0
    ).T.astype(o_ref.dtype)


def _moba(rel_bias, q, k, vt, ksum, bias, casts=()):
    s = q.shape[0]
    nblk = s // MOBA_BLOCK
    assert nblk >= N_BIAS_TABLES and FAR_GROUP <= N_BIAS_TABLES and nblk % FAR_GROUP == 0
    cast_specs = [_cast_specs(w, layer, nblk) for w, layer in casts]
    outs = pl.pallas_call(
        functools.partial(_moba_kernel, len(casts)),
        grid=(nblk,),
        in_specs=[
            pl.BlockSpec(memory_space=pltpu.SMEM),
            pl.BlockSpec((MOBA_BLOCK, AT_WIDTH), lambda i: (i, 0)),
            _const_spec((nblk, MOBA_BLOCK, AT_WIDTH)),
            _const_spec((nblk, AT_HEADS, V_ROWS, MOBA_BLOCK)),
            _const_spec((nblk * KSUM_ROWS, AT_WIDTH)),
            _const_spec((2 * N_BIAS_TABLES - 1, MOBA_BLOCK, AT_HEADS * MOBA_BLOCK)),
        ] + [c[0] for c in cast_specs],
        out_specs=[pl.BlockSpec((MOBA_BLOCK, AT_WIDTH), lambda i: (i, 0))] + [c[1] for c in cast_specs],
        out_shape=[jax.ShapeDtypeStruct((s, AT_WIDTH), BF16)] + [c[2] for c in cast_specs],
        scratch_shapes=[
            pltpu.VMEM((nblk, AT_WIDTH), F32),
            pltpu.VMEM((nblk, AT_HEADS * MOBA_BLOCK), F32),
            pltpu.VMEM((nblk, AT_HEADS * MOBA_BLOCK), F32),
            pltpu.VMEM((AT_WIDTH, AT_HEADS * MOBA_BLOCK), BF16),
            pltpu.VMEM((1, AT_HEADS * MOBA_BLOCK), F32),
            pltpu.VMEM((V_ROWS, AT_HEADS * MOBA_BLOCK), F32),
            pltpu.VMEM((N_BIAS_TABLES * MOBA_BLOCK, AT_HEADS * MOBA_BLOCK), F32),
            pltpu.VMEM((FAR_GROUP * MOBA_BLOCK, AT_HEADS * MOBA_BLOCK), F32),
        ],
        compiler_params=_params("arbitrary"),
        name="moba",
    )(rel_bias, q, k, vt, ksum, bias, *[w for w, _ in casts])
    return outs[0], outs[1:]


def _mix_rows(x, ohg, oat, zcv, uh, cw, w_ref, nw):
    bgate = zcv[:, 0:CV_WIDTH]
    u = zcv[:, CV_WIDTH:2 * CV_WIDTH] * zcv[:, 2 * CV_WIDTH:3 * CV_WIDTH]
    row = lax.broadcasted_iota(jnp.int32, u.shape, 0)
    u1 = jnp.where(row == 0, uh[7:8, :], pltpu.roll(u, 1, axis=0))
    u2 = jnp.where(row == 0, uh[6:7, :], jnp.where(row == 1, uh[7:8, :], pltpu.roll(u, 2, axis=0)))
    ocv = bgate * (cw[0:1, :] * u2 + cw[1:2, :] * u1 + cw[2:3, :] * u)
    h = jnp.dot(ohg, w_ref[0:HG_WIDTH, :], preferred_element_type=F32)
    h = h + jnp.dot(oat, w_ref[HG_WIDTH:HG_WIDTH + AT_WIDTH, :], preferred_element_type=F32)
    h = h + jnp.dot(ocv.astype(BF16), w_ref[HG_WIDTH + AT_WIDTH:D_MIX, :], preferred_element_type=F32)
    return x + _rms(h, nw)


def _hgmixffn_kernel(layer, x_ref, oat_ref, zcv_ref, halo_ref, cw_ref, wo_ref, nw_ref,
                     wg_ref, wu_ref, wd_ref, q_ref, f_ref, i_ref, g_ref, lb_ref, hnw_ref,
                     o_ref, ohg_ref, st_ref):
    t = pl.program_id(0)

    @pl.when(t == 0)
    def _():
        st_ref[...] = jnp.zeros_like(st_ref)
        ohg_ref[...] = jnp.zeros_like(ohg_ref)

    halo = halo_ref[...]
    uh = halo[:, CV_WIDTH:2 * CV_WIDTH] * halo[:, 2 * CV_WIDTH:3 * CV_WIDTH]
    uh = jnp.where(t > 1, uh, 0.0)
    y = _mix_rows(x_ref[...], ohg_ref[...], oat_ref[...], zcv_ref[...], uh, cw_ref[...],
                  wo_ref, nw_ref[0:1, :])
    units = _hgrn_units(layer, q_ref, f_ref, i_ref, g_ref, lb_ref, hnw_ref, ohg_ref, st_ref)
    per_chunk = -(-len(units) // len(HGFF_CHUNKS))
    xn = _rms(y, nw_ref[1:2, :]).astype(BF16)
    h = None
    for n_chunk, (c0, c1) in enumerate(HGFF_CHUNKS):
        for run in units[n_chunk * per_chunk:(n_chunk + 1) * per_chunk]:
            run()
        g = jnp.dot(xn, wg_ref[:, c0:c1], preferred_element_type=F32)
        u = jnp.dot(xn, wu_ref[:, c0:c1], preferred_element_type=F32)
        a = (g * jax.nn.sigmoid(g) * u).astype(BF16)
        part = jnp.dot(a, wd_ref[c0:c1, :], preferred_element_type=F32)
        h = part if h is None else h + part
    o_ref[...] = y + 0.5 * _rms(h, nw_ref[2:3, :])


def _hgmixffn(x, zhg, oat, zcv, conv_w, w_out, nw3, wg, wu, wd, hg_lb, hg_nw, layer):
    s = x.shape[0]
    n = s // ROW_TILE
    depth = hg_lb.shape[0]
    halo_blocks = ROW_TILE // 8

    def prev(width):
        return pl.BlockSpec((ROW_TILE, width), lambda t: (jnp.maximum(t - 1, 0), 0))

    def cur(k):
        return pl.BlockSpec((ROW_TILE, HG_WIDTH), lambda t: (jnp.minimum(t, n - 1), k))

    return pl.pallas_call(
        functools.partial(_hgmixffn_kernel, layer),
        grid=(n + 1,),
        in_specs=[
            prev(D_MODEL),
            prev(AT_WIDTH),
            prev(3 * CV_WIDTH),
            pl.BlockSpec((8, 3 * CV_WIDTH),
                         lambda t: (jnp.maximum(jnp.maximum(t - 1, 0) * halo_blocks - 1, 0), 0)),
            _const_spec((CV_KERNEL, CV_WIDTH)),
            _const_spec((D_MIX, D_MODEL)),
            _const_spec((3, D_MODEL)),
            _const_spec((D_MODEL, D_FF)),
            _const_spec((D_MODEL, D_FF)),
            _const_spec((D_FF, D_MODEL)),
            cur(0), cur(1), cur(2), cur(3),
            _const_spec((depth, HG_WIDTH)),
            _const_spec((1, HG_DIM)),
        ],
        out_specs=prev(D_MODEL),
        out_shape=jax.ShapeDtypeStruct((s, D_MODEL), F32),
        scratch_shapes=[pltpu.VMEM((ROW_TILE, HG_WIDTH), BF16),
                        pltpu.VMEM((HG_HEADS, HG_DIM, HG_DIM), F32)],
        compiler_params=_params("arbitrary"),
        name="hgmixffn",
    )(x, oat, zcv, zcv, conv_w, w_out, nw3, wg, wu, wd, zhg, zhg, zhg, zhg, hg_lb, hg_nw)


def kernel(x, norm_w, ffn1_wg, ffn1_wu, ffn1_wd, mix_w_in, mix_w_out, hg_lb, hg_norm_w, conv_w,
           ffn2_wg, ffn2_wu, ffn2_wd, rel_bias):
    batch, seq, _ = x.shape
    depth = norm_w.shape[0]
    assert batch == 1 and seq % ROW_TILE == 0 and seq % MOBA_BLOCK == 0
    rel_bias = rel_bias.astype(F32)
    bias = _bias_tables(rel_bias)
    y = x.reshape(seq, D_MODEL)
    early = (ffn1_wg, ffn1_wu, ffn1_wd, mix_w_in)
    late = (mix_w_out, ffn2_wg, ffn2_wu, ffn2_wd)
    wb = {(id(w), 0): w[0].astype(BF16) for w in early}
    casts = [(w, 0) for w in late] + [(w, l) for l in range(1, depth) for w in early + late]
    for l in range(depth):
        def wl(w, l=l):
            return wb[(id(w), l)]

        y = _ffn(y, norm_w[l, 0:2], wl(ffn1_wg), wl(ffn1_wu), wl(ffn1_wd))
        zhg, q, k, vt, ksum, zcv = _inproj(y, norm_w[l, 2:3], wl(mix_w_in))
        oat, cast_out = _moba(rel_bias, q, k, vt, ksum, bias, casts if l == 0 else ())
        if l == 0:
            wb.update({(id(w), cl): o for (w, cl), o in zip(casts, cast_out)})
        y = _hgmixffn(y, zhg, oat, zcv, conv_w[l], wl(mix_w_out), norm_w[l, 3:6],
                      wl(ffn2_wg), wl(ffn2_wu), wl(ffn2_wd), hg_lb, hg_norm_w[l:l + 1], l)
    return y.reshape(batch, seq, D_MODEL)
```

```python
import functools
import math

import numpy as np
import jax
import jax.numpy as jnp
from jax import lax
from jax.experimental import pallas as pl
from jax.experimental.pallas import tpu as pltpu

F32 = jnp.float32
BF16 = jnp.bfloat16

D_MODEL = 1024
D_FF = 2816
HG_HEADS = 4
HG_DIM = 128
HG_WIDTH = HG_HEADS * HG_DIM
AT_HEADS = 4
AT_DIM = 64
AT_WIDTH = AT_HEADS * AT_DIM
MOBA_BLOCK = 256
MOBA_TOPK = 3
REL_BUCKETS = 32
REL_MAX_DIST = 1024
CV_WIDTH = 256
CV_KERNEL = 3
D_MIX = HG_WIDTH + AT_WIDTH + CV_WIDTH
D_IN = 4 * HG_WIDTH + 3 * AT_WIDTH + 3 * CV_WIDTH
EPS = 1e-6

ROW_TILE = 512
FFN_TILE = 1024
FFN_CHAIN = 512
HG_TILE = 256
KSUM_ROWS = 8
V_ROWS = AT_DIM + 16
LOG2E = math.log2(math.e)
FAR_GROUP = 2
NEG = -1e30
VMEM_LIMIT = 56 * 1024 * 1024
FF_CHUNKS = ((0, 1536), (1536, 2816))
HGFF_CHUNKS = tuple((c, c + 256) for c in range(0, 2816, 256))

_NT = (((1,), (1,)), ((), ()))
_TN = (((0,), (0,)), ((), ()))


def _rms(x, w):
    ms = jnp.mean(x * x, axis=-1, keepdims=True)
    return x * lax.rsqrt(ms + EPS) * w


def _const_spec(shape):
    nd = len(shape)
    return pl.BlockSpec(shape, lambda *_: (0,) * nd, pipeline_mode=pl.Buffered(1))


def _cast_specs(w, layer, steps):
    _, rows, cols = w.shape
    span = 1
    while rows % (steps // span) or (rows // (steps // span)) % 16:
        span *= 2
        assert span <= steps
    blk = rows // (steps // span)
    return (pl.BlockSpec((None, blk, cols), lambda i: (layer, i // span, 0)),
            pl.BlockSpec((blk, cols), lambda i: (i // span, 0)),
            jax.ShapeDtypeStruct((rows, cols), BF16))


def _params(*sem):
    return pltpu.CompilerParams(dimension_semantics=sem, vmem_limit_bytes=VMEM_LIMIT)


def _ffn_rows(x, nw_pre, nw_post, wg_ref, wu_ref, wd_ref):
    xn = _rms(x, nw_pre).astype(BF16)
    h = None
    for c0, c1 in FF_CHUNKS:
        g = jnp.dot(xn, wg_ref[:, c0:c1], preferred_element_type=F32)
        u = jnp.dot(xn, wu_ref[:, c0:c1], preferred_element_type=F32)
        a = (g * jax.nn.sigmoid(g) * u).astype(BF16)
        part = jnp.dot(a, wd_ref[c0:c1, :], preferred_element_type=F32)
        h = part if h is None else h + part
    return x + 0.5 * _rms(h, nw_post)


def _ffn_kernel(x_ref, nw_ref, wg_ref, wu_ref, wd_ref, o_ref):
    for r0 in range(0, FFN_TILE, FFN_CHAIN):
        rows = slice(r0, r0 + FFN_CHAIN)
        o_ref[rows, :] = _ffn_rows(x_ref[rows, :], nw_ref[0:1, :], nw_ref[1:2, :],
                                   wg_ref, wu_ref, wd_ref)


def _ffn(x, nw2, wg, wu, wd):
    s = x.shape[0]
    return pl.pallas_call(
        _ffn_kernel,
        grid=(s // FFN_TILE,),
        in_specs=[
            pl.BlockSpec((FFN_TILE, D_MODEL), lambda i: (i, 0)),
            _const_spec((2, D_MODEL)),
            _const_spec((D_MODEL, D_FF)),
            _const_spec((D_MODEL, D_FF)),
            _const_spec((D_FF, D_MODEL)),
        ],
        out_specs=pl.BlockSpec((FFN_TILE, D_MODEL), lambda i: (i, 0)),
        out_shape=jax.ShapeDtypeStruct((s, D_MODEL), F32),
        compiler_params=_params("parallel"),
        name="ffn",
    )(x, nw2, wg, wu, wd)


def _inproj_kernel(x_ref, nw_ref, w_ref, zhg_ref, q_ref, k_ref, vt_ref, ksum_ref, zcv_ref):
    xn = _rms(x_ref[...], nw_ref[...]).astype(BF16)
    c = 4 * HG_WIDTH
    zhg_ref[...] = jnp.dot(xn, w_ref[:, 0:c], preferred_element_type=F32)
    q_ref[...] = jnp.dot(xn, w_ref[:, c:c + AT_WIDTH], preferred_element_type=F32)
    k = jnp.dot(xn, w_ref[:, c + AT_WIDTH:c + 2 * AT_WIDTH], preferred_element_type=F32)
    v = jnp.dot(xn, w_ref[:, c + 2 * AT_WIDTH:c + 3 * AT_WIDTH], preferred_element_type=F32)
    c += 3 * AT_WIDTH
    zcv_ref[...] = jnp.dot(xn, w_ref[:, c:c + 3 * CV_WIDTH], preferred_element_type=F32)
    ones = jnp.ones((V_ROWS - AT_DIM, MOBA_BLOCK), F32)
    for b in range(ROW_TILE // MOBA_BLOCK):
        kb = k[b * MOBA_BLOCK:(b + 1) * MOBA_BLOCK, :]
        vbt = v[b * MOBA_BLOCK:(b + 1) * MOBA_BLOCK, :].T
        k_ref[b] = kb.astype(BF16)
        for h in range(AT_HEADS):
            vt_ref[b, h] = jnp.concatenate(
                [vbt[h * AT_DIM:(h + 1) * AT_DIM, :], ones], axis=0).astype(BF16)
        ksum_ref[b * KSUM_ROWS:(b + 1) * KSUM_ROWS, :] = jnp.sum(
            kb.reshape(MOBA_BLOCK // KSUM_ROWS, KSUM_ROWS, AT_WIDTH), axis=0)


def _inproj(x, nw, w_in):
    s = x.shape[0]
    nblk = s // MOBA_BLOCK
    bpt = ROW_TILE // MOBA_BLOCK
    return pl.pallas_call(
        _inproj_kernel,
        grid=(s // ROW_TILE,),
        in_specs=[
            pl.BlockSpec((ROW_TILE, D_MODEL), lambda i: (i, 0)),
            _const_spec((1, D_MODEL)),
            _const_spec((D_MODEL, D_IN)),
        ],
        out_specs=[
            pl.BlockSpec((ROW_TILE, 4 * HG_WIDTH), lambda i: (i, 0)),
            pl.BlockSpec((ROW_TILE, AT_WIDTH), lambda i: (i, 0)),
            pl.BlockSpec((bpt, MOBA_BLOCK, AT_WIDTH), lambda i: (i, 0, 0)),
            pl.BlockSpec((bpt, AT_HEADS, V_ROWS, MOBA_BLOCK), lambda i: (i, 0, 0, 0)),
            pl.BlockSpec((bpt * KSUM_ROWS, AT_WIDTH), lambda i: (i, 0)),
            pl.BlockSpec((ROW_TILE, 3 * CV_WIDTH), lambda i: (i, 0)),
        ],
        out_shape=[
            jax.ShapeDtypeStruct((s, 4 * HG_WIDTH), F32),
            jax.ShapeDtypeStruct((s, AT_WIDTH), F32),
            jax.ShapeDtypeStruct((nblk, MOBA_BLOCK, AT_WIDTH), BF16),
            jax.ShapeDtypeStruct((nblk, AT_HEADS, V_ROWS, MOBA_BLOCK), BF16),
            jax.ShapeDtypeStruct((nblk * KSUM_ROWS, AT_WIDTH), F32),
            jax.ShapeDtypeStruct((s, 3 * CV_WIDTH), F32),
        ],
        compiler_params=_params("parallel"),
        name="inproj",
    )(x, nw, w_in)


def _hg_levels():
    levels = []
    n = HG_TILE
    while n >= 2:
        levels.append(n)
        n //= 2
    return levels


def _hgrn_head(q, fp, v, gate, lb, nw, st_ref, row, masks):
    u = jnp.exp2(jnp.minimum(fp * -LOG2E, 126.0))
    r = 1.0 / (1.0 + u)
    logf = jnp.log2(lb + (1.0 - lb) * r)
    kk = (1.0 - lb) * (u * r)
    vb = v.astype(BF16)

    b = logf
    sh = 1
    while sh < 8:
        rolled = pltpu.roll(b, sh, axis=0)
        head = jnp.where(row[:8] >= sh, rolled[:8], 0.0)
        b = b + jnp.concatenate([head, rolled[8:]], axis=0)
        sh *= 2
    while sh < HG_TILE:
        b = jnp.concatenate([b[:sh], b[sh:] + b[:-sh]], axis=0)
        sh *= 2

    diag, same = masks
    scores = jnp.where(
        diag, lax.dot_general(q.astype(BF16), kk.astype(BF16), _NT, preferred_element_type=F32), 0.0)
    for n in _hg_levels():
        half = n // 2
        if half >= 8:
            qp, kp, same_up = [], [], []
            zero = jnp.zeros((half, HG_DIM), F32)
            for lo in range(0, HG_TILE, n):
                mid, hi = lo + half, lo + n
                bm = b[mid - 1:mid, :]
                qp.append(q[mid:hi] * jnp.exp2(b[mid:hi] - bm))
                kp += [kk[lo:mid] * jnp.exp2(bm - b[lo:mid]), zero]
                same_up.append(same[n][mid:hi] if n < HG_TILE else None)
            lvl = lax.dot_general(jnp.concatenate(qp, axis=0).astype(BF16),
                                  jnp.concatenate(kp, axis=0).astype(BF16), _NT,
                                  preferred_element_type=F32)
            pieces = []
            for blk, lo in enumerate(range(0, HG_TILE, n)):
                mid, hi = lo + half, lo + n
                part = lvl[blk * half:(blk + 1) * half]
                if n < HG_TILE:
                    part = jnp.where(same_up[blk], part, 0.0)
                pieces += [scores[lo:mid], scores[mid:hi] + part]
            scores = jnp.concatenate(pieces, axis=0)
            continue
        else:
            upper = (row & (n - 1)) >= half
            if n == 2:
                ex = jnp.exp2(jnp.where(upper, logf, 0.0))
            else:
                b3 = b.reshape(HG_TILE // 8, 8, HG_DIM)
                sub = lax.broadcasted_iota(jnp.int32, (HG_TILE // 8, 8, HG_DIM), 1)
                bm3 = None
                for lo in range(0, 8, n):
                    piece = jnp.broadcast_to(b3[:, lo + half - 1:lo + half, :], b3.shape)
                    bm3 = piece if bm3 is None else jnp.where(sub >= lo, piece, bm3)
                ex = jnp.exp2(-jnp.abs(b - bm3.reshape(HG_TILE, HG_DIM)))
            qs = jnp.where(upper, q * ex, 0.0)
            ks = jnp.where(upper, 0.0, kk * ex)
        lvl = lax.dot_general(qs.astype(BF16), ks.astype(BF16), _NT, preferred_element_type=F32)
        scores = scores + (lvl if n == HG_TILE else jnp.where(same[n], lvl, 0.0))

    st = st_ref[...]
    b_last = b[HG_TILE - 1:HG_TILE, :]
    o = jnp.dot(scores.astype(BF16), vb, preferred_element_type=F32)
    o = o + lax.dot_general((q * jnp.exp2(b)).astype(BF16), st.astype(BF16), _NT,
                            preferred_element_type=F32)
    kdec = (kk * jnp.exp2(b_last - b)).astype(BF16)
    st_ref[...] = st * jnp.exp2(b_last) + lax.dot_general(vb, kdec, _TN, preferred_element_type=F32)
    return _rms(o, nw) * (gate * jax.nn.sigmoid(gate))


def _hgrn_units(layer, q_ref, f_ref, i_ref, g_ref, lb_ref, nw_ref, o_ref, st_ref):
    lbraw = lb_ref[...]
    e = jnp.exp(lbraw - jnp.max(lbraw, axis=0, keepdims=True))
    soft = e / jnp.sum(e, axis=0, keepdims=True)
    lb = jnp.sum(soft[0:layer + 1, :], axis=0, keepdims=True) - soft[0:1, :]

    row = lax.broadcasted_iota(jnp.int32, (HG_TILE, HG_DIM), 0)
    ti = lax.broadcasted_iota(jnp.int32, (HG_TILE, HG_TILE), 0)
    si = lax.broadcasted_iota(jnp.int32, (HG_TILE, HG_TILE), 1)
    masks = (ti == si, {n: (ti & -n) == (si & -n) for n in _hg_levels() if n < HG_TILE})
    def unit(r0, h):
        rows = slice(r0, r0 + HG_TILE)
        cols = slice(h * HG_DIM, (h + 1) * HG_DIM)
        o_ref[rows, cols] = _hgrn_head(
            q_ref[rows, cols], f_ref[rows, cols], i_ref[rows, cols], g_ref[rows, cols],
            lb[:, cols], nw_ref[...], st_ref.at[h], row, masks).astype(o_ref.dtype)

    return [functools.partial(unit, r0, h)
            for r0 in range(0, o_ref.shape[0], HG_TILE) for h in range(HG_HEADS)]


N_BIAS_TABLES = 5


def _bucket_thresholds():
    max_exact = REL_BUCKETS // 2
    d = np.arange(1, 2 * REL_MAX_DIST, dtype=np.float64)
    large = max_exact + (np.log(d / max_exact) / math.log(REL_MAX_DIST / max_exact)
                         * (REL_BUCKETS - max_exact)).astype(np.int64)
    large = np.minimum(large, REL_BUCKETS - 1)
    bucket = np.where(d < max_exact, d.astype(np.int64), large)
    thr = [0] * REL_BUCKETS
    for bkt in range(1, REL_BUCKETS):
        thr[bkt] = int(d[np.argmax(bucket >= bkt)])
    assert (N_BIAS_TABLES - 1) * MOBA_BLOCK + 1 >= thr[REL_BUCKETS - 1]
    return thr


def _bias_kernel(rb_ref, o_ref):
    thr = _bucket_thresholds()
    key = lax.broadcasted_iota(jnp.int32, (MOBA_BLOCK, MOBA_BLOCK), 0)
    qry = lax.broadcasted_iota(jnp.int32, (MOBA_BLOCK, MOBA_BLOCK), 1)
    o_ref[N_BIAS_TABLES:] = jnp.zeros((N_BIAS_TABLES - 1,) + o_ref.shape[1:], F32)
    for u in range(N_BIAS_TABLES):
        t = N_BIAS_TABLES - 1 - u
        dist = qry - key + t * MOBA_BLOCK
        for h in range(AT_HEADS):
            val = jnp.full((MOBA_BLOCK, MOBA_BLOCK), rb_ref[REL_BUCKETS - 1, h], F32)
            for bkt in range(REL_BUCKETS - 2, -1, -1):
                val = jnp.where(dist < thr[bkt + 1], rb_ref[bkt, h], val)
            val = val * LOG2E
            if t == 0:
                val = jnp.where(dist < 0, NEG, val)
            o_ref[u, :, h * MOBA_BLOCK:(h + 1) * MOBA_BLOCK] = val


def _bias_tables(rel_bias):
    return pl.pallas_call(
        _bias_kernel,
        in_specs=[pl.BlockSpec(memory_space=pltpu.SMEM)],
        out_shape=jax.ShapeDtypeStruct((2 * N_BIAS_TABLES - 1, MOBA_BLOCK, AT_HEADS * MOBA_BLOCK), F32),
        name="moba_bias",
    )(rel_bias)


def _moba_kernel(n_cast, rb_ref, q_ref, k_ref, vt_ref, ksum_ref, bias_ref, *refs):
    cast_in, o_ref, cast_out = refs[:n_cast], refs[n_cast], refs[n_cast + 1:2 * n_cast + 1]
    kmean_ref, mnear_ref, mfar_ref, qs_ref, m_ref, acc_ref, sa_ref, sb_ref = refs[2 * n_cast + 1:]
    for src, dst in zip(cast_in, cast_out):
        dst[...] = src[...].astype(BF16)

    i = pl.program_id(0)
    nblk = k_ref.shape[0]

    @pl.when(i == 0)
    def _():
        ks = ksum_ref[...].reshape(nblk, KSUM_ROWS, AT_WIDTH)
        kmean_ref[...] = jnp.sum(ks, axis=1) * (1.0 / MOBA_BLOCK)

    qt = q_ref[...].T
    zero = jnp.zeros((AT_DIM, MOBA_BLOCK), F32)
    qmt = jnp.concatenate(
        [jnp.concatenate([qt[r * AT_DIM:(r + 1) * AT_DIM, :] if r == h else zero
                          for r in range(AT_HEADS)], axis=0) for h in range(AT_HEADS)], axis=1)
    qs_ref[...] = (qmt * (AT_DIM ** -0.5 * LOG2E)).astype(BF16)

    ncol = AT_HEADS * MOBA_BLOCK
    jio = lax.broadcasted_iota(jnp.int32, (nblk, ncol), 0).astype(F32)
    fi = i.astype(F32)
    n_far = jnp.maximum(i - (N_BIAS_TABLES - 1), 0)
    gate = jnp.dot(kmean_ref[...], qmt, precision=lax.Precision.HIGHEST,
                   preferred_element_type=F32)
    gate = jnp.where(jio < fi, gate, -jnp.inf)
    sel = jio == fi
    for _ in range(MOBA_TOPK):
        mx = jnp.max(gate, axis=0, keepdims=True)
        cand = jnp.where(gate == mx, jio, float(nblk))
        idx = jnp.min(cand, axis=0, keepdims=True)
        pick = (jio == idx) & (mx > -jnp.inf)
        sel = sel | pick
        gate = jnp.where(pick, -jnp.inf, gate)
    col_head = lax.broadcasted_iota(jnp.int32, (1, ncol), 1) // MOBA_BLOCK
    far_bias = jnp.zeros((1, ncol), F32)
    for h in range(AT_HEADS):
        far_bias = jnp.where(col_head == h, rb_ref[REL_BUCKETS - 1, h] * LOG2E, far_bias)
    mnear_ref[...] = jnp.where(sel, 0.0, NEG)
    mfar_ref[...] = jnp.where(sel & (jio < n_far.astype(F32)), far_bias, NEG)
    m_ref[...] = jnp.full(m_ref.shape, NEG, F32)
    acc_ref[...] = jnp.zeros(acc_ref.shape, F32)

    def logits_into(s_ref, j0, nb):
        kg = k_ref[pl.ds(j0, nb)].reshape(nb * MOBA_BLOCK, AT_WIDTH)
        s_ref[0:nb * MOBA_BLOCK, :] = jnp.dot(kg, qs_ref[...], preferred_element_type=F32)

    def fold(s_ref, j0, nb, mask_ref, table, nxt_ref=None, nxt_j0=None):
        if nxt_ref is not None:
            kg = k_ref[pl.ds(nxt_j0, FAR_GROUP)].reshape(FAR_GROUP * MOBA_BLOCK, AT_WIDTH)
        for h in range(AT_HEADS):
            cols = slice(h * MOBA_BLOCK, (h + 1) * MOBA_BLOCK)
            if nxt_ref is not None:
                nxt_ref[0:FAR_GROUP * MOBA_BLOCK, cols] = jnp.dot(
                    kg, qs_ref[:, cols], preferred_element_type=F32)
            m = m_ref[:, cols]
            acc = acc_ref[:, cols]
            for g in range(nb):
                rows = slice(g * MOBA_BLOCK, (g + 1) * MOBA_BLOCK)
                mrow = mask_ref[pl.ds(j0 + g, 1), cols]
                s = s_ref[rows, cols]
                if table is not None:
                    s = s + mrow + bias_ref[table + g, :, cols]
                mb = jnp.max(s, axis=0, keepdims=True)
                p = jnp.exp2(s - mb).astype(BF16)
                if table is None:
                    mb = mb + mrow
                pv = jnp.dot(vt_ref[j0 + g, h], p, preferred_element_type=F32)
                m_new = jnp.maximum(m, mb)
                acc = jnp.exp2(m - m_new) * acc + jnp.exp2(mb - m_new) * pv
                m = m_new
            m_ref[:, cols] = m
            acc_ref[:, cols] = acc

    near0 = jnp.maximum(i - (N_BIAS_TABLES - 1), 0)
    n_groups = (n_far + FAR_GROUP - 1) // FAR_GROUP
    last_group = nblk // FAR_GROUP - 1
    logits_into(sa_ref, near0, N_BIAS_TABLES)
    fold(sa_ref, near0, N_BIAS_TABLES, mnear_ref, near0 - (i - (N_BIAS_TABLES - 1)), sb_ref, 0)

    def far_pair(gp, carry):
        ga = 2 * gp
        fold(sb_ref, FAR_GROUP * ga, FAR_GROUP, mfar_ref, None, sa_ref, FAR_GROUP * (ga + 1))
        fold(sa_ref, FAR_GROUP * (ga + 1), FAR_GROUP, mfar_ref, None,
             sb_ref, FAR_GROUP * jnp.minimum(ga + 2, last_group))
        return carry

    lax.fori_loop(0, (n_groups + 1) // 2, far_pair, 0)
    acc = acc_ref[...]
    ot = acc[0:AT_DIM, :] / acc[AT_DIM:AT_DIM + 1, :]
    o_ref[...] = jnp.concatenate(
        [ot[:, h * MOBA_BLOCK:(h + 1) * MOBA_BLOCK] for h in range(AT_HEADS)], axis=0
    ).T.astype(o_ref.dtype)


def _moba(rel_bias, q, k, vt, ksum, bias, casts=()):
    s = q.shape[0]
    nblk = s // MOBA_BLOCK
    assert nblk >= N_BIAS_TABLES and FAR_GROUP <= N_BIAS_TABLES and nblk % FAR_GROUP == 0
    cast_specs = [_cast_specs(w, layer, nblk) for w, layer in casts]
    outs = pl.pallas_call(
        functools.partial(_moba_kernel, len(casts)),
        grid=(nblk,),
        in_specs=[
            pl.BlockSpec(memory_space=pltpu.SMEM),
            pl.BlockSpec((MOBA_BLOCK, AT_WIDTH), lambda i: (i, 0)),
            _const_spec((nblk, MOBA_BLOCK, AT_WIDTH)),
            _const_spec((nblk, AT_HEADS, V_ROWS, MOBA_BLOCK)),
            _const_spec((nblk * KSUM_ROWS, AT_WIDTH)),
            _const_spec((2 * N_BIAS_TABLES - 1, MOBA_BLOCK, AT_HEADS * MOBA_BLOCK)),
        ] + [c[0] for c in cast_specs],
        out_specs=[pl.BlockSpec((MOBA_BLOCK, AT_WIDTH), lambda i: (i, 0))] + [c[1] for c in cast_specs],
        out_shape=[jax.ShapeDtypeStruct((s, AT_WIDTH), BF16)] + [c[2] for c in cast_specs],
        scratch_shapes=[
            pltpu.VMEM((nblk, AT_WIDTH), F32),
            pltpu.VMEM((nblk, AT_HEADS * MOBA_BLOCK), F32),
            pltpu.VMEM((nblk, AT_HEADS * MOBA_BLOCK), F32),
            pltpu.VMEM((AT_WIDTH, AT_HEADS * MOBA_BLOCK), BF16),
            pltpu.VMEM((1, AT_HEADS * MOBA_BLOCK), F32),
            pltpu.VMEM((V_ROWS, AT_HEADS * MOBA_BLOCK), F32),
            pltpu.VMEM((N_BIAS_TABLES * MOBA_BLOCK, AT_HEADS * MOBA_BLOCK), F32),
            pltpu.VMEM((FAR_GROUP * MOBA_BLOCK, AT_HEADS * MOBA_BLOCK), F32),
        ],
        compiler_params=_params("arbitrary"),
        name="moba",
    )(rel_bias, q, k, vt, ksum, bias, *[w for w, _ in casts])
    return outs[0], outs[1:]


def _mix_rows(x, ohg, oat, zcv, uh, cw, w_ref, nw):
    bgate = zcv[:, 0:CV_WIDTH]
    u = zcv[:, CV_WIDTH:2 * CV_WIDTH] * zcv[:, 2 * CV_WIDTH:3 * CV_WIDTH]
    row = lax.broadcasted_iota(jnp.int32, u.shape, 0)
    u1 = jnp.where(row == 0, uh[7:8, :], pltpu.roll(u, 1, axis=0))
    u2 = jnp.where(row == 0, uh[6:7, :], jnp.where(row == 1, uh[7:8, :], pltpu.roll(u, 2, axis=0)))
    ocv = bgate * (cw[0:1, :] * u2 + cw[1:2, :] * u1 + cw[2:3, :] * u)
    h = jnp.dot(ohg, w_ref[0:HG_WIDTH, :], preferred_element_type=F32)
    h = h + jnp.dot(oat, w_ref[HG_WIDTH:HG_WIDTH + AT_WIDTH, :], preferred_element_type=F32)
    h = h + jnp.dot(ocv.astype(BF16), w_ref[HG_WIDTH + AT_WIDTH:D_MIX, :], preferred_element_type=F32)
    return x + _rms(h, nw)


def _hgmixffn_kernel(layer, x_ref, oat_ref, zcv_ref, halo_ref, cw_ref, wo_ref, nw_ref,
                     wg_ref, wu_ref, wd_ref, q_ref, f_ref, i_ref, g_ref, lb_ref, hnw_ref,
                     o_ref, ohg_ref, st_ref):
    t = pl.program_id(0)

    @pl.when(t == 0)
    def _():
        st_ref[...] = jnp.zeros_like(st_ref)
        ohg_ref[...] = jnp.zeros_like(ohg_ref)

    halo = halo_ref[...]
    uh = halo[:, CV_WIDTH:2 * CV_WIDTH] * halo[:, 2 * CV_WIDTH:3 * CV_WIDTH]
    uh = jnp.where(t > 1, uh, 0.0)
    y = _mix_rows(x_ref[...], ohg_ref[...], oat_ref[...], zcv_ref[...], uh, cw_ref[...],
                  wo_ref, nw_ref[0:1, :])
    units = _hgrn_units(layer, q_ref, f_ref, i_ref, g_ref, lb_ref, hnw_ref, ohg_ref, st_ref)
    per_chunk = -(-len(units) // len(HGFF_CHUNKS))
    xn = _rms(y, nw_ref[1:2, :]).astype(BF16)
    h = None
    for n_chunk, (c0, c1) in enumerate(HGFF_CHUNKS):
        for run in units[n_chunk * per_chunk:(n_chunk + 1) * per_chunk]:
            run()
        g = jnp.dot(xn, wg_ref[:, c0:c1], preferred_element_type=F32)
        u = jnp.dot(xn, wu_ref[:, c0:c1], preferred_element_type=F32)
        a = (g * jax.nn.sigmoid(g) * u).astype(BF16)
        part = jnp.dot(a, wd_ref[c0:c1, :], preferred_element_type=F32)
        h = part if h is None else h + part
    o_ref[...] = y + 0.5 * _rms(h, nw_ref[2:3, :])


def _hgmixffn(x, zhg, oat, zcv, conv_w, w_out, nw3, wg, wu, wd, hg_lb, hg_nw, layer):
    s = x.shape[0]
    n = s // ROW_TILE
    depth = hg_lb.shape[0]
    halo_blocks = ROW_TILE // 8

    def prev(width):
        return pl.BlockSpec((ROW_TILE, width), lambda t: (jnp.maximum(t - 1, 0), 0))

    def cur(k):
        return pl.BlockSpec((ROW_TILE, HG_WIDTH), lambda t: (jnp.minimum(t, n - 1), k))

    return pl.pallas_call(
        functools.partial(_hgmixffn_kernel, layer),
        grid=(n + 1,),
        in_specs=[
            prev(D_MODEL),
            prev(AT_WIDTH),
            prev(3 * CV_WIDTH),
            pl.BlockSpec((8, 3 * CV_WIDTH),
                         lambda t: (jnp.maximum(jnp.maximum(t - 1, 0) * halo_blocks - 1, 0), 0)),
            _const_spec((CV_KERNEL, CV_WIDTH)),
            _const_spec((D_MIX, D_MODEL)),
            _const_spec((3, D_MODEL)),
            _const_spec((D_MODEL, D_FF)),
            _const_spec((D_MODEL, D_FF)),
            _const_spec((D_FF, D_MODEL)),
            cur(0), cur(1), cur(2), cur(3),
            _const_spec((depth, HG_WIDTH)),
            _const_spec((1, HG_DIM)),
        ],
        out_specs=prev(D_MODEL),
        out_shape=jax.ShapeDtypeStruct((s, D_MODEL), F32),
        scratch_shapes=[pltpu.VMEM((ROW_TILE, HG_WIDTH), BF16),
                        pltpu.VMEM((HG_HEADS, HG_DIM, HG_DIM), F32)],
        compiler_params=_params("arbitrary"),
        name="hgmixffn",
    )(x, oat, zcv, zcv, conv_w, w_out, nw3, wg, wu, wd, zhg, zhg, zhg, zhg, hg_lb, hg_nw)


def kernel(x, norm_w, ffn1_wg, ffn1_wu, ffn1_wd, mix_w_in, mix_w_out, hg_lb, hg_norm_w, conv_w,
           ffn2_wg, ffn2_wu, ffn2_wd, rel_bias):
    batch, seq, _ = x.shape
    depth = norm_w.shape[0]
    assert batch == 1 and seq % ROW_TILE == 0 and seq % MOBA_BLOCK == 0
    rel_bias = rel_bias.astype(F32)
    bias = _bias_tables(rel_bias)
    y = x.reshape(seq, D_MODEL)
    early = (ffn1_wg, ffn1_wu, ffn1_wd, mix_w_in)
    late = (mix_w_out, ffn2_wg, ffn2_wu, ffn2_wd)
    wb = {(id(w), 0): w[0].astype(BF16) for w in early}
    casts = [(w, 0) for w in late] + [(w, l) for l in range(1, depth) for w in early + late]
    for l in range(depth):
        def wl(w, l=l):
            return wb[(id(w), l)]

        y = _ffn(y, norm_w[l, 0:2], wl(ffn1_wg), wl(ffn1_wu), wl(ffn1_wd))
        zhg, q, k, vt, ksum, zcv = _inproj(y, norm_w[l, 2:3], wl(mix_w_in))
        oat, cast_out = _moba(rel_bias, q, k, vt, ksum, bias, casts if l == 0 else ())
        if l == 0:
            wb.update({(id(w), cl): o for (w, cl), o in zip(casts, cast_out)})
        y = _hgmixffn(y, zhg, oat, zcv, conv_w[l], wl(mix_w_out), norm_w[l, 3:6],
                      wl(ffn2_wg), wl(ffn2_wu), wl(ffn2_wd), hg_lb, hg_norm_w[l:l + 1], l)
    return y.reshape(batch, seq, D_MODEL)
```

```python
import functools
import math

import numpy as np
import jax
import jax.numpy as jnp
from jax import lax
from jax.experimental import pallas as pl
from jax.experimental.pallas import tpu as pltpu

F32 = jnp.float32
BF16 = jnp.bfloat16

D_MODEL = 1024
D_FF = 2816
HG_HEADS = 4
HG_DIM = 128
HG_WIDTH = HG_HEADS * HG_DIM
AT_HEADS = 4
AT_DIM = 64
AT_WIDTH = AT_HEADS * AT_DIM
MOBA_BLOCK = 256
MOBA_TOPK = 3
REL_BUCKETS = 32
REL_MAX_DIST = 1024
CV_WIDTH = 256
CV_KERNEL = 3
D_MIX = HG_WIDTH + AT_WIDTH + CV_WIDTH
D_IN = 4 * HG_WIDTH + 3 * AT_WIDTH + 3 * CV_WIDTH
EPS = 1e-6

ROW_TILE = 512
FFN_TILE = 1024
FFN_CHAIN = 512
HG_TILE = 256
KSUM_ROWS = 8
V_ROWS = AT_DIM + 16
LOG2E = math.log2(math.e)
FAR_GROUP = 2
NEG = -1e30
VMEM_LIMIT = 56 * 1024 * 1024
FF_CHUNKS = ((0, 1536), (1536, 2816))
HGFF_CHUNKS = tuple((c, c + 256) for c in range(0, 2816, 256))

_NT = (((1,), (1,)), ((), ()))
_TN = (((0,), (0,)), ((), ()))


def _rms(x, w):
    ms = jnp.mean(x * x, axis=-1, keepdims=True)
    return x * lax.rsqrt(ms + EPS) * w


def _const_spec(shape):
    nd = len(shape)
    return pl.BlockSpec(shape, lambda *_: (0,) * nd, pipeline_mode=pl.Buffered(1))


def _cast_specs(w, layer, steps):
    _, rows, cols = w.shape
    span = 1
    while rows % (steps // span) or (rows // (steps // span)) % 16:
        span *= 2
        assert span <= steps
    blk = rows // (steps // span)
    return (pl.BlockSpec((None, blk, cols), lambda i: (layer, i // span, 0)),
            pl.BlockSpec((blk, cols), lambda i: (i // span, 0)),
            jax.ShapeDtypeStruct((rows, cols), BF16))


def _params(*sem):
    return pltpu.CompilerParams(dimension_semantics=sem, vmem_limit_bytes=VMEM_LIMIT)


def _ffn_rows(x, nw_pre, nw_post, wg_ref, wu_ref, wd_ref):
    xn = _rms(x, nw_pre).astype(BF16)
    h = None
    for c0, c1 in FF_CHUNKS:
        g = jnp.dot(xn, wg_ref[:, c0:c1], preferred_element_type=F32)
        u = jnp.dot(xn, wu_ref[:, c0:c1], preferred_element_type=F32)
        a = (g * jax.nn.sigmoid(g) * u).astype(BF16)
        part = jnp.dot(a, wd_ref[c0:c1, :], preferred_element_type=F32)
        h = part if h is None else h + part
    return x + 0.5 * _rms(h, nw_post)


def _ffn_kernel(x_ref, nw_ref, wg_ref, wu_ref, wd_ref, o_ref):
    chains = [slice(r0, r0 + FFN_CHAIN) for r0 in range(0, FFN_TILE, FFN_CHAIN)]
    xs = [x_ref[rows, :] for rows in chains]
    xns = [_rms(x, nw_ref[0:1, :]).astype(BF16) for x in xs]
    hs = [None] * len(chains)
    for c0, c1 in FF_CHUNKS:
        for n, xn in enumerate(xns):
            g = jnp.dot(xn, wg_ref[:, c0:c1], preferred_element_type=F32)
            u = jnp.dot(xn, wu_ref[:, c0:c1], preferred_element_type=F32)
            a = (g * jax.nn.sigmoid(g) * u).astype(BF16)
            part = jnp.dot(a, wd_ref[c0:c1, :], preferred_element_type=F32)
            hs[n] = part if hs[n] is None else hs[n] + part
    for rows, x, h in zip(chains, xs, hs):
        o_ref[rows, :] = x + 0.5 * _rms(h, nw_ref[1:2, :])


def _ffn(x, nw2, wg, wu, wd):
    s = x.shape[0]
    return pl.pallas_call(
        _ffn_kernel,
        grid=(s // FFN_TILE,),
        in_specs=[
            pl.BlockSpec((FFN_TILE, D_MODEL), lambda i: (i, 0)),
            _const_spec((2, D_MODEL)),
            _const_spec((D_MODEL, D_FF)),
            _const_spec((D_MODEL, D_FF)),
            _const_spec((D_FF, D_MODEL)),
        ],
        out_specs=pl.BlockSpec((FFN_TILE, D_MODEL), lambda i: (i, 0)),
        out_shape=jax.ShapeDtypeStruct((s, D_MODEL), F32),
        compiler_params=_params("parallel"),
        name="ffn",
    )(x, nw2, wg, wu, wd)


def _inproj_kernel(x_ref, nw_ref, w_ref, zhg_ref, q_ref, k_ref, vt_ref, ksum_ref, zcv_ref):
    xn = _rms(x_ref[...], nw_ref[...]).astype(BF16)
    c = 4 * HG_WIDTH
    zhg_ref[...] = jnp.dot(xn, w_ref[:, 0:c], preferred_element_type=F32)
    q_ref[...] = jnp.dot(xn, w_ref[:, c:c + AT_WIDTH], preferred_element_type=F32)
    k = jnp.dot(xn, w_ref[:, c + AT_WIDTH:c + 2 * AT_WIDTH], preferred_element_type=F32)
    v = jnp.dot(xn, w_ref[:, c + 2 * AT_WIDTH:c + 3 * AT_WIDTH], preferred_element_type=F32)
    c += 3 * AT_WIDTH
    zcv_ref[...] = jnp.dot(xn, w_ref[:, c:c + 3 * CV_WIDTH], preferred_element_type=F32)
    ones = jnp.ones((V_ROWS - AT_DIM, MOBA_BLOCK), F32)
    for b in range(ROW_TILE // MOBA_BLOCK):
        kb = k[b * MOBA_BLOCK:(b + 1) * MOBA_BLOCK, :]
        vbt = v[b * MOBA_BLOCK:(b + 1) * MOBA_BLOCK, :].T
        k_ref[b] = kb.astype(BF16)
        for h in range(AT_HEADS):
            vt_ref[b, h] = jnp.concatenate(
                [vbt[h * AT_DIM:(h + 1) * AT_DIM, :], ones], axis=0).astype(BF16)
        ksum_ref[b * KSUM_ROWS:(b + 1) * KSUM_ROWS, :] = jnp.sum(
            kb.reshape(MOBA_BLOCK // KSUM_ROWS, KSUM_ROWS, AT_WIDTH), axis=0)


def _inproj(x, nw, w_in):
    s = x.shape[0]
    nblk = s // MOBA_BLOCK
    bpt = ROW_TILE // MOBA_BLOCK
    return pl.pallas_call(
        _inproj_kernel,
        grid=(s // ROW_TILE,),
        in_specs=[
            pl.BlockSpec((ROW_TILE, D_MODEL), lambda i: (i, 0)),
            _const_spec((1, D_MODEL)),
            _const_spec((D_MODEL, D_IN)),
        ],
        out_specs=[
            pl.BlockSpec((ROW_TILE, 4 * HG_WIDTH), lambda i: (i, 0)),
            pl.BlockSpec((ROW_TILE, AT_WIDTH), lambda i: (i, 0)),
            pl.BlockSpec((bpt, MOBA_BLOCK, AT_WIDTH), lambda i: (i, 0, 0)),
            pl.BlockSpec((bpt, AT_HEADS, V_ROWS, MOBA_BLOCK), lambda i: (i, 0, 0, 0)),
            pl.BlockSpec((bpt * KSUM_ROWS, AT_WIDTH), lambda i: (i, 0)),
            pl.BlockSpec((ROW_TILE, 3 * CV_WIDTH), lambda i: (i, 0)),
        ],
        out_shape=[
            jax.ShapeDtypeStruct((s, 4 * HG_WIDTH), F32),
            jax.ShapeDtypeStruct((s, AT_WIDTH), F32),
            jax.ShapeDtypeStruct((nblk, MOBA_BLOCK, AT_WIDTH), BF16),
            jax.ShapeDtypeStruct((nblk, AT_HEADS, V_ROWS, MOBA_BLOCK), BF16),
            jax.ShapeDtypeStruct((nblk * KSUM_ROWS, AT_WIDTH), F32),
            jax.ShapeDtypeStruct((s, 3 * CV_WIDTH), F32),
        ],
        compiler_params=_params("parallel"),
        name="inproj",
    )(x, nw, w_in)


def _hg_levels():
    levels = []
    n = HG_TILE
    while n >= 2:
        levels.append(n)
        n //= 2
    return levels


def _hgrn_head(q, fp, v, gate, lb, nw, st_ref, row, masks):
    u = jnp.exp2(jnp.minimum(fp * -LOG2E, 126.0))
    r = 1.0 / (1.0 + u)
    logf = jnp.log2(lb + (1.0 - lb) * r)
    kk = (1.0 - lb) * (u * r)
    vb = v.astype(BF16)

    b = logf
    sh = 1
    while sh < 8:
        rolled = pltpu.roll(b, sh, axis=0)
        head = jnp.where(row[:8] >= sh, rolled[:8], 0.0)
        b = b + jnp.concatenate([head, rolled[8:]], axis=0)
        sh *= 2
    while sh < HG_TILE:
        b = jnp.concatenate([b[:sh], b[sh:] + b[:-sh]], axis=0)
        sh *= 2

    diag, same = masks
    scores = jnp.where(
        diag, lax.dot_general(q.astype(BF16), kk.astype(BF16), _NT, preferred_element_type=F32), 0.0)
    for n in _hg_levels():
        half = n // 2
        if half >= 8:
            qp, kp, same_up = [], [], []
            zero = jnp.zeros((half, HG_DIM), F32)
            for lo in range(0, HG_TILE, n):
                mid, hi = lo + half, lo + n
                bm = b[mid - 1:mid, :]
                qp.append(q[mid:hi] * jnp.exp2(b[mid:hi] - bm))
                kp += [kk[lo:mid] * jnp.exp2(bm - b[lo:mid]), zero]
                same_up.append(same[n][mid:hi] if n < HG_TILE else None)
            lvl = lax.dot_general(jnp.concatenate(qp, axis=0).astype(BF16),
                                  jnp.concatenate(kp, axis=0).astype(BF16), _NT,
                                  preferred_element_type=F32)
            pieces = []
            for blk, lo in enumerate(range(0, HG_TILE, n)):
                mid, hi = lo + half, lo + n
                part = lvl[blk * half:(blk + 1) * half]
                if n < HG_TILE:
                    part = jnp.where(same_up[blk], part, 0.0)
                pieces += [scores[lo:mid], scores[mid:hi] + part]
            scores = jnp.concatenate(pieces, axis=0)
            continue
        else:
            upper = (row & (n - 1)) >= half
            if n == 2:
                ex = jnp.exp2(jnp.where(upper, logf, 0.0))
            else:
                b3 = b.reshape(HG_TILE // 8, 8, HG_DIM)
                sub = lax.broadcasted_iota(jnp.int32, (HG_TILE // 8, 8, HG_DIM), 1)
                bm3 = None
                for lo in range(0, 8, n):
                    piece = jnp.broadcast_to(b3[:, lo + half - 1:lo + half, :], b3.shape)
                    bm3 = piece if bm3 is None else jnp.where(sub >= lo, piece, bm3)
                ex = jnp.exp2(-jnp.abs(b - bm3.reshape(HG_TILE, HG_DIM)))
            qs = jnp.where(upper, q * ex, 0.0)
            ks = jnp.where(upper, 0.0, kk * ex)
        lvl = lax.dot_general(qs.astype(BF16), ks.astype(BF16), _NT, preferred_element_type=F32)
        scores = scores + (lvl if n == HG_TILE else jnp.where(same[n], lvl, 0.0))

    st = st_ref[...]
    b_last = b[HG_TILE - 1:HG_TILE, :]
    o = jnp.dot(scores.astype(BF16), vb, preferred_element_type=F32)
    o = o + lax.dot_general((q * jnp.exp2(b)).astype(BF16), st.astype(BF16), _NT,
                            preferred_element_type=F32)
    kdec = (kk * jnp.exp2(b_last - b)).astype(BF16)
    st_ref[...] = st * jnp.exp2(b_last) + lax.dot_general(vb, kdec, _TN, preferred_element_type=F32)
    return _rms(o, nw) * (gate * jax.nn.sigmoid(gate))


def _hgrn_units(layer, q_ref, f_ref, i_ref, g_ref, lb_ref, nw_ref, o_ref, st_ref):
    lbraw = lb_ref[...]
    e = jnp.exp(lbraw - jnp.max(lbraw, axis=0, keepdims=True))
    soft = e / jnp.sum(e, axis=0, keepdims=True)
    lb = jnp.sum(soft[0:layer + 1, :], axis=0, keepdims=True) - soft[0:1, :]

    row = lax.broadcasted_iota(jnp.int32, (HG_TILE, HG_DIM), 0)
    ti = lax.broadcasted_iota(jnp.int32, (HG_TILE, HG_TILE), 0)
    si = lax.broadcasted_iota(jnp.int32, (HG_TILE, HG_TILE), 1)
    masks = (ti == si, {n: (ti & -n) == (si & -n) for n in _hg_levels() if n < HG_TILE})
    def unit(r0, h):
        rows = slice(r0, r0 + HG_TILE)
        cols = slice(h * HG_DIM, (h + 1) * HG_DIM)
        o_ref[rows, cols] = _hgrn_head(
            q_ref[rows, cols], f_ref[rows, cols], i_ref[rows, cols], g_ref[rows, cols],
            lb[:, cols], nw_ref[...], st_ref.at[h], row, masks).astype(o_ref.dtype)

    return [functools.partial(unit, r0, h)
            for r0 in range(0, o_ref.shape[0], HG_TILE) for h in range(HG_HEADS)]


N_BIAS_TABLES = 5


def _bucket_thresholds():
    max_exact = REL_BUCKETS // 2
    d = np.arange(1, 2 * REL_MAX_DIST, dtype=np.float64)
    large = max_exact + (np.log(d / max_exact) / math.log(REL_MAX_DIST / max_exact)
                         * (REL_BUCKETS - max_exact)).astype(np.int64)
    large = np.minimum(large, REL_BUCKETS - 1)
    bucket = np.where(d < max_exact, d.astype(np.int64), large)
    thr = [0] * REL_BUCKETS
    for bkt in range(1, REL_BUCKETS):
        thr[bkt] = int(d[np.argmax(bucket >= bkt)])
    assert (N_BIAS_TABLES - 1) * MOBA_BLOCK + 1 >= thr[REL_BUCKETS - 1]
    return thr


def _bias_kernel(rb_ref, o_ref):
    thr = _bucket_thresholds()
    key = lax.broadcasted_iota(jnp.int32, (MOBA_BLOCK, MOBA_BLOCK), 0)
    qry = lax.broadcasted_iota(jnp.int32, (MOBA_BLOCK, MOBA_BLOCK), 1)
    o_ref[N_BIAS_TABLES:] = jnp.zeros((N_BIAS_TABLES - 1,) + o_ref.shape[1:], F32)
    for u in range(N_BIAS_TABLES):
        t = N_BIAS_TABLES - 1 - u
        dist = qry - key + t * MOBA_BLOCK
        for h in range(AT_HEADS):
            val = jnp.full((MOBA_BLOCK, MOBA_BLOCK), rb_ref[REL_BUCKETS - 1, h], F32)
            for bkt in range(REL_BUCKETS - 2, -1, -1):
                val = jnp.where(dist < thr[bkt + 1], rb_ref[bkt, h], val)
            val = val * LOG2E
            if t == 0:
                val = jnp.where(dist < 0, NEG, val)
            o_ref[u, :, h * MOBA_BLOCK:(h + 1) * MOBA_BLOCK] = val


def _bias_tables(rel_bias):
    return pl.pallas_call(
        _bias_kernel,
        in_specs=[pl.BlockSpec(memory_space=pltpu.SMEM)],
        out_shape=jax.ShapeDtypeStruct((2 * N_BIAS_TABLES - 1, MOBA_BLOCK, AT_HEADS * MOBA_BLOCK), F32),
        name="moba_bias",
    )(rel_bias)


def _moba_kernel(n_cast, rb_ref, q_ref, k_ref, vt_ref, ksum_ref, bias_ref, *refs):
    cast_in, o_ref, cast_out = refs[:n_cast], refs[n_cast], refs[n_cast + 1:2 * n_cast + 1]
    kmean_ref, mnear_ref, mfar_ref, qs_ref, m_ref, acc_ref, sa_ref, sb_ref = refs[2 * n_cast + 1:]
    for src, dst in zip(cast_in, cast_out):
        dst[...] = src[...].astype(BF16)

    i = pl.program_id(0)
    nblk = k_ref.shape[0]

    @pl.when(i == 0)
    def _():
        ks = ksum_ref[...].reshape(nblk, KSUM_ROWS, AT_WIDTH)
        kmean_ref[...] = jnp.sum(ks, axis=1) * (1.0 / MOBA_BLOCK)

    qt = q_ref[...].T
    zero = jnp.zeros((AT_DIM, MOBA_BLOCK), F32)
    qmt = jnp.concatenate(
        [jnp.concatenate([qt[r * AT_DIM:(r + 1) * AT_DIM, :] if r == h else zero
                          for r in range(AT_HEADS)], axis=0) for h in range(AT_HEADS)], axis=1)
    qs_ref[...] = (qmt * (AT_DIM ** -0.5 * LOG2E)).astype(BF16)

    ncol = AT_HEADS * MOBA_BLOCK
    jio = lax.broadcasted_iota(jnp.int32, (nblk, ncol), 0).astype(F32)
    fi = i.astype(F32)
    n_far = jnp.maximum(i - (N_BIAS_TABLES - 1), 0)
    gate = jnp.dot(kmean_ref[...], qmt, precision=lax.Precision.HIGHEST,
                   preferred_element_type=F32)
    gate = jnp.where(jio < fi, gate, -jnp.inf)
    sel = jio == fi
    for _ in range(MOBA_TOPK):
        mx = jnp.max(gate, axis=0, keepdims=True)
        cand = jnp.where(gate == mx, jio, float(nblk))
        idx = jnp.min(cand, axis=0, keepdims=True)
        pick = (jio == idx) & (mx > -jnp.inf)
        sel = sel | pick
        gate = jnp.where(pick, -jnp.inf, gate)
    col_head = lax.broadcasted_iota(jnp.int32, (1, ncol), 1) // MOBA_BLOCK
    far_bias = jnp.zeros((1, ncol), F32)
    for h in range(AT_HEADS):
        far_bias = jnp.where(col_head == h, rb_ref[REL_BUCKETS - 1, h] * LOG2E, far_bias)
    mnear_ref[...] = jnp.where(sel, 0.0, NEG)
    mfar_ref[...] = jnp.where(sel & (jio < n_far.astype(F32)), far_bias, NEG)
    m_ref[...] = jnp.full(m_ref.shape, NEG, F32)
    acc_ref[...] = jnp.zeros(acc_ref.shape, F32)

    def fold(s_ref, j0, nb, mask_ref, table, nxt_ref=None, nxt_j0=None):
        if nxt_ref is not None:
            kg = k_ref[pl.ds(nxt_j0, FAR_GROUP)].reshape(FAR_GROUP * MOBA_BLOCK, AT_WIDTH)
        for h in range(AT_HEADS):
            cols = slice(h * MOBA_BLOCK, (h + 1) * MOBA_BLOCK)
            if nxt_ref is not None:
                nxt_ref[0:FAR_GROUP * MOBA_BLOCK, cols] = jnp.dot(
                    kg, qs_ref[:, cols], preferred_element_type=F32)
            m = m_ref[:, cols]
            acc = acc_ref[:, cols]
            for g in range(nb):
                rows = slice(g * MOBA_BLOCK, (g + 1) * MOBA_BLOCK)
                mrow = mask_ref[pl.ds(j0 + g, 1), cols]
                s = s_ref[rows, cols]
                if table is not None:
                    s = s + mrow + bias_ref[table + g, :, cols]
                mb = jnp.max(s, axis=0, keepdims=True)
                p = jnp.exp2(s - mb).astype(BF16)
                if table is None:
                    mb = mb + mrow
                pv = jnp.dot(vt_ref[j0 + g, h], p, preferred_element_type=F32)
                m_new = jnp.maximum(m, mb)
                acc = jnp.exp2(m - m_new) * acc + jnp.exp2(mb - m_new) * pv
                m = m_new
            m_ref[:, cols] = m
            acc_ref[:, cols] = acc

    near0 = jnp.maximum(i - (N_BIAS_TABLES - 1), 0)
    n_groups = (n_far + FAR_GROUP - 1) // FAR_GROUP
    last_group = nblk // FAR_GROUP - 1
    sa_ref[...] = jnp.dot(k_ref[pl.ds(near0, N_BIAS_TABLES)].reshape(N_BIAS_TABLES * MOBA_BLOCK, AT_WIDTH),
                          qs_ref[...], preferred_element_type=F32)
    fold(sa_ref, near0, N_BIAS_TABLES, mnear_ref, near0 - (i - (N_BIAS_TABLES - 1)), sb_ref, 0)

    def far_pair(gp, carry):
        ga = 2 * gp
        fold(sb_ref, FAR_GROUP * ga, FAR_GROUP, mfar_ref, None, sa_ref, FAR_GROUP * (ga + 1))
        fold(sa_ref, FAR_GROUP * (ga + 1), FAR_GROUP, mfar_ref, None,
             sb_ref, FAR_GROUP * jnp.minimum(ga + 2, last_group))
        return carry

    lax.fori_loop(0, (n_groups + 1) // 2, far_pair, 0)
    acc = acc_ref[...]
    ot = acc[0:AT_DIM, :] / acc[AT_DIM:AT_DIM + 1, :]
    o_ref[...] = jnp.concatenate(
        [ot[:, h * MOBA_BLOCK:(h + 1) * MOBA_BLOCK] for h in range(AT_HEADS)], axis=0
    ).T.astype(o_ref.dtype)


def _moba(rel_bias, q, k, vt, ksum, bias, casts=()):
    s = q.shape[0]
    nblk = s // MOBA_BLOCK
    assert nblk >= N_BIAS_TABLES and FAR_GROUP <= N_BIAS_TABLES and nblk % FAR_GROUP == 0
    cast_specs = [_cast_specs(w, layer, nblk) for w, layer in casts]
    outs = pl.pallas_call(
        functools.partial(_moba_kernel, len(casts)),
        grid=(nblk,),
        in_specs=[
            pl.BlockSpec(memory_space=pltpu.SMEM),
            pl.BlockSpec((MOBA_BLOCK, AT_WIDTH), lambda i: (i, 0)),
            _const_spec((nblk, MOBA_BLOCK, AT_WIDTH)),
            _const_spec((nblk, AT_HEADS, V_ROWS, MOBA_BLOCK)),
            _const_spec((nblk * KSUM_ROWS, AT_WIDTH)),
            _const_spec((2 * N_BIAS_TABLES - 1, MOBA_BLOCK, AT_HEADS * MOBA_BLOCK)),
        ] + [c[0] for c in cast_specs],
        out_specs=[pl.BlockSpec((MOBA_BLOCK, AT_WIDTH), lambda i: (i, 0))] + [c[1] for c in cast_specs],
        out_shape=[jax.ShapeDtypeStruct((s, AT_WIDTH), BF16)] + [c[2] for c in cast_specs],
        scratch_shapes=[
            pltpu.VMEM((nblk, AT_WIDTH), F32),
            pltpu.VMEM((nblk, AT_HEADS * MOBA_BLOCK), F32),
            pltpu.VMEM((nblk, AT_HEADS * MOBA_BLOCK), F32),
            pltpu.VMEM((AT_WIDTH, AT_HEADS * MOBA_BLOCK), BF16),
            pltpu.VMEM((1, AT_HEADS * MOBA_BLOCK), F32),
            pltpu.VMEM((V_ROWS, AT_HEADS * MOBA_BLOCK), F32),
            pltpu.VMEM((N_BIAS_TABLES * MOBA_BLOCK, AT_HEADS * MOBA_BLOCK), F32),
            pltpu.VMEM((FAR_GROUP * MOBA_BLOCK, AT_HEADS * MOBA_BLOCK), F32),
        ],
        compiler_params=_params("arbitrary"),
        name="moba",
    )(rel_bias, q, k, vt, ksum, bias, *[w for w, _ in casts])
    return outs[0], outs[1:]


def _mix_rows(x, ohg, oat, zcv, uh, cw, w_ref, nw):
    bgate = zcv[:, 0:CV_WIDTH]
    u = zcv[:, CV_WIDTH:2 * CV_WIDTH] * zcv[:, 2 * CV_WIDTH:3 * CV_WIDTH]
    row = lax.broadcasted_iota(jnp.int32, u.shape, 0)
    u1 = jnp.where(row == 0, uh[7:8, :], pltpu.roll(u, 1, axis=0))
    u2 = jnp.where(row == 0, uh[6:7, :], jnp.where(row == 1, uh[7:8, :], pltpu.roll(u, 2, axis=0)))
    ocv = bgate * (cw[0:1, :] * u2 + cw[1:2, :] * u1 + cw[2:3, :] * u)
    h = jnp.dot(ohg, w_ref[0:HG_WIDTH, :], preferred_element_type=F32)
    h = h + jnp.dot(oat, w_ref[HG_WIDTH:HG_WIDTH + AT_WIDTH, :], preferred_element_type=F32)
    h = h + jnp.dot(ocv.astype(BF16), w_ref[HG_WIDTH + AT_WIDTH:D_MIX, :], preferred_element_type=F32)
    return x + _rms(h, nw)


def _hgmixffn_kernel(layer, x_ref, oat_ref, zcv_ref, halo_ref, cw_ref, wo_ref, nw_ref,
                     wg_ref, wu_ref, wd_ref, q_ref, f_ref, i_ref, g_ref, lb_ref, hnw_ref,
                     o_ref, ohg_ref, st_ref):
    t = pl.program_id(0)

    @pl.when(t == 0)
    def _():
        st_ref[...] = jnp.zeros_like(st_ref)
        ohg_ref[...] = jnp.zeros_like(ohg_ref)

    halo = halo_ref[...]
    uh = halo[:, CV_WIDTH:2 * CV_WIDTH] * halo[:, 2 * CV_WIDTH:3 * CV_WIDTH]
    uh = jnp.where(t > 1, uh, 0.0)
    y = _mix_rows(x_ref[...], ohg_ref[...], oat_ref[...], zcv_ref[...], uh, cw_ref[...],
                  wo_ref, nw_ref[0:1, :])
    units = _hgrn_units(layer, q_ref, f_ref, i_ref, g_ref, lb_ref, hnw_ref, ohg_ref, st_ref)
    per_chunk = -(-len(units) // len(HGFF_CHUNKS))
    xn = _rms(y, nw_ref[1:2, :]).astype(BF16)
    h = None
    for n_chunk, (c0, c1) in enumerate(HGFF_CHUNKS):
        for run in units[n_chunk * per_chunk:(n_chunk + 1) * per_chunk]:
            run()
        g = jnp.dot(xn, wg_ref[:, c0:c1], preferred_element_type=F32)
        u = jnp.dot(xn, wu_ref[:, c0:c1], preferred_element_type=F32)
        a = (g * jax.nn.sigmoid(g) * u).astype(BF16)
        part = jnp.dot(a, wd_ref[c0:c1, :], preferred_element_type=F32)
        h = part if h is None else h + part
    o_ref[...] = y + 0.5 * _rms(h, nw_ref[2:3, :])


def _hgmixffn(x, zhg, oat, zcv, conv_w, w_out, nw3, wg, wu, wd, hg_lb, hg_nw, layer):
    s = x.shape[0]
    n = s // ROW_TILE
    depth = hg_lb.shape[0]
    halo_blocks = ROW_TILE // 8

    def prev(width):
        return pl.BlockSpec((ROW_TILE, width), lambda t: (jnp.maximum(t - 1, 0), 0))

    def cur(k):
        return pl.BlockSpec((ROW_TILE, HG_WIDTH), lambda t: (jnp.minimum(t, n - 1), k))

    return pl.pallas_call(
        functools.partial(_hgmixffn_kernel, layer),
        grid=(n + 1,),
        in_specs=[
            prev(D_MODEL),
            prev(AT_WIDTH),
            prev(3 * CV_WIDTH),
            pl.BlockSpec((8, 3 * CV_WIDTH),
                         lambda t: (jnp.maximum(jnp.maximum(t - 1, 0) * halo_blocks - 1, 0), 0)),
            _const_spec((CV_KERNEL, CV_WIDTH)),
            _const_spec((D_MIX, D_MODEL)),
            _const_spec((3, D_MODEL)),
            _const_spec((D_MODEL, D_FF)),
            _const_spec((D_MODEL, D_FF)),
            _const_spec((D_FF, D_MODEL)),
            cur(0), cur(1), cur(2), cur(3),
            _const_spec((depth, HG_WIDTH)),
            _const_spec((1, HG_DIM)),
        ],
        out_specs=prev(D_MODEL),
        out_shape=jax.ShapeDtypeStruct((s, D_MODEL), F32),
        scratch_shapes=[pltpu.VMEM((ROW_TILE, HG_WIDTH), BF16),
                        pltpu.VMEM((HG_HEADS, HG_DIM, HG_DIM), F32)],
        compiler_params=_params("arbitrary"),
        name="hgmixffn",
    )(x, oat, zcv, zcv, conv_w, w_out, nw3, wg, wu, wd, zhg, zhg, zhg, zhg, hg_lb, hg_nw)


def kernel(x, norm_w, ffn1_wg, ffn1_wu, ffn1_wd, mix_w_in, mix_w_out, hg_lb, hg_norm_w, conv_w,
           ffn2_wg, ffn2_wu, ffn2_wd, rel_bias):
    batch, seq, _ = x.shape
    depth = norm_w.shape[0]
    assert batch == 1 and seq % ROW_TILE == 0 and seq % MOBA_BLOCK == 0
    rel_bias = rel_bias.astype(F32)
    bias = _bias_tables(rel_bias)
    y = x.reshape(seq, D_MODEL)
    early = (ffn1_wg, ffn1_wu, ffn1_wd, mix_w_in)
    late = (mix_w_out, ffn2_wg, ffn2_wu, ffn2_wd)
    wb = {(id(w), 0): w[0].astype(BF16) for w in early}
    casts = [(w, 0) for w in late] + [(w, l) for l in range(1, depth) for w in early + late]
    for l in range(depth):
        def wl(w, l=l):
            return wb[(id(w), l)]

        y = _ffn(y, norm_w[l, 0:2], wl(ffn1_wg), wl(ffn1_wu), wl(ffn1_wd))
        zhg, q, k, vt, ksum, zcv = _inproj(y, norm_w[l, 2:3], wl(mix_w_in))
        oat, cast_out = _moba(rel_bias, q, k, vt, ksum, bias, casts if l == 0 else ())
        if l == 0:
            wb.update({(id(w), cl): o for (w, cl), o in zip(casts, cast_out)})
        y = _hgmixffn(y, zhg, oat, zcv, conv_w[l], wl(mix_w_out), norm_w[l, 3:6],
                      wl(ffn2_wg), wl(ffn2_wu), wl(ffn2_wd), hg_lb, hg_norm_w[l:l + 1], l)
    return y.reshape(batch, seq, D_MODEL)
```

```python
import functools
import math

import numpy as np
import jax
import jax.numpy as jnp
from jax import lax
from jax.experimental import pallas as pl
from jax.experimental.pallas import tpu as pltpu

F32 = jnp.float32
BF16 = jnp.bfloat16

D_MODEL = 1024
D_FF = 2816
HG_HEADS = 4
HG_DIM = 128
HG_WIDTH = HG_HEADS * HG_DIM
AT_HEADS = 4
AT_DIM = 64
AT_WIDTH = AT_HEADS * AT_DIM
MOBA_BLOCK = 256
MOBA_TOPK = 3
REL_BUCKETS = 32
REL_MAX_DIST = 1024
CV_WIDTH = 256
CV_KERNEL = 3
D_MIX = HG_WIDTH + AT_WIDTH + CV_WIDTH
D_IN = 4 * HG_WIDTH + 3 * AT_WIDTH + 3 * CV_WIDTH
EPS = 1e-6

ROW_TILE = 512
FFN_TILE = 1024
FFN_CHAIN = 512
HG_TILE = 256
KSUM_ROWS = 8
V_ROWS = AT_DIM + 16
LOG2E = math.log2(math.e)
FAR_GROUP = 2
NEG = -1e30
VMEM_LIMIT = 56 * 1024 * 1024
FF_CHUNKS = ((0, 1536), (1536, 2816))
HGFF_CHUNKS = tuple((c, c + 256) for c in range(0, 2816, 256))

_NT = (((1,), (1,)), ((), ()))
_TN = (((0,), (0,)), ((), ()))


def _rms(x, w):
    ms = jnp.mean(x * x, axis=-1, keepdims=True)
    return x * lax.rsqrt(ms + EPS) * w


def _const_spec(shape):
    nd = len(shape)
    return pl.BlockSpec(shape, lambda *_: (0,) * nd, pipeline_mode=pl.Buffered(1))


def _cast_specs(w, layer, steps):
    _, rows, cols = w.shape
    span = 1
    while rows % (steps // span) or (rows // (steps // span)) % 16:
        span *= 2
        assert span <= steps
    blk = rows // (steps // span)
    return (pl.BlockSpec((None, blk, cols), lambda i: (layer, i // span, 0)),
            pl.BlockSpec((blk, cols), lambda i: (i // span, 0)),
            jax.ShapeDtypeStruct((rows, cols), BF16))


def _params(*sem):
    return pltpu.CompilerParams(dimension_semantics=sem, vmem_limit_bytes=VMEM_LIMIT)


def _ffn_chunk(xn, h, c0, c1, wg_ref, wu_ref, wd_ref):
    g = jnp.dot(xn, wg_ref[:, c0:c1], preferred_element_type=F32)
    u = jnp.dot(xn, wu_ref[:, c0:c1], preferred_element_type=F32)
    a = (g * jax.nn.sigmoid(g) * u).astype(BF16)
    part = jnp.dot(a, wd_ref[c0:c1, :], preferred_element_type=F32)
    return part if h is None else h + part


def _ffn_kernel(x_ref, nw_ref, wg_ref, wu_ref, wd_ref, o_ref):
    chains = [slice(r0, r0 + FFN_CHAIN) for r0 in range(0, FFN_TILE, FFN_CHAIN)]
    xs = [x_ref[rows, :] for rows in chains]
    xns = [_rms(x, nw_ref[0:1, :]).astype(BF16) for x in xs]
    hs = [None] * len(chains)
    for c0, c1 in FF_CHUNKS:
        for n, xn in enumerate(xns):
            hs[n] = _ffn_chunk(xn, hs[n], c0, c1, wg_ref, wu_ref, wd_ref)
    for rows, x, h in zip(chains, xs, hs):
        o_ref[rows, :] = x + 0.5 * _rms(h, nw_ref[1:2, :])


def _ffn(x, nw2, wg, wu, wd):
    s = x.shape[0]
    return pl.pallas_call(
        _ffn_kernel,
        grid=(s // FFN_TILE,),
        in_specs=[
            pl.BlockSpec((FFN_TILE, D_MODEL), lambda i: (i, 0)),
            _const_spec((2, D_MODEL)),
            _const_spec((D_MODEL, D_FF)),
            _const_spec((D_MODEL, D_FF)),
            _const_spec((D_FF, D_MODEL)),
        ],
        out_specs=pl.BlockSpec((FFN_TILE, D_MODEL), lambda i: (i, 0)),
        out_shape=jax.ShapeDtypeStruct((s, D_MODEL), F32),
        compiler_params=_params("parallel"),
        name="ffn",
    )(x, nw2, wg, wu, wd)


def _inproj_kernel(x_ref, nw_ref, w_ref, zhg_ref, q_ref, k_ref, vt_ref, ksum_ref, zcv_ref):
    xn = _rms(x_ref[...], nw_ref[...]).astype(BF16)
    c = 4 * HG_WIDTH
    zhg_ref[...] = jnp.dot(xn, w_ref[:, 0:c], preferred_element_type=F32)
    q_ref[...] = jnp.dot(xn, w_ref[:, c:c + AT_WIDTH], preferred_element_type=F32)
    k = jnp.dot(xn, w_ref[:, c + AT_WIDTH:c + 2 * AT_WIDTH], preferred_element_type=F32)
    v = jnp.dot(xn, w_ref[:, c + 2 * AT_WIDTH:c + 3 * AT_WIDTH], preferred_element_type=F32)
    c += 3 * AT_WIDTH
    zcv_ref[...] = jnp.dot(xn, w_ref[:, c:c + 3 * CV_WIDTH], preferred_element_type=F32)
    ones = jnp.ones((V_ROWS - AT_DIM, MOBA_BLOCK), F32)
    for b in range(ROW_TILE // MOBA_BLOCK):
        kb = k[b * MOBA_BLOCK:(b + 1) * MOBA_BLOCK, :]
        vbt = v[b * MOBA_BLOCK:(b + 1) * MOBA_BLOCK, :].T
        k_ref[b] = kb.astype(BF16)
        for h in range(AT_HEADS):
            vt_ref[b, h] = jnp.concatenate(
                [vbt[h * AT_DIM:(h + 1) * AT_DIM, :], ones], axis=0).astype(BF16)
        ksum_ref[b * KSUM_ROWS:(b + 1) * KSUM_ROWS, :] = jnp.sum(
            kb.reshape(MOBA_BLOCK // KSUM_ROWS, KSUM_ROWS, AT_WIDTH), axis=0)


def _inproj(x, nw, w_in):
    s = x.shape[0]
    nblk = s // MOBA_BLOCK
    bpt = ROW_TILE // MOBA_BLOCK
    return pl.pallas_call(
        _inproj_kernel,
        grid=(s // ROW_TILE,),
        in_specs=[
            pl.BlockSpec((ROW_TILE, D_MODEL), lambda i: (i, 0)),
            _const_spec((1, D_MODEL)),
            _const_spec((D_MODEL, D_IN)),
        ],
        out_specs=[
            pl.BlockSpec((ROW_TILE, 4 * HG_WIDTH), lambda i: (i, 0)),
            pl.BlockSpec((ROW_TILE, AT_WIDTH), lambda i: (i, 0)),
            pl.BlockSpec((bpt, MOBA_BLOCK, AT_WIDTH), lambda i: (i, 0, 0)),
            pl.BlockSpec((bpt, AT_HEADS, V_ROWS, MOBA_BLOCK), lambda i: (i, 0, 0, 0)),
            pl.BlockSpec((bpt * KSUM_ROWS, AT_WIDTH), lambda i: (i, 0)),
            pl.BlockSpec((ROW_TILE, 3 * CV_WIDTH), lambda i: (i, 0)),
        ],
        out_shape=[
            jax.ShapeDtypeStruct((s, 4 * HG_WIDTH), F32),
            jax.ShapeDtypeStruct((s, AT_WIDTH), F32),
            jax.ShapeDtypeStruct((nblk, MOBA_BLOCK, AT_WIDTH), BF16),
            jax.ShapeDtypeStruct((nblk, AT_HEADS, V_ROWS, MOBA_BLOCK), BF16),
            jax.ShapeDtypeStruct((nblk * KSUM_ROWS, AT_WIDTH), F32),
            jax.ShapeDtypeStruct((s, 3 * CV_WIDTH), F32),
        ],
        compiler_params=_params("parallel"),
        name="inproj",
    )(x, nw, w_in)


def _hg_levels():
    levels = []
    n = HG_TILE
    while n >= 2:
        levels.append(n)
        n //= 2
    return levels


def _hgrn_head(q, fp, v, gate, lb, nw, st_ref, row, masks):
    u = jnp.exp2(jnp.minimum(fp * -LOG2E, 126.0))
    r = 1.0 / (1.0 + u)
    logf = jnp.log2(lb + (1.0 - lb) * r)
    kk = (1.0 - lb) * (u * r)
    vb = v.astype(BF16)

    b = logf
    sh = 1
    while sh < 8:
        rolled = pltpu.roll(b, sh, axis=0)
        head = jnp.where(row[:8] >= sh, rolled[:8], 0.0)
        b = b + jnp.concatenate([head, rolled[8:]], axis=0)
        sh *= 2
    while sh < HG_TILE:
        b = jnp.concatenate([b[:sh], b[sh:] + b[:-sh]], axis=0)
        sh *= 2

    diag, same = masks
    scores = jnp.where(
        diag, lax.dot_general(q.astype(BF16), kk.astype(BF16), _NT, preferred_element_type=F32), 0.0)
    for n in _hg_levels():
        half = n // 2
        if half >= 8:
            qp, kp, same_up = [], [], []
            zero = jnp.zeros((half, HG_DIM), F32)
            for lo in range(0, HG_TILE, n):
                mid, hi = lo + half, lo + n
                bm = b[mid - 1:mid, :]
                qp.append(q[mid:hi] * jnp.exp2(b[mid:hi] - bm))
                kp += [kk[lo:mid] * jnp.exp2(bm - b[lo:mid]), zero]
                same_up.append(same[n][mid:hi] if n < HG_TILE else None)
            lvl = lax.dot_general(jnp.concatenate(qp, axis=0).astype(BF16),
                                  jnp.concatenate(kp, axis=0).astype(BF16), _NT,
                                  preferred_element_type=F32)
            pieces = []
            for blk, lo in enumerate(range(0, HG_TILE, n)):
                mid, hi = lo + half, lo + n
                part = lvl[blk * half:(blk + 1) * half]
                if n < HG_TILE:
                    part = jnp.where(same_up[blk], part, 0.0)
                pieces += [scores[lo:mid], scores[mid:hi] + part]
            scores = jnp.concatenate(pieces, axis=0)
            continue
        else:
            upper = (row & (n - 1)) >= half
            if n == 2:
                ex = jnp.exp2(jnp.where(upper, logf, 0.0))
            else:
                b3 = b.reshape(HG_TILE // 8, 8, HG_DIM)
                sub = lax.broadcasted_iota(jnp.int32, (HG_TILE // 8, 8, HG_DIM), 1)
                bm3 = None
                for lo in range(0, 8, n):
                    piece = jnp.broadcast_to(b3[:, lo + half - 1:lo + half, :], b3.shape)
                    bm3 = piece if bm3 is None else jnp.where(sub >= lo, piece, bm3)
                ex = jnp.exp2(-jnp.abs(b - bm3.reshape(HG_TILE, HG_DIM)))
            qs = jnp.where(upper, q * ex, 0.0)
            ks = jnp.where(upper, 0.0, kk * ex)
        lvl = lax.dot_general(qs.astype(BF16), ks.astype(BF16), _NT, preferred_element_type=F32)
        scores = scores + (lvl if n == HG_TILE else jnp.where(same[n], lvl, 0.0))

    st = st_ref[...]
    b_last = b[HG_TILE - 1:HG_TILE, :]
    o = jnp.dot(scores.astype(BF16), vb, preferred_element_type=F32)
    o = o + lax.dot_general((q * jnp.exp2(b)).astype(BF16), st.astype(BF16), _NT,
                            preferred_element_type=F32)
    kdec = (kk * jnp.exp2(b_last - b)).astype(BF16)
    st_ref[...] = st * jnp.exp2(b_last) + lax.dot_general(vb, kdec, _TN, preferred_element_type=F32)
    return _rms(o, nw) * (gate * jax.nn.sigmoid(gate))


def _hgrn_units(layer, q_ref, f_ref, i_ref, g_ref, lb_ref, nw_ref, o_ref, st_ref):
    lbraw = lb_ref[...]
    e = jnp.exp(lbraw - jnp.max(lbraw, axis=0, keepdims=True))
    soft = e / jnp.sum(e, axis=0, keepdims=True)
    lb = jnp.sum(soft[0:layer + 1, :], axis=0, keepdims=True) - soft[0:1, :]

    row = lax.broadcasted_iota(jnp.int32, (HG_TILE, HG_DIM), 0)
    ti = lax.broadcasted_iota(jnp.int32, (HG_TILE, HG_TILE), 0)
    si = lax.broadcasted_iota(jnp.int32, (HG_TILE, HG_TILE), 1)
    masks = (ti == si, {n: (ti & -n) == (si & -n) for n in _hg_levels() if n < HG_TILE})
    def unit(r0, h):
        rows = slice(r0, r0 + HG_TILE)
        cols = slice(h * HG_DIM, (h + 1) * HG_DIM)
        o_ref[rows, cols] = _hgrn_head(
            q_ref[rows, cols], f_ref[rows, cols], i_ref[rows, cols], g_ref[rows, cols],
            lb[:, cols], nw_ref[...], st_ref.at[h], row, masks).astype(o_ref.dtype)

    return [functools.partial(unit, r0, h)
            for r0 in range(0, o_ref.shape[0], HG_TILE) for h in range(HG_HEADS)]


N_BIAS_TABLES = 5


def _bucket_thresholds():
    max_exact = REL_BUCKETS // 2
    d = np.arange(1, 2 * REL_MAX_DIST, dtype=np.float64)
    large = max_exact + (np.log(d / max_exact) / math.log(REL_MAX_DIST / max_exact)
                         * (REL_BUCKETS - max_exact)).astype(np.int64)
    large = np.minimum(large, REL_BUCKETS - 1)
    bucket = np.where(d < max_exact, d.astype(np.int64), large)
    thr = [0] * REL_BUCKETS
    for bkt in range(1, REL_BUCKETS):
        thr[bkt] = int(d[np.argmax(bucket >= bkt)])
    assert (N_BIAS_TABLES - 1) * MOBA_BLOCK + 1 >= thr[REL_BUCKETS - 1]
    return thr


def _bias_kernel(rb_ref, o_ref):
    thr = _bucket_thresholds()
    key = lax.broadcasted_iota(jnp.int32, (MOBA_BLOCK, MOBA_BLOCK), 0)
    qry = lax.broadcasted_iota(jnp.int32, (MOBA_BLOCK, MOBA_BLOCK), 1)
    o_ref[N_BIAS_TABLES:] = jnp.zeros((N_BIAS_TABLES - 1,) + o_ref.shape[1:], F32)
    for u in range(N_BIAS_TABLES):
        t = N_BIAS_TABLES - 1 - u
        dist = qry - key + t * MOBA_BLOCK
        for h in range(AT_HEADS):
            val = jnp.full((MOBA_BLOCK, MOBA_BLOCK), rb_ref[REL_BUCKETS - 1, h], F32)
            for bkt in range(REL_BUCKETS - 2, -1, -1):
                val = jnp.where(dist < thr[bkt + 1], rb_ref[bkt, h], val)
            val = val * LOG2E
            if t == 0:
                val = jnp.where(dist < 0, NEG, val)
            o_ref[u, :, h * MOBA_BLOCK:(h + 1) * MOBA_BLOCK] = val


def _bias_tables(rel_bias):
    return pl.pallas_call(
        _bias_kernel,
        in_specs=[pl.BlockSpec(memory_space=pltpu.SMEM)],
        out_shape=jax.ShapeDtypeStruct((2 * N_BIAS_TABLES - 1, MOBA_BLOCK, AT_HEADS * MOBA_BLOCK), F32),
        name="moba_bias",
    )(rel_bias)


def _moba_kernel(n_cast, rb_ref, q_ref, k_ref, vt_ref, ksum_ref, bias_ref, *refs):
    cast_in, o_ref, cast_out = refs[:n_cast], refs[n_cast], refs[n_cast + 1:2 * n_cast + 1]
    kmean_ref, mnear_ref, mfar_ref, qs_ref, m_ref, acc_ref, sa_ref, sb_ref = refs[2 * n_cast + 1:]
    for src, dst in zip(cast_in, cast_out):
        dst[...] = src[...].astype(BF16)

    i = pl.program_id(0)
    nblk = k_ref.shape[0]

    @pl.when(i == 0)
    def _():
        ks = ksum_ref[...].reshape(nblk, KSUM_ROWS, AT_WIDTH)
        kmean_ref[...] = jnp.sum(ks, axis=1) * (1.0 / MOBA_BLOCK)

    qt = q_ref[...].T
    zero = jnp.zeros((AT_DIM, MOBA_BLOCK), F32)
    qmt = jnp.concatenate(
        [jnp.concatenate([qt[r * AT_DIM:(r + 1) * AT_DIM, :] if r == h else zero
                          for r in range(AT_HEADS)], axis=0) for h in range(AT_HEADS)], axis=1)
    qs_ref[...] = (qmt * (AT_DIM ** -0.5 * LOG2E)).astype(BF16)

    ncol = AT_HEADS * MOBA_BLOCK
    jio = lax.broadcasted_iota(jnp.int32, (nblk, ncol), 0).astype(F32)
    fi = i.astype(F32)
    n_far = jnp.maximum(i - (N_BIAS_TABLES - 1), 0)
    gate = jnp.dot(kmean_ref[...], qmt, precision=lax.Precision.HIGHEST,
                   preferred_element_type=F32)
    gate = jnp.where(jio < fi, gate, -jnp.inf)
    sel = jio == fi
    for _ in range(MOBA_TOPK):
        mx = jnp.max(gate, axis=0, keepdims=True)
        cand = jnp.where(gate == mx, jio, float(nblk))
        idx = jnp.min(cand, axis=0, keepdims=True)
        pick = (jio == idx) & (mx > -jnp.inf)
        sel = sel | pick
        gate = jnp.where(pick, -jnp.inf, gate)
    col_head = lax.broadcasted_iota(jnp.int32, (1, ncol), 1) // MOBA_BLOCK
    far_bias = jnp.zeros((1, ncol), F32)
    for h in range(AT_HEADS):
        far_bias = jnp.where(col_head == h, rb_ref[REL_BUCKETS - 1, h] * LOG2E, far_bias)
    mnear_ref[...] = jnp.where(sel, 0.0, NEG)
    mfar_ref[...] = jnp.where(sel & (jio < n_far.astype(F32)), far_bias, NEG)
    m_ref[...] = jnp.full(m_ref.shape, NEG, F32)
    acc_ref[...] = jnp.zeros(acc_ref.shape, F32)

    def fold(s_ref, j0, nb, mask_ref, table, nxt_ref=None, nxt_j0=None):
        if nxt_ref is not None:
            kg = k_ref[pl.ds(nxt_j0, FAR_GROUP)].reshape(FAR_GROUP * MOBA_BLOCK, AT_WIDTH)
        for h in range(AT_HEADS):
            cols = slice(h * MOBA_BLOCK, (h + 1) * MOBA_BLOCK)
            if nxt_ref is not None:
                nxt_ref[0:FAR_GROUP * MOBA_BLOCK, cols] = jnp.dot(
                    kg, qs_ref[:, cols], preferred_element_type=F32)
            m = m_ref[:, cols]
            acc = acc_ref[:, cols]
            for g in range(nb):
                rows = slice(g * MOBA_BLOCK, (g + 1) * MOBA_BLOCK)
                mrow = mask_ref[pl.ds(j0 + g, 1), cols]
                s = s_ref[rows, cols]
                if table is not None:
                    s = s + mrow + bias_ref[table + g, :, cols]
                mb = jnp.max(s, axis=0, keepdims=True)
                p = jnp.exp2(s - mb).astype(BF16)
                if table is None:
                    mb = mb + mrow
                pv = jnp.dot(vt_ref[j0 + g, h], p, preferred_element_type=F32)
                m_new = jnp.maximum(m, mb)
                acc = jnp.exp2(m - m_new) * acc + jnp.exp2(mb - m_new) * pv
                m = m_new
            m_ref[:, cols] = m
            acc_ref[:, cols] = acc

    near0 = jnp.maximum(i - (N_BIAS_TABLES - 1), 0)
    n_groups = (n_far + FAR_GROUP - 1) // FAR_GROUP
    last_group = nblk // FAR_GROUP - 1
    sa_ref[...] = jnp.dot(k_ref[pl.ds(near0, N_BIAS_TABLES)].reshape(N_BIAS_TABLES * MOBA_BLOCK, AT_WIDTH),
                          qs_ref[...], preferred_element_type=F32)
    fold(sa_ref, near0, N_BIAS_TABLES, mnear_ref, near0 - (i - (N_BIAS_TABLES - 1)), sb_ref, 0)

    def far_pair(gp, carry):
        ga = 2 * gp
        fold(sb_ref, FAR_GROUP * ga, FAR_GROUP, mfar_ref, None, sa_ref, FAR_GROUP * (ga + 1))
        fold(sa_ref, FAR_GROUP * (ga + 1), FAR_GROUP, mfar_ref, None,
             sb_ref, FAR_GROUP * jnp.minimum(ga + 2, last_group))
        return carry

    lax.fori_loop(0, (n_groups + 1) // 2, far_pair, 0)
    acc = acc_ref[...]
    ot = acc[0:AT_DIM, :] / acc[AT_DIM:AT_DIM + 1, :]
    o_ref[...] = jnp.concatenate(
        [ot[:, h * MOBA_BLOCK:(h + 1) * MOBA_BLOCK] for h in range(AT_HEADS)], axis=0
    ).T.astype(o_ref.dtype)


def _moba(rel_bias, q, k, vt, ksum, bias, casts=()):
    s = q.shape[0]
    nblk = s // MOBA_BLOCK
    assert nblk >= N_BIAS_TABLES and FAR_GROUP <= N_BIAS_TABLES and nblk % FAR_GROUP == 0
    cast_specs = [_cast_specs(w, layer, nblk) for w, layer in casts]
    outs = pl.pallas_call(
        functools.partial(_moba_kernel, len(casts)),
        grid=(nblk,),
        in_specs=[
            pl.BlockSpec(memory_space=pltpu.SMEM),
            pl.BlockSpec((MOBA_BLOCK, AT_WIDTH), lambda i: (i, 0)),
            _const_spec((nblk, MOBA_BLOCK, AT_WIDTH)),
            _const_spec((nblk, AT_HEADS, V_ROWS, MOBA_BLOCK)),
            _const_spec((nblk * KSUM_ROWS, AT_WIDTH)),
            _const_spec((2 * N_BIAS_TABLES - 1, MOBA_BLOCK, AT_HEADS * MOBA_BLOCK)),
        ] + [c[0] for c in cast_specs],
        out_specs=[pl.BlockSpec((MOBA_BLOCK, AT_WIDTH), lambda i: (i, 0))] + [c[1] for c in cast_specs],
        out_shape=[jax.ShapeDtypeStruct((s, AT_WIDTH), BF16)] + [c[2] for c in cast_specs],
        scratch_shapes=[
            pltpu.VMEM((nblk, AT_WIDTH), F32),
            pltpu.VMEM((nblk, AT_HEADS * MOBA_BLOCK), F32),
            pltpu.VMEM((nblk, AT_HEADS * MOBA_BLOCK), F32),
            pltpu.VMEM((AT_WIDTH, AT_HEADS * MOBA_BLOCK), BF16),
            pltpu.VMEM((1, AT_HEADS * MOBA_BLOCK), F32),
            pltpu.VMEM((V_ROWS, AT_HEADS * MOBA_BLOCK), F32),
            pltpu.VMEM((N_BIAS_TABLES * MOBA_BLOCK, AT_HEADS * MOBA_BLOCK), F32),
            pltpu.VMEM((FAR_GROUP * MOBA_BLOCK, AT_HEADS * MOBA_BLOCK), F32),
        ],
        compiler_params=_params("arbitrary"),
        name="moba",
    )(rel_bias, q, k, vt, ksum, bias, *[w for w, _ in casts])
    return outs[0], outs[1:]


def _mix_rows(x, ohg, oat, zcv, uh, cw, w_ref, nw):
    bgate = zcv[:, 0:CV_WIDTH]
    u = zcv[:, CV_WIDTH:2 * CV_WIDTH] * zcv[:, 2 * CV_WIDTH:3 * CV_WIDTH]
    row = lax.broadcasted_iota(jnp.int32, u.shape, 0)
    u1 = jnp.where(row == 0, uh[7:8, :], pltpu.roll(u, 1, axis=0))
    u2 = jnp.where(row == 0, uh[6:7, :], jnp.where(row == 1, uh[7:8, :], pltpu.roll(u, 2, axis=0)))
    ocv = bgate * (cw[0:1, :] * u2 + cw[1:2, :] * u1 + cw[2:3, :] * u)
    h = jnp.dot(ohg, w_ref[0:HG_WIDTH, :], preferred_element_type=F32)
    h = h + jnp.dot(oat, w_ref[HG_WIDTH:HG_WIDTH + AT_WIDTH, :], preferred_element_type=F32)
    h = h + jnp.dot(ocv.astype(BF16), w_ref[HG_WIDTH + AT_WIDTH:D_MIX, :], preferred_element_type=F32)
    return x + _rms(h, nw)


def _hgmixffn_kernel(layer, x_ref, oat_ref, zcv_ref, halo_ref, cw_ref, wo_ref, nw_ref,
                     wg_ref, wu_ref, wd_ref, q_ref, f_ref, i_ref, g_ref, lb_ref, hnw_ref,
                     o_ref, ohg_ref, st_ref):
    t = pl.program_id(0)

    @pl.when(t == 0)
    def _():
        st_ref[...] = jnp.zeros_like(st_ref)
        ohg_ref[...] = jnp.zeros_like(ohg_ref)

    halo = halo_ref[...]
    uh = halo[:, CV_WIDTH:2 * CV_WIDTH] * halo[:, 2 * CV_WIDTH:3 * CV_WIDTH]
    uh = jnp.where(t > 1, uh, 0.0)
    y = _mix_rows(x_ref[...], ohg_ref[...], oat_ref[...], zcv_ref[...], uh, cw_ref[...],
                  wo_ref, nw_ref[0:1, :])
    units = _hgrn_units(layer, q_ref, f_ref, i_ref, g_ref, lb_ref, hnw_ref, ohg_ref, st_ref)
    per_chunk = -(-len(units) // len(HGFF_CHUNKS))
    xn = _rms(y, nw_ref[1:2, :]).astype(BF16)
    h = None
    for n_chunk, (c0, c1) in enumerate(HGFF_CHUNKS):
        for run in units[n_chunk * per_chunk:(n_chunk + 1) * per_chunk]:
            run()
        h = _ffn_chunk(xn, h, c0, c1, wg_ref, wu_ref, wd_ref)
    o_ref[...] = y + 0.5 * _rms(h, nw_ref[2:3, :])


def _hgmixffn(x, zhg, oat, zcv, conv_w, w_out, nw3, wg, wu, wd, hg_lb, hg_nw, layer):
    s = x.shape[0]
    n = s // ROW_TILE
    depth = hg_lb.shape[0]
    halo_blocks = ROW_TILE // 8

    def prev(width):
        return pl.BlockSpec((ROW_TILE, width), lambda t: (jnp.maximum(t - 1, 0), 0))

    def cur(k):
        return pl.BlockSpec((ROW_TILE, HG_WIDTH), lambda t: (jnp.minimum(t, n - 1), k))

    return pl.pallas_call(
        functools.partial(_hgmixffn_kernel, layer),
        grid=(n + 1,),
        in_specs=[
            prev(D_MODEL),
            prev(AT_WIDTH),
            prev(3 * CV_WIDTH),
            pl.BlockSpec((8, 3 * CV_WIDTH),
                         lambda t: (jnp.maximum(jnp.maximum(t - 1, 0) * halo_blocks - 1, 0), 0)),
            _const_spec((CV_KERNEL, CV_WIDTH)),
            _const_spec((D_MIX, D_MODEL)),
            _const_spec((3, D_MODEL)),
            _const_spec((D_MODEL, D_FF)),
            _const_spec((D_MODEL, D_FF)),
            _const_spec((D_FF, D_MODEL)),
            cur(0), cur(1), cur(2), cur(3),
            _const_spec((depth, HG_WIDTH)),
            _const_spec((1, HG_DIM)),
        ],
        out_specs=prev(D_MODEL),
        out_shape=jax.ShapeDtypeStruct((s, D_MODEL), F32),
        scratch_shapes=[pltpu.VMEM((ROW_TILE, HG_WIDTH), BF16),
                        pltpu.VMEM((HG_HEADS, HG_DIM, HG_DIM), F32)],
        compiler_params=_params("arbitrary"),
        name="hgmixffn",
    )(x, oat, zcv, zcv, conv_w, w_out, nw3, wg, wu, wd, zhg, zhg, zhg, zhg, hg_lb, hg_nw)


def kernel(x, norm_w, ffn1_wg, ffn1_wu, ffn1_wd, mix_w_in, mix_w_out, hg_lb, hg_norm_w, conv_w,
           ffn2_wg, ffn2_wu, ffn2_wd, rel_bias):
    batch, seq, _ = x.shape
    depth = norm_w.shape[0]
    assert batch == 1 and seq % ROW_TILE == 0 and seq % MOBA_BLOCK == 0
    rel_bias = rel_bias.astype(F32)
    bias = _bias_tables(rel_bias)
    y = x.reshape(seq, D_MODEL)
    early = (ffn1_wg, ffn1_wu, ffn1_wd, mix_w_in)
    late = (mix_w_out, ffn2_wg, ffn2_wu, ffn2_wd)
    wb = {(id(w), 0): w[0].astype(BF16) for w in early}
    casts = [(w, 0) for w in late] + [(w, l) for l in range(1, depth) for w in early + late]
    for l in range(depth):
        def wl(w, l=l):
            return wb[(id(w), l)]

        y = _ffn(y, norm_w[l, 0:2], wl(ffn1_wg), wl(ffn1_wu), wl(ffn1_wd))
        zhg, q, k, vt, ksum, zcv = _inproj(y, norm_w[l, 2:3], wl(mix_w_in))
        oat, cast_out = _moba(rel_bias, q, k, vt, ksum, bias, casts if l == 0 else ())
        if l == 0:
            wb.update({(id(w), cl): o for (w, cl), o in zip(casts, cast_out)})
        y = _hgmixffn(y, zhg, oat, zcv, conv_w[l], wl(mix_w_out), norm_w[l, 3:6],
                      wl(ffn2_wg), wl(ffn2_wu), wl(ffn2_wd), hg_lb, hg_norm_w[l:l + 1], l)
    return y.reshape(batch, seq, D_MODEL)
```
